```python
import jax
import jax.numpy as jnp
from jax import lax
import numpy as np

D_MODEL = 2048
BATCH = 8
SEQ = 2048
DEPTH = 1
DEC_BATCH = 128
DEC_SEQ = 1
PAST_LEN = 2048
PAGE_SIZE = 128

HEAD_DIM_A = 128
N_HEADS_A = D_MODEL // (2 * HEAD_DIM_A)
MOBA_BLOCK = 256
MOBA_TOPK = 3
MOBA_Q_CHUNK = 16
ROT_DIM = HEAD_DIM_A // 4
ROPE_THETA = 500000.0
N_HEADS_B = 4
DV_B = D_MODEL // (2 * N_HEADS_B)
DK_B = DV_B // 2
GLA_LOWRANK = 16
GLA_TAU = 16.0
GLA_CHUNK = 64
N_EXPERTS = 32
MOE_TOPK = 4
MOE_D_FF = D_MODEL
MOE_BLOCK = 128
SWIGLU_LIMIT = 7.0
SWIGLU_ALPHA = 1.702
PLE_DIM = 256
EPS = 1e-6
NEG_INF = -1e30

kernel_name = "moba_gla_gated_hybrid_moe_decode_step"


def _in_sizes():
    a = N_HEADS_A * HEAD_DIM_A
    return (a, a, a,
            N_HEADS_B * DK_B, N_HEADS_B * DK_B, N_HEADS_B * DV_B, N_HEADS_B * DV_B,
            GLA_LOWRANK, D_MODEL, D_MODEL)


def rmsnorm(x, g=None):
    x32 = x.astype(jnp.float32)
    y = (x32 * lax.rsqrt(jnp.mean(x32 * x32, axis=-1, keepdims=True) + EPS)).astype(x.dtype)
    return y if g is None else y * g


def partial_rope(x, pos):
    half = ROT_DIM // 2
    inv = ROPE_THETA ** (-jnp.arange(half, dtype=jnp.float32) * 2.0 / ROT_DIM)
    ang = pos.astype(jnp.float32)[:, None] * inv[None, :]
    cos = jnp.cos(ang)[None, :, None, :]
    sin = jnp.sin(ang)[None, :, None, :]
    x32 = x.astype(jnp.float32)
    x1 = x32[..., :half]
    x2 = x32[..., half:ROT_DIM]
    out = jnp.concatenate([x1 * cos - x2 * sin, x2 * cos + x1 * sin, x32[..., ROT_DIM:]], axis=-1)
    return out.astype(x.dtype)


def moba_attention(q, k_all, v_all, q_offset):
    B, Sq, H, Dh = q.shape
    L = k_all.shape[1]
    nb = -(-L // MOBA_BLOCK)
    pad = nb * MOBA_BLOCK - L
    kb = jnp.pad(k_all, ((0, 0), (0, pad), (0, 0), (0, 0))).reshape(B, nb, MOBA_BLOCK, H, Dh)
    vb = jnp.pad(v_all, ((0, 0), (0, pad), (0, 0), (0, 0))).reshape(B, nb, MOBA_BLOCK, H, Dh)
    k_mean = jnp.mean(kb.astype(jnp.float32), axis=2)
    qpos = q_offset + jnp.arange(Sq, dtype=jnp.int32)
    qblk = qpos // MOBA_BLOCK
    gate = jnp.einsum('bqhd,bnhd->bhqn', q.astype(jnp.float32), k_mean)
    fully_past = jnp.arange(nb, dtype=jnp.int32)[None, :] < qblk[:, None]
    gate = jnp.where(fully_past[None, None], gate, NEG_INF)
    n_sel = max(1, min(MOBA_TOPK, nb - 1))
    _, top = lax.top_k(gate, n_sel)
    top = top.astype(jnp.int32)
    top_valid = top < qblk[None, None, :, None]
    own = jnp.broadcast_to(qblk[None, None, :, None], (B, H, Sq, 1))
    sel = jnp.concatenate([top, own], axis=-1)
    sel_valid = jnp.concatenate([top_valid, jnp.ones((B, H, Sq, 1), bool)], axis=-1)
    n_slot = n_sel + 1
    kb_h = kb.transpose(0, 3, 1, 2, 4)
    vb_h = vb.transpose(0, 3, 1, 2, 4)
    qc = MOBA_Q_CHUNK if Sq % MOBA_Q_CHUNK == 0 else Sq
    nc = Sq // qc
    scale = HEAD_DIM_A ** -0.5
    gather_blocks = jax.vmap(jax.vmap(lambda a, i: a[i]))

    def to_chunks(a):
        a = a.reshape(a.shape[:2] + (nc, qc) + a.shape[3:])
        return jnp.moveaxis(a, 2, 0)

    def chunk(args):
        q_c, sel_c, val_c, pos_c = args
        flat = sel_c.reshape(B, H, qc * n_slot)
        kg = gather_blocks(kb_h, flat).reshape(B, H, qc, n_slot, MOBA_BLOCK, Dh)
        vg = gather_blocks(vb_h, flat).reshape(B, H, qc, n_slot, MOBA_BLOCK, Dh)
        s = jnp.einsum('bhqd,bhqnkd->bhqnk', q_c.astype(kg.dtype), kg,
                       preferred_element_type=jnp.float32) * scale
        kpos = sel_c[..., None] * MOBA_BLOCK + jnp.arange(MOBA_BLOCK, dtype=jnp.int32)
        mask = val_c[..., None] & (kpos <= pos_c[None, None, :, None, None])
        s = jnp.where(mask, s, NEG_INF)
        pr = jax.nn.softmax(s.reshape(B, H, qc, n_slot * MOBA_BLOCK), axis=-1).reshape(s.shape)
        return jnp.einsum('bhqnk,bhqnkd->bhqd', pr.astype(vg.dtype), vg)

    q_h = q.transpose(0, 2, 1, 3)
    out = lax.map(chunk, (to_chunks(q_h), to_chunks(sel), to_chunks(sel_valid), qpos.reshape(nc, qc)))
    out = jnp.moveaxis(out, 0, 2).reshape(B, H, Sq, Dh)
    return out.transpose(0, 2, 1, 3)


def gla_scan(q, k, v, log_a, s0, chunk):
    B, S, H, dk = q.shape
    dv = v.shape[-1]
    nc = S // chunk

    def blocks(a):
        return a.astype(jnp.float32).reshape(B, nc, chunk, H, a.shape[-1]).transpose(1, 0, 3, 2, 4)

    causal = jnp.tril(jnp.ones((chunk, chunk), bool))

    def step(s, xs):
        qc, kc, vc, ac = xs
        b = jnp.cumsum(ac, axis=2)
        q_t = qc * jnp.exp(b)
        k_t = kc * jnp.exp(-b)
        att = jnp.where(causal, jnp.einsum('bhtd,bhsd->bhts', q_t, k_t), 0.0)
        o = jnp.einsum('bhts,bhsv->bhtv', att, vc) + jnp.einsum('bhtd,bhdv->bhtv', q_t, s)
        b_last = b[:, :, -1:, :]
        k_dec = kc * jnp.exp(b_last - b)
        s_new = jnp.exp(b_last[:, :, 0, :])[..., None] * s + jnp.einsum('bhsd,bhsv->bhdv', k_dec, vc)
        return s_new, o

    s_fin, o = lax.scan(step, s0.astype(jnp.float32), (blocks(q), blocks(k), blocks(v), blocks(log_a)))
    o = o.transpose(1, 0, 3, 2, 4).reshape(B, S, H, dv)
    return o, s_fin


def moe_ffn(x, router_w, router_b, w_gate, b_gate, w_up, b_up, w_down, b_down):
    T, D = x.shape
    logits = (x @ router_w).astype(jnp.float32) + router_b.astype(jnp.float32)
    top_val, top_idx = lax.top_k(logits, MOE_TOPK)
    top_w = jax.nn.softmax(top_val, axis=-1)
    A = T * MOE_TOPK
    flat_e = top_idx.reshape(A).astype(jnp.int32)
    flat_tok = jnp.arange(A, dtype=jnp.int32) // MOE_TOPK
    flat_w = top_w.reshape(A)
    order = jnp.argsort(flat_e, stable=True)
    se = flat_e[order]
    counts = jnp.zeros((N_EXPERTS,), jnp.int32).at[flat_e].add(1)
    start = jnp.cumsum(counts) - counts
    pcounts = (counts + MOE_BLOCK - 1) // MOE_BLOCK * MOE_BLOCK
    pend = jnp.cumsum(pcounts)
    pstart = pend - pcounts
    dest = pstart[se] + jnp.arange(A, dtype=jnp.int32) - start[se]
    n_blocks = -(-A // MOE_BLOCK) + N_EXPERTS
    P = n_blocks * MOE_BLOCK
    slot_tok = jnp.full((P,), T, jnp.int32).at[dest].set(flat_tok[order])
    slot_w = jnp.zeros((P,), jnp.float32).at[dest].set(flat_w[order])
    block_e = jnp.minimum(jnp.searchsorted(pend, jnp.arange(n_blocks, dtype=jnp.int32) * MOE_BLOCK,
                                           side='right'), N_EXPERTS - 1)
    xpad = jnp.concatenate([x, jnp.zeros((1, D), x.dtype)], axis=0)
    xs = xpad[slot_tok].reshape(n_blocks, MOE_BLOCK, D)

    def expert_block(args):
        xb, e = args
        gate = jnp.minimum(xb @ w_gate[e] + b_gate[e], SWIGLU_LIMIT)
        up = jnp.clip(xb @ w_up[e] + b_up[e], -SWIGLU_LIMIT, SWIGLU_LIMIT)
        hid = (up + 1.0) * (gate * jax.nn.sigmoid(SWIGLU_ALPHA * gate))
        return hid @ w_down[e] + b_down[e]

    ys = lax.map(expert_block, (xs, block_e)).reshape(P, D)
    out = jnp.zeros((T + 1, D), ys.dtype).at[slot_tok].add(ys * slot_w[:, None].astype(ys.dtype))
    return out[:T].astype(x.dtype)


def decoder_layer(x, p, pos0, k_past, v_past, s0,
                  norm1_g, w_in, q_norm_g, k_norm_g, gla_w_a2, gla_b_a, gla_norm_g,
                  w_branch_a, w_branch_b, w_out, norm2_g, router_w, router_b,
                  moe_w_gate, moe_b_gate, moe_w_up, moe_b_up, moe_w_down, moe_b_down,
                  ple_w, ple_norm_g, ple_gate_w):
    B, S, D = x.shape
    pos = pos0 + jnp.arange(S, dtype=jnp.int32)
    h = rmsnorm(x, norm1_g)
    proj = h @ w_in
    splits = [int(v) for v in np.cumsum(_in_sizes())[:-1]]
    q_a, k_a, v_a, q_b, k_b, v_b, g_b, a_lr, gate_a, gate_b = jnp.split(proj, splits, axis=-1)
    q_a = partial_rope(rmsnorm(q_a.reshape(B, S, N_HEADS_A, HEAD_DIM_A), q_norm_g), pos)
    k_a = partial_rope(rmsnorm(k_a.reshape(B, S, N_HEADS_A, HEAD_DIM_A), k_norm_g), pos)
    v_a = v_a.reshape(B, S, N_HEADS_A, HEAD_DIM_A)
    if k_past is None:
        k_all, v_all = k_a, v_a
    else:
        k_all = jnp.concatenate([k_past.astype(k_a.dtype), k_a], axis=1)
        v_all = jnp.concatenate([v_past.astype(v_a.dtype), v_a], axis=1)
    o_a = moba_attention(q_a, k_all, v_all, pos0)
    y_a = o_a.reshape(B, S, N_HEADS_A * HEAD_DIM_A) @ w_branch_a
    qb = q_b.reshape(B, S, N_HEADS_B, DK_B) * (DK_B ** -0.5)
    kb = k_b.reshape(B, S, N_HEADS_B, DK_B)
    vb = v_b.reshape(B, S, N_HEADS_B, DV_B)
    log_a = (jax.nn.log_sigmoid((a_lr @ gla_w_a2 + gla_b_a).astype(jnp.float32)) / GLA_TAU
             ).reshape(B, S, N_HEADS_B, DK_B)
    if s0 is None:
        s0 = jnp.zeros((B, N_HEADS_B, DK_B, DV_B), jnp.float32)
    chunk = GLA_CHUNK if S % GLA_CHUNK == 0 else S
    o_b, s_fin = gla_scan(qb, kb, vb, log_a, s0, chunk)
    o_b = rmsnorm(o_b.astype(x.dtype), gla_norm_g) * jax.nn.silu(g_b.reshape(B, S, N_HEADS_B, DV_B))
    y_b = o_b.reshape(B, S, N_HEADS_B * DV_B) @ w_branch_b
    merged = jax.nn.sigmoid(gate_a) * y_a + jax.nn.sigmoid(gate_b) * y_b
    x = x + merged @ w_out
    h2 = rmsnorm(x, norm2_g)
    x = x + moe_ffn(h2.reshape(B * S, D), router_w, router_b, moe_w_gate, moe_b_gate,
                    moe_w_up, moe_b_up, moe_w_down, moe_b_down).reshape(B, S, D)
    e = rmsnorm(p @ ple_w, ple_norm_g)
    x = x + jax.nn.sigmoid(rmsnorm(x) @ ple_gate_w) * e
    return x, k_a, v_a, s_fin.astype(x.dtype)


def setup_inputs(seed: int = 0) -> dict:
    key = jax.random.key(seed)
    ks = jax.random.split(key, 40)
    f32 = jnp.float32

    def nrm(i, shape, scale):
        return scale * jax.random.normal(ks[i], shape, f32)

    n_pages = PAST_LEN // PAGE_SIZE
    n_used = DEC_BATCH * n_pages
    n_phys = n_used + max(1, n_used // 4)
    page_table = jax.random.permutation(ks[0], n_phys)[:n_used].reshape(DEC_BATCH, n_pages).astype(jnp.int32)
    n_in = sum(_in_sizes())
    wa = N_HEADS_A * HEAD_DIM_A
    wb = N_HEADS_B * DV_B
    return {
        "x_prompt": nrm(1, (BATCH, SEQ, D_MODEL), 1.0),
        "x_sample": nrm(2, (DEC_BATCH, DEC_SEQ, D_MODEL), 1.0),
        "cache_k": nrm(3, (DEPTH, n_phys, PAGE_SIZE, N_HEADS_A, HEAD_DIM_A), 1.0),
        "cache_v": nrm(4, (DEPTH, n_phys, PAGE_SIZE, N_HEADS_A, HEAD_DIM_A), 1.0),
        "state_gla": nrm(5, (DEPTH, DEC_BATCH, N_HEADS_B, DK_B, DV_B), 0.5),
        "page_table": page_table,
        "p_prompt": nrm(6, (DEPTH, BATCH, SEQ, PLE_DIM), 1.0),
        "p_sample": nrm(7, (DEPTH, DEC_BATCH, DEC_SEQ, PLE_DIM), 1.0),
        "norm1_g": 1.0 + nrm(8, (DEPTH, D_MODEL), 0.02),
        "w_in": nrm(9, (DEPTH, D_MODEL, n_in), D_MODEL ** -0.5),
        "q_norm_g": 1.0 + nrm(10, (DEPTH, HEAD_DIM_A), 0.02),
        "k_norm_g": 1.0 + nrm(11, (DEPTH, HEAD_DIM_A), 0.02),
        "gla_w_a2": nrm(12, (DEPTH, GLA_LOWRANK, N_HEADS_B * DK_B), GLA_LOWRANK ** -0.5),
        "gla_b_a": nrm(13, (DEPTH, N_HEADS_B * DK_B), 0.1),
        "gla_norm_g": 1.0 + nrm(14, (DEPTH, DV_B), 0.02),
        "w_branch_a": nrm(15, (DEPTH, wa, D_MODEL), wa ** -0.5),
        "w_branch_b": nrm(16, (DEPTH, wb, D_MODEL), wb ** -0.5),
        "w_out": nrm(17, (DEPTH, D_MODEL, D_MODEL), D_MODEL ** -0.5),
        "norm2_g": 1.0 + nrm(18, (DEPTH, D_MODEL), 0.02),
        "router_w": nrm(19, (DEPTH, D_MODEL, N_EXPERTS), D_MODEL ** -0.5),
        "router_b": nrm(20, (DEPTH, N_EXPERTS), 0.01),
        "moe_w_gate": nrm(21, (DEPTH, N_EXPERTS, D_MODEL, MOE_D_FF), D_MODEL ** -0.5),
        "moe_b_gate": nrm(22, (DEPTH, N_EXPERTS, MOE_D_FF), 0.02),
        "moe_w_up": nrm(23, (DEPTH, N_EXPERTS, D_MODEL, MOE_D_FF), D_MODEL ** -0.5),
        "moe_b_up": nrm(24, (DEPTH, N_EXPERTS, MOE_D_FF), 0.02),
        "moe_w_down": nrm(25, (DEPTH, N_EXPERTS, MOE_D_FF, D_MODEL), MOE_D_FF ** -0.5),
        "moe_b_down": nrm(26, (DEPTH, N_EXPERTS, D_MODEL), 0.02),
        "ple_w": nrm(27, (DEPTH, PLE_DIM, D_MODEL), PLE_DIM ** -0.5),
        "ple_norm_g": 1.0 + nrm(28, (DEPTH, D_MODEL), 0.02),
        "ple_gate_w": nrm(29, (DEPTH, D_MODEL, D_MODEL), D_MODEL ** -0.5),
    }


def reference(x_prompt, x_sample, cache_k, cache_v, state_gla, page_table, p_prompt, p_sample,
              norm1_g, w_in, q_norm_g, k_norm_g, gla_w_a2, gla_b_a, gla_norm_g,
              w_branch_a, w_branch_b, w_out, norm2_g, router_w, router_b,
              moe_w_gate, moe_b_gate, moe_w_up, moe_b_up, moe_w_down, moe_b_down,
              ple_w, ple_norm_g, ple_gate_w):
    y_p, y_s = x_prompt, x_sample
    kp, vp, sp, ksl, vsl, ssl = [], [], [], [], [], []
    n_seq = page_table.shape[0]
    for i in range(DEPTH):
        lw = (norm1_g[i], w_in[i], q_norm_g[i], k_norm_g[i], gla_w_a2[i], gla_b_a[i], gla_norm_g[i],
              w_branch_a[i], w_branch_b[i], w_out[i], norm2_g[i], router_w[i], router_b[i],
              moe_w_gate[i], moe_b_gate[i], moe_w_up[i], moe_b_up[i], moe_w_down[i], moe_b_down[i],
              ple_w[i], ple_norm_g[i], ple_gate_w[i])
        y_p, k_new, v_new, s_new = decoder_layer(y_p, p_prompt[i], 0, None, None, None, *lw)
        kp.append(k_new)
        vp.append(v_new)
        sp.append(s_new)
        k_past = cache_k[i][page_table].reshape(n_seq, -1, N_HEADS_A, HEAD_DIM_A)
        v_past = cache_v[i][page_table].reshape(n_seq, -1, N_HEADS_A, HEAD_DIM_A)
        y_s, k_new, v_new, s_new = decoder_layer(y_s, p_sample[i], k_past.shape[1], k_past, v_past,
                                                 state_gla[i], *lw)
        ksl.append(k_new)
        vsl.append(v_new)
        ssl.append(s_new)
    k_prompt = jnp.stack(kp)
    v_prompt = jnp.stack(vp)
    gla_prompt = jnp.stack(sp)
    k_sample = jnp.stack(ksl)
    v_sample = jnp.stack(vsl)
    gla_sample = jnp.stack(ssl)
    return (y_p, y_s, k_prompt, v_prompt, gla_prompt, k_sample, v_sample, gla_sample)
```

```python
import functools

import jax
import jax.numpy as jnp
from jax import lax
from jax.experimental import pallas as pl
from jax.experimental.pallas import tpu as pltpu

HEAD_DIM_A = 128
MOBA_BLOCK = 256
MOBA_TOPK = 3
ROT_DIM = HEAD_DIM_A // 4
ROPE_THETA = 500000.0
N_HEADS_B = 4
GLA_LOWRANK = 16
GLA_TAU = 16.0
GLA_CHUNK = 64
MOE_TOPK = 4
SWIGLU_LIMIT = 7.0
SWIGLU_ALPHA = 1.702
EPS = 1e-6
NEG_INF = -1e30

LANES = 128
MOE_ROW_BLOCK = 256
VMEM_LIMIT = 56 * 1024 * 1024

F32 = jnp.float32
BF16 = jnp.bfloat16


def _params(sem):
    return pltpu.CompilerParams(dimension_semantics=sem, vmem_limit_bytes=VMEM_LIMIT)


def _dot(a, b):
    return jnp.dot(a, b, preferred_element_type=F32)


def _dot_nt(a, b):
    return lax.dot_general(a, b, (((1,), (1,)), ((), ())), preferred_element_type=F32)


def _dot_tn(a, b):
    return lax.dot_general(a, b, (((0,), (0,)), ((), ())), preferred_element_type=F32)


def _split2(x):
    hi = x.astype(BF16)
    lo = (x - hi.astype(F32)).astype(BF16)
    return hi, lo


def _rms(x):
    return x * lax.rsqrt(jnp.mean(x * x, axis=-1, keepdims=True) + EPS)


def _col_from_row(row):
    n = row.shape[1]
    r = lax.broadcasted_iota(jnp.int32, (n, n), 0)
    c = lax.broadcasted_iota(jnp.int32, (n, n), 1)
    return jnp.sum(jnp.where(r == c, jnp.broadcast_to(row, (n, n)), 0.0), axis=1, keepdims=True)


def _log_sigmoid(z):
    return jnp.minimum(z, 0.0) - jnp.log(1.0 + jnp.exp(-jnp.abs(z)))


def _norm_matmul_kernel(x_ref, g_ref, w_ref, o_ref, h_ref):
    @pl.when(pl.program_id(1) == 0)
    def _():
        h_ref[...] = (_rms(x_ref[...]) * g_ref[...]).astype(BF16)

    o_ref[...] = _dot(h_ref[...], w_ref[...])


def _norm_matmul(x, g, w, tm, tn):
    m, d = x.shape
    n = w.shape[1]
    return pl.pallas_call(
        _norm_matmul_kernel,
        grid=(m // tm, n // tn),
        in_specs=[pl.BlockSpec((tm, d), lambda i, j: (i, 0)),
                  pl.BlockSpec((1, d), lambda i, j: (0, 0)),
                  pl.BlockSpec((d, tn), lambda i, j: (0, j))],
        out_specs=pl.BlockSpec((tm, tn), lambda i, j: (i, j)),
        out_shape=jax.ShapeDtypeStruct((m, n), F32),
        scratch_shapes=[pltpu.VMEM((tm, d), BF16)],
        compiler_params=_params(("parallel", "arbitrary")),
        name="in_proj",
    )(x, g, w)


def _qk_rope_kernel(q_ref, k_ref, v_ref, qg_ref, kg_ref, c_ref, sn_ref, sp_ref, qo_ref, ko_ref, vo_ref):
    c, sn, sp = c_ref[...], sn_ref[...], sp_ref[...]
    half = ROT_DIM // 2

    def norm_rope(x, g):
        y = _rms(x) * g
        up = pltpu.roll(y, LANES - half, 1)
        dn = pltpu.roll(y, half, 1)
        return y * c + up * sn + dn * sp

    qo_ref[...] = norm_rope(q_ref[...], qg_ref[...])
    ko_ref[...] = norm_rope(k_ref[...], kg_ref[...])
    vo_ref[...] = v_ref[...]


def _qk_rope(proj, qg, kg, tabs, n_heads, tr, tab_index):
    t = proj.shape[0]
    w = n_heads * HEAD_DIM_A
    col = lambda off: pl.BlockSpec((tr, HEAD_DIM_A), lambda i, j: (i, off + j))
    vec = pl.BlockSpec((1, HEAD_DIM_A), lambda i, j: (0, 0))
    tab = pl.BlockSpec((tr, HEAD_DIM_A), lambda i, j: (tab_index(i), 0))
    out = pl.BlockSpec((tr, HEAD_DIM_A), lambda i, j: (i, j))
    return pl.pallas_call(
        _qk_rope_kernel,
        grid=(t // tr, n_heads),
        in_specs=[col(0), col(n_heads), col(2 * n_heads), vec, vec, tab, tab, tab],
        out_specs=[out, out, out],
        out_shape=[jax.ShapeDtypeStruct((t, w), F32)] * 3,
        compiler_params=_params(("parallel", "parallel")),
        name="qk_rope",
    )(proj, proj, proj, qg, kg, *tabs)


def _rope_tables(pos):
    half = ROT_DIM // 2
    inv = ROPE_THETA ** (-jnp.arange(half, dtype=F32) * 2.0 / ROT_DIM)
    ang = pos.astype(F32)[:, None] * inv[None, :]
    cos, sin = jnp.cos(ang), jnp.sin(ang)
    n = pos.shape[0]
    rest = HEAD_DIM_A - ROT_DIM
    c = jnp.concatenate([cos, cos, jnp.ones((n, rest), F32)], axis=1)
    sn = jnp.concatenate([-sin, jnp.zeros((n, half + rest), F32)], axis=1)
    sp = jnp.concatenate([jnp.zeros((n, half), F32), sin, jnp.zeros((n, rest), F32)], axis=1)
    return c, sn, sp


def _block_select(gate, n_cand, n_sel, max_cand):
    lane = lax.broadcasted_iota(jnp.int32, gate.shape, 1)
    gate = jnp.where(lane < n_cand, gate, -jnp.inf)
    rank = jnp.zeros(gate.shape, F32)
    for m in range(max_cand):
        gm = jnp.broadcast_to(gate[:, m:m + 1], gate.shape)
        beats = (gm > gate) | ((gm == gate) & (lane > m))
        rank = rank + jnp.where(beats, 1.0, 0.0)
    return jnp.where((rank < n_sel) & (lane < n_cand), 1.0, 0.0)


def _moba_prompt_kernel(q_ref, k_ref, v_ref, o_ref, kmean_s, kb_s, vb_s, m_s, l_s, acc_s, *, nb, n_sel):
    qb = pl.program_id(2)
    blk = MOBA_BLOCK

    @pl.when(qb == 0)
    def _():
        kmean_s[...] = jnp.zeros(kmean_s.shape, F32)
        for n in range(nb):
            kmean_s[n:n + 1, :] = jnp.mean(k_ref[n * blk:(n + 1) * blk, :], axis=0, keepdims=True)
        kb_s[...] = k_ref[...].astype(BF16)
        vb_s[...] = v_ref[...].astype(BF16)

    q = q_ref[...]
    qh, ql = _split2(q)
    kh, kl = _split2(kmean_s[...])
    gate = _dot_nt(qh, kh) + _dot_nt(qh, kl) + _dot_nt(ql, kh)
    sel = _block_select(gate, qb, n_sel, nb - 1)

    scale = HEAD_DIM_A ** -0.5
    q16 = q.astype(BF16)
    r0 = pl.multiple_of(qb * blk, blk)
    s = _dot_nt(q16, kb_s[pl.ds(r0, blk), :]) * scale
    row = lax.broadcasted_iota(jnp.int32, (blk, blk), 0)
    colk = lax.broadcasted_iota(jnp.int32, (blk, blk), 1)
    s = jnp.where(colk <= row, s, NEG_INF)
    m0 = jnp.max(s, axis=-1, keepdims=True)
    p = jnp.exp(s - m0)
    m_s[...] = m0
    l_s[...] = jnp.sum(p, axis=-1, keepdims=True)
    acc_s[...] = _dot(p.astype(BF16), vb_s[pl.ds(r0, blk), :])

    for n in range(nb - 1):
        @pl.when(n < qb)
        def _(n=n):
            sn = _dot_nt(q16, kb_s[n * blk:(n + 1) * blk, :]) * scale
            sn = jnp.where(sel[:, n:n + 1] > 0.5, sn, NEG_INF)
            m_prev = m_s[...]
            m_new = jnp.maximum(m_prev, jnp.max(sn, axis=-1, keepdims=True))
            alpha = jnp.exp(m_prev - m_new)
            pn = jnp.exp(sn - m_new)
            l_s[...] = alpha * l_s[...] + jnp.sum(pn, axis=-1, keepdims=True)
            acc_s[...] = alpha * acc_s[...] + _dot(pn.astype(BF16), vb_s[n * blk:(n + 1) * blk, :])
            m_s[...] = m_new

    o_ref[...] = acc_s[...] / l_s[...]


def _moba_prompt(q, k, v, n_heads):
    b, s, _ = q.shape
    blk = MOBA_BLOCK
    nb = s // blk
    n_sel = max(1, min(MOBA_TOPK, nb - 1))
    d = HEAD_DIM_A
    qspec = pl.BlockSpec((None, blk, d), lambda bi, h, i: (bi, i, h))
    kspec = pl.BlockSpec((None, s, d), lambda bi, h, i: (bi, 0, h))
    return pl.pallas_call(
        functools.partial(_moba_prompt_kernel, nb=nb, n_sel=n_sel),
        grid=(b, n_heads, nb),
        in_specs=[qspec, kspec, kspec],
        out_specs=qspec,
        out_shape=jax.ShapeDtypeStruct(q.shape, F32),
        scratch_shapes=[pltpu.VMEM((LANES, d), F32), pltpu.VMEM((s, d), BF16), pltpu.VMEM((s, d), BF16),
                        pltpu.VMEM((blk, 1), F32), pltpu.VMEM((blk, 1), F32), pltpu.VMEM((blk, d), F32)],
        compiler_params=_params(("parallel", "parallel", "arbitrary")),
        name="moba_prompt",
    )(q, k, v)


def _moba_decode_kernel(pt_ref, q_ref, kn_ref, vn_ref, *refs, n_pages, page, n_sel):
    k_refs = refs[:n_pages]
    v_refs = refs[n_pages:2 * n_pages]
    o_ref = refs[2 * n_pages]
    ppb = MOBA_BLOCK // page
    nb = n_pages // ppb
    q = q_ref[...]
    scale = HEAD_DIM_A ** -0.5

    lane = lax.broadcasted_iota(jnp.int32, (q.shape[0], LANES), 1)
    gate = jnp.zeros((q.shape[0], LANES), F32)
    for n in range(nb):
        ksum = jnp.zeros(q.shape, F32)
        for j in range(ppb):
            ksum = ksum + jnp.sum(k_refs[n * ppb + j][...], axis=0)
        g = jnp.sum(q * (ksum * (1.0 / MOBA_BLOCK)), axis=-1, keepdims=True)
        gate = jnp.where(lane == n, g, gate)
    sel = _block_select(gate, nb, n_sel, nb)

    m = jnp.sum(q * kn_ref[...], axis=-1, keepdims=True) * scale
    l = jnp.ones_like(m)
    acc = vn_ref[...]
    for pg in range(n_pages):
        n = pg // ppb
        keep = sel[:, n:n + 1] > 0.5
        kp = k_refs[pg][...]
        s = jnp.sum(kp * q[None], axis=-1, keepdims=True) * scale
        s = jnp.where(keep[None], s, NEG_INF)
        m_new = jnp.maximum(m, jnp.max(s, axis=0))
        alpha = jnp.exp(m - m_new)
        p = jnp.exp(s - m_new[None])
        l = alpha * l + jnp.sum(p, axis=0)
        acc = alpha * acc + jnp.sum(p * v_refs[pg][...], axis=0)
        m = m_new
    o_ref[...] = acc / l


def _moba_decode(q, k_new, v_new, cache_k, cache_v, page_table, n_heads):
    nseq, n_pages = page_table.shape
    page = cache_k.shape[1]
    d = HEAD_DIM_A
    nb_all = -(-(n_pages * page + 1) // MOBA_BLOCK)
    n_sel = max(1, min(MOBA_TOPK, nb_all - 1))
    assert MOBA_BLOCK % page == 0 and (n_pages * page) % MOBA_BLOCK == 0
    hd = pl.BlockSpec((None, n_heads, d), lambda b, pt: (b, 0, 0))

    def page_spec(pg):
        return pl.BlockSpec((None, page, n_heads, d), lambda b, pt: (pt[b * n_pages + pg], 0, 0, 0))

    specs = [page_spec(pg) for pg in range(n_pages)]
    q3 = q.reshape(nseq, n_heads, d)
    return pl.pallas_call(
        functools.partial(_moba_decode_kernel, n_pages=n_pages, page=page, n_sel=n_sel),
        grid_spec=pltpu.PrefetchScalarGridSpec(
            num_scalar_prefetch=1,
            grid=(nseq,),
            in_specs=[hd, hd, hd] + specs + specs,
            out_specs=hd),
        out_shape=jax.ShapeDtypeStruct((nseq, n_heads, d), F32),
        compiler_params=_params(("parallel",)),
        name="moba_decode",
    )(page_table.reshape(-1), q3, k_new.reshape(nseq, n_heads, d), v_new.reshape(nseq, n_heads, d),
      *([cache_k] * n_pages), *([cache_v] * n_pages)).reshape(nseq, n_heads * d)


def _gla_out(o, gn, g):
    return _rms(o) * gn * (g * jax.nn.sigmoid(g))


def _gla_prompt_kernel(q_ref, k_ref, v_ref, g_ref, alr_ref, wa2_ref, ba_ref, gn_ref, o_ref, sfin_ref, la_s, st_s,
                       *, n_chunks):
    c = GLA_CHUNK
    dk = q_ref.shape[1]
    z = _dot(alr_ref[...].astype(BF16), wa2_ref[...].astype(BF16)) + ba_ref[...]
    la_s[...] = _log_sigmoid(z) * (1.0 / GLA_TAU)
    st_s[...] = jnp.zeros(st_s.shape, F32)
    row = lax.broadcasted_iota(jnp.int32, (c, c), 0)
    col = lax.broadcasted_iota(jnp.int32, (c, c), 1)
    causal = col <= row
    tril = causal.astype(BF16)
    gn = gn_ref[...]

    def body(i, carry):
        r0 = pl.multiple_of(i * c, c)
        qc = q_ref[pl.ds(r0, c), :] * (dk ** -0.5)
        kc = k_ref[pl.ds(r0, c), :]
        vc = v_ref[pl.ds(r0, c), :].astype(BF16)
        ac = la_s[pl.ds(r0, c), :]
        a1 = ac.astype(BF16)
        r1 = ac - a1.astype(F32)
        a2 = r1.astype(BF16)
        a3 = (r1 - a2.astype(F32)).astype(BF16)
        b = _dot(tril, a1) + _dot(tril, a2) + _dot(tril, a3)
        q_t = (qc * jnp.exp(b)).astype(BF16)
        k_t = (kc * jnp.exp(-b)).astype(BF16)
        att = jnp.where(causal, _dot_nt(q_t, k_t), 0.0)
        st = st_s[...]
        o = _dot(att.astype(BF16), vc) + _dot(q_t, st.astype(BF16))
        b_last = b[c - 1:c, :]
        k_dec = (kc * jnp.exp(b_last - b)).astype(BF16)
        st_s[...] = _col_from_row(jnp.exp(b_last)) * st + _dot_tn(k_dec, vc)
        o_ref[pl.ds(r0, c), :] = _gla_out(o, gn, g_ref[pl.ds(r0, c), :])
        return carry

    lax.fori_loop(0, n_chunks, body, 0)
    sfin_ref[...] = st_s[...]


def _gla_prompt(proj3, wa2, ba, gn, cols):
    b, s, _ = proj3.shape
    dk, dv = LANES, 2 * LANES
    h = N_HEADS_B
    assert s % GLA_CHUNK == 0
    sk = lambda off: pl.BlockSpec((None, s, dk), lambda bi, hi: (bi, 0, off + hi))
    sv = lambda off: pl.BlockSpec((None, s, dv), lambda bi, hi: (bi, 0, off + hi))
    return pl.pallas_call(
        functools.partial(_gla_prompt_kernel, n_chunks=s // GLA_CHUNK),
        grid=(b, h),
        in_specs=[sk(cols["q_b"] // dk), sk(cols["k_b"] // dk), sv(cols["v_b"] // dv), sv(cols["g_b"] // dv),
                  pl.BlockSpec((None, s, LANES), lambda bi, hi: (bi, 0, cols["a_lr"] // LANES)),
                  pl.BlockSpec((LANES, dk), lambda bi, hi: (0, hi)),
                  pl.BlockSpec((1, dk), lambda bi, hi: (0, hi)),
                  pl.BlockSpec((1, dv), lambda bi, hi: (0, 0))],
        out_specs=[pl.BlockSpec((None, s, dv), lambda bi, hi: (bi, 0, hi)),
                   pl.BlockSpec((None, None, dk, dv), lambda bi, hi: (bi, hi, 0, 0))],
        out_shape=[jax.ShapeDtypeStruct((b, s, h * dv), F32), jax.ShapeDtypeStruct((b, h, dk, dv), F32)],
        scratch_shapes=[pltpu.VMEM((s, dk), F32), pltpu.VMEM((dk, dv), F32)],
        compiler_params=_params(("parallel", "parallel")),
        name="gla_prompt",
    )(proj3, proj3, proj3, proj3, proj3, wa2, ba, gn)


def _gla_decode_kernel(q_ref, k_ref, v_ref, g_ref, alr_ref, wa2_ref, ba_ref, gn_ref, s0_ref, o_ref, s_ref):
    dk, dv = LANES, 2 * LANES
    alr = jnp.broadcast_to(alr_ref[...], (8, LANES)).astype(BF16)
    z = _dot(alr, wa2_ref[...].astype(BF16))[0:1, :] + ba_ref[...]
    a = jnp.exp(_log_sigmoid(z) * (1.0 / GLA_TAU))
    gn = gn_ref[...]
    for h in range(N_HEADS_B):
        ks = slice(h * dk, (h + 1) * dk)
        vs = slice(h * dv, (h + 1) * dv)
        a_col = _col_from_row(a[:, ks])
        k_col = _col_from_row(k_ref[:, ks])
        q_col = _col_from_row(q_ref[:, ks] * (dk ** -0.5))
        s_new = a_col * s0_ref[h] + k_col * v_ref[:, vs]
        s_ref[h] = s_new
        o = jnp.sum(q_col * s_new, axis=0, keepdims=True)
        o_ref[:, vs] = _gla_out(o, gn, g_ref[:, vs])


def _gla_decode(proj_s, wa2, ba, gn, state, cols):
    t, n = proj_s.shape
    dk, dv = LANES, 2 * LANES
    h = N_HEADS_B
    p3 = proj_s.reshape(t, 1, n)
    blk = lambda w, off: pl.BlockSpec((None, 1, w), lambda i: (i, 0, off // w))
    full = lambda a: pl.BlockSpec(a.shape, lambda i: (0,) * a.ndim)
    st = pl.BlockSpec((None, h, dk, dv), lambda i: (i, 0, 0, 0))
    o, s_new = pl.pallas_call(
        _gla_decode_kernel,
        grid=(t,),
        in_specs=[blk(h * dk, cols["q_b"]), blk(h * dk, cols["k_b"]), blk(h * dv, cols["v_b"]),
                  blk(h * dv, cols["g_b"]), blk(LANES, cols["a_lr"]), full(wa2), full(ba), full(gn), st],
        out_specs=[pl.BlockSpec((None, 1, h * dv), lambda i: (i, 0, 0)), st],
        out_shape=[jax.ShapeDtypeStruct((t, 1, h * dv), F32), jax.ShapeDtypeStruct(state.shape, F32)],
        compiler_params=_params(("parallel",)),
        name="gla_decode",
    )(p3, p3, p3, p3, p3, wa2, ba, gn, state)
    return o.reshape(t, h * dv), s_new


def _merge_kernel(oa_ref, ob_ref, ga_ref, gb_ref, x_ref, wa_ref, wb_ref, wo_ref, n2_ref, rwh_ref, rwl_ref, rb_ref,
                  x1_ref, h2_ref, lg_ref):
    ya = _dot(oa_ref[...].astype(BF16), wa_ref[...])
    yb = _dot(ob_ref[...].astype(BF16), wb_ref[...])
    merged = jax.nn.sigmoid(ga_ref[...]) * ya + jax.nn.sigmoid(gb_ref[...]) * yb
    x1 = x_ref[...] + _dot(merged.astype(BF16), wo_ref[...])
    x1_ref[...] = x1
    h2 = _rms(x1) * n2_ref[...]
    h2_ref[...] = h2
    hh, hl = _split2(h2)
    lg_ref[...] = _dot(hh, rwh_ref[...]) + _dot(hl, rwh_ref[...]) + _dot(hh, rwl_ref[...]) + rb_ref[...]


def _merge(o_a, o_b, proj, x, wa, wb, wo, n2, rwh, rwl, rb, cols, tm):
    t, d = x.shape
    row = lambda w, off: pl.BlockSpec((tm, w), lambda i: (i, off // w))
    full = lambda a: pl.BlockSpec(a.shape, lambda i: (0,) * a.ndim)
    return pl.pallas_call(
        _merge_kernel,
        grid=(t // tm,),
        in_specs=[row(o_a.shape[1], 0), row(o_b.shape[1], 0), row(d, cols["gate_a"]), row(d, cols["gate_b"]),
                  row(d, 0), full(wa), full(wb), full(wo), full(n2), full(rwh), full(rwl), full(rb)],
        out_specs=[row(d, 0), row(d, 0), row(LANES, 0)],
        out_shape=[jax.ShapeDtypeStruct((t, d), F32), jax.ShapeDtypeStruct((t, d), F32),
                   jax.ShapeDtypeStruct((t, LANES), F32)],
        compiler_params=_params(("parallel",)),
        name="merge_out_proj",
    )(o_a, o_b, proj, proj, x, wa, wb, wo, n2, rwh, rwl, rb)


def _router_kernel(lg_ref, idx_ref, w_ref, rank_ref, cnt_ref, carry_s):
    @pl.when(pl.program_id(0) == 0)
    def _():
        carry_s[...] = jnp.zeros(carry_s.shape, F32)

    l = lg_ref[...]
    tm = l.shape[0]
    lane = lax.broadcasted_iota(jnp.int32, l.shape, 1).astype(F32)
    vals, idxs = [], []
    for _ in range(MOE_TOPK):
        mx = jnp.max(l, axis=-1, keepdims=True)
        ix = jnp.min(jnp.where(l == mx, lane, float(LANES)), axis=-1, keepdims=True)
        vals.append(mx)
        idxs.append(ix)
        l = jnp.where(lane == ix, -jnp.inf, l)
    es = [jnp.exp(v - vals[0]) for v in vals]
    tot = es[0]
    for e in es[1:]:
        tot = tot + e
    onehot = jnp.zeros(l.shape, F32)
    for ix in idxs:
        onehot = onehot + jnp.where(lane == ix, 1.0, 0.0)
    r = lax.broadcasted_iota(jnp.int32, (tm, tm), 0)
    c = lax.broadcasted_iota(jnp.int32, (tm, tm), 1)
    before = _dot((c < r).astype(BF16), onehot.astype(BF16)) + carry_s[...]
    idx_o = jnp.zeros(l.shape, F32)
    w_o = jnp.zeros(l.shape, F32)
    rank_o = jnp.zeros(l.shape, F32)
    for k in range(MOE_TOPK):
        rk = jnp.sum(jnp.where(lane == idxs[k], before, 0.0), axis=-1, keepdims=True)
        idx_o = jnp.where(lane == k, idxs[k], idx_o)
        w_o = jnp.where(lane == k, es[k] / tot, w_o)
        rank_o = jnp.where(lane == k, rk, rank_o)
    idx_ref[...] = idx_o.astype(jnp.int32)
    w_ref[...] = w_o
    rank_ref[...] = rank_o.astype(jnp.int32)
    carry_s[...] = carry_s[...] + jnp.sum(onehot, axis=0, keepdims=True)
    cnt_ref[...] = carry_s[...].astype(jnp.int32)


def _router(logits, tm):
    t = logits.shape[0]
    row = pl.BlockSpec((tm, LANES), lambda i: (i, 0))
    return pl.pallas_call(
        _router_kernel,
        grid=(t // tm,),
        in_specs=[row],
        out_specs=[row, row, row, pl.BlockSpec((1, LANES), lambda i: (0, 0))],
        out_shape=[jax.ShapeDtypeStruct((t, LANES), jnp.int32), jax.ShapeDtypeStruct((t, LANES), F32),
                   jax.ShapeDtypeStruct((t, LANES), jnp.int32), jax.ShapeDtypeStruct((1, LANES), jnp.int32)],
        scratch_shapes=[pltpu.VMEM((1, LANES), F32)],
        compiler_params=_params(("arbitrary",)),
        name="router",
    )(logits)


def _gather_rows_kernel(tok_ref, h_hbm, o_ref, buf, sem):
    bm = buf.shape[0]
    base = pl.program_id(0) * bm

    def issue(r, carry):
        tok = tok_ref[base + r]
        pltpu.make_async_copy(h_hbm.at[pl.ds(tok, 1), :], buf.at[pl.ds(r, 1), :], sem).start()
        return carry

    lax.fori_loop(0, bm, issue, 0)
    pltpu.make_async_copy(h_hbm.at[pl.ds(0, bm), :], buf, sem).wait()
    o_ref[...] = buf[...].astype(BF16)


def _gather_rows(slot_tok, h, bm):
    p = slot_tok.shape[0]
    d = h.shape[1]
    return pl.pallas_call(
        _gather_rows_kernel,
        grid_spec=pltpu.PrefetchScalarGridSpec(
            num_scalar_prefetch=1,
            grid=(p // bm,),
            in_specs=[pl.BlockSpec(memory_space=pl.ANY)],
            out_specs=pl.BlockSpec((bm, d), lambda i, tok: (i, 0)),
            scratch_shapes=[pltpu.VMEM((bm, d), F32), pltpu.SemaphoreType.DMA(())]),
        out_shape=jax.ShapeDtypeStruct((p, d), BF16),
        compiler_params=_params(("arbitrary",)),
        name="moe_dispatch",
    )(slot_tok, h)


def _expert_changed(be_ref, i):
    prev = be_ref[jnp.maximum(i - 1, 0)]
    return (i == 0) | (be_ref[i] != prev)


def _expert_up_kernel(be_ref, x_ref, wg_ref, wu_ref, bg_ref, bu_ref, o_ref, wg_s, wu_s):
    @pl.when(_expert_changed(be_ref, pl.program_id(1)))
    def _():
        wg_s[...] = wg_ref[...].astype(BF16)
        wu_s[...] = wu_ref[...].astype(BF16)

    x = x_ref[...]
    gate = jnp.minimum(_dot(x, wg_s[...]) + bg_ref[...], SWIGLU_LIMIT)
    up = jnp.clip(_dot(x, wu_s[...]) + bu_ref[...], -SWIGLU_LIMIT, SWIGLU_LIMIT)
    o_ref[...] = ((up + 1.0) * (gate * jax.nn.sigmoid(SWIGLU_ALPHA * gate))).astype(BF16)


def _expert_down_kernel(be_ref, h_ref, wd_ref, bd_ref, o_ref, wd_s):
    @pl.when(_expert_changed(be_ref, pl.program_id(1)))
    def _():
        wd_s[...] = wd_ref[...].astype(BF16)

    o_ref[...] = _dot(h_ref[...], wd_s[...]) + bd_ref[...]


def _expert_up(block_e, xs, wg, wu, bg, bu, bm, tf):
    p, d = xs.shape
    f = wg.shape[2]
    wspec = pl.BlockSpec((None, d, tf), lambda j, i, be: (be[i], 0, j))
    bspec = pl.BlockSpec((None, 1, tf), lambda j, i, be: (be[i], 0, j))
    return pl.pallas_call(
        _expert_up_kernel,
        grid_spec=pltpu.PrefetchScalarGridSpec(
            num_scalar_prefetch=1,
            grid=(f // tf, p // bm),
            in_specs=[pl.BlockSpec((bm, d), lambda j, i, be: (i, 0)), wspec, wspec, bspec, bspec],
            out_specs=pl.BlockSpec((bm, tf), lambda j, i, be: (i, j)),
            scratch_shapes=[pltpu.VMEM((d, tf), BF16), pltpu.VMEM((d, tf), BF16)]),
        out_shape=jax.ShapeDtypeStruct((p, f), BF16),
        compiler_params=_params(("arbitrary", "arbitrary")),
        name="moe_up",
    )(block_e, xs, wg, wu, bg, bu)


def _expert_down(block_e, hid, wd, bd, bm, tn):
    p, f = hid.shape
    d = wd.shape[2]
    return pl.pallas_call(
        _expert_down_kernel,
        grid_spec=pltpu.PrefetchScalarGridSpec(
            num_scalar_prefetch=1,
            grid=(d // tn, p // bm),
            in_specs=[pl.BlockSpec((bm, f), lambda j, i, be: (i, 0)),
                      pl.BlockSpec((None, f, tn), lambda j, i, be: (be[i], 0, j)),
                      pl.BlockSpec((None, 1, tn), lambda j, i, be: (be[i], 0, j))],
            out_specs=pl.BlockSpec((bm, tn), lambda j, i, be: (i, j)),
            scratch_shapes=[pltpu.VMEM((f, tn), BF16)]),
        out_shape=jax.ShapeDtypeStruct((p, d), F32),
        compiler_params=_params(("arbitrary", "arbitrary")),
        name="moe_down",
    )(block_e, hid, wd, bd)


def _combine_kernel(dest_ref, ys_hbm, x1_ref, w_ref, p_ref, pw_ref, pg_ref, gw_ref, y_ref, buf, sem):
    tm = x1_ref.shape[0]
    base = pl.program_id(0) * tm * MOE_TOPK

    def issue(r, carry):
        for k in range(MOE_TOPK):
            dst = dest_ref[base + r * MOE_TOPK + k]
            pltpu.make_async_copy(ys_hbm.at[pl.ds(dst, 1), :], buf.at[pl.ds(k * tm + r, 1), :], sem).start()
        return carry

    lax.fori_loop(0, tm, issue, 0)
    e = _rms(_dot(p_ref[...].astype(BF16), pw_ref[...])) * pg_ref[...]
    pltpu.make_async_copy(ys_hbm.at[pl.ds(0, MOE_TOPK * tm), :], buf, sem).wait()
    w = w_ref[...]
    x2 = x1_ref[...]
    for k in range(MOE_TOPK):
        x2 = x2 + w[:, k:k + 1] * buf[k * tm:(k + 1) * tm, :]
    gate = jax.nn.sigmoid(_dot(_rms(x2).astype(BF16), gw_ref[...]))
    y_ref[...] = x2 + gate * e


def _combine(dest, ys, x1, w, p, pw, pg, gw, tm):
    t, d = x1.shape
    full = lambda a: pl.BlockSpec(a.shape, lambda i, dst: (0,) * a.ndim)
    row = lambda width: pl.BlockSpec((tm, width), lambda i, dst: (i, 0))
    return pl.pallas_call(
        _combine_kernel,
        grid_spec=pltpu.PrefetchScalarGridSpec(
            num_scalar_prefetch=1,
            grid=(t // tm,),
            in_specs=[pl.BlockSpec(memory_space=pl.ANY), row(d), row(LANES), row(p.shape[1]),
                      full(pw), full(pg), full(gw)],
            out_specs=row(d),
            scratch_shapes=[pltpu.VMEM((MOE_TOPK * tm, d), F32), pltpu.SemaphoreType.DMA(())]),
        out_shape=jax.ShapeDtypeStruct((t, d), F32),
        compiler_params=_params(("arbitrary",)),
        name="moe_combine_ple",
    )(dest, ys, x1, w, p, pw, pg, gw)


def _largest_tile(n, cap, quantum):
    best = None
    for t in range(quantum, cap + 1, quantum):
        if n % t == 0:
            best = t
    assert best is not None, (n, cap, quantum)
    return best


def kernel(x_prompt, x_sample, cache_k, cache_v, state_gla, page_table, p_prompt, p_sample, norm1_g, w_in, q_norm_g, k_norm_g, gla_w_a2, gla_b_a, gla_norm_g, w_branch_a, w_branch_b, w_out, norm2_g, router_w, router_b, moe_w_gate, moe_b_gate, moe_w_up, moe_b_up, moe_w_down, moe_b_down, ple_w, ple_norm_g, ple_gate_w):
    assert norm1_g.shape[0] == 1, "one layer"
    b, s, d = x_prompt.shape
    nseq = x_sample.shape[0]
    assert x_sample.shape[1] == 1
    n_heads = d // (2 * HEAD_DIM_A)
    wa = n_heads * HEAD_DIM_A
    kb_w = N_HEADS_B * LANES
    vb_w = 2 * kb_w
    n_exp = router_w.shape[2]
    tp, ts = b * s, nseq

    sizes = [("q_a", wa), ("k_a", wa), ("v_a", wa), ("q_b", kb_w), ("k_b", kb_w), ("v_b", vb_w), ("g_b", vb_w),
             ("a_lr", GLA_LOWRANK), ("gate_a", d), ("gate_b", d)]
    src, off = {}, 0
    for name, width in sizes:
        src[name] = (off, width)
        off += width
    order = [n for n, _ in sizes if n != "a_lr"] + ["a_lr"]
    cols, parts, off = {}, [], 0
    w_in0 = w_in[0]
    for name in order:
        o0, width = src[name]
        part = w_in0[:, o0:o0 + width]
        if name == "a_lr":
            part = jnp.pad(part, ((0, 0), (0, LANES - width)))
            width = LANES
        cols[name] = off
        parts.append(part)
        off += width
    w_main = jnp.concatenate(parts, axis=1).astype(BF16)
    n_main = w_main.shape[1]
    tn = _largest_tile(n_main, 1280, LANES)

    g1 = norm1_g[0][None, :]
    qg, kg = q_norm_g[0][None, :], k_norm_g[0][None, :]
    wa2 = jnp.pad(gla_w_a2[0], ((0, LANES - GLA_LOWRANK), (0, 0)))
    ba = gla_b_a[0][None, :]
    gn = gla_norm_g[0][None, :]
    wba, wbb, wo = w_branch_a[0].astype(BF16), w_branch_b[0].astype(BF16), w_out[0].astype(BF16)
    n2 = norm2_g[0][None, :]
    rw = jnp.pad(router_w[0], ((0, 0), (0, LANES - n_exp)))
    rwh = rw.astype(BF16)
    rwl = (rw - rwh.astype(F32)).astype(BF16)
    rb = jnp.concatenate([router_b[0], jnp.full((LANES - n_exp,), -jnp.inf, F32)])[None, :]

    xp = x_prompt.reshape(tp, d)
    proj_p = _norm_matmul(xp, g1, w_main, _largest_tile(tp, 512, 8), tn)
    tr = _largest_tile(s, 512, 8)
    tabs_p = _rope_tables(jnp.arange(s, dtype=jnp.int32))
    q_p, k_p, v_p = _qk_rope(proj_p, qg, kg, tabs_p, n_heads, tr, lambda i: i % (s // tr))
    o_a_p = _moba_prompt(q_p.reshape(b, s, wa), k_p.reshape(b, s, wa), v_p.reshape(b, s, wa), n_heads)
    o_b_p, gla_p = _gla_prompt(proj_p.reshape(b, s, n_main), wa2, ba, gn, cols)
    x1_p, h2_p, lg_p = _merge(o_a_p.reshape(tp, wa), o_b_p.reshape(tp, vb_w), proj_p, xp, wba, wbb, wo, n2,
                              rwh, rwl, rb, cols, _largest_tile(tp, 128, 8))

    xs_ = x_sample.reshape(ts, d)
    past = page_table.shape[1] * cache_k.shape[2]
    proj_s = _norm_matmul(xs_, g1, w_main, ts, tn)
    tabs_s = tuple(jnp.broadcast_to(t_, (ts, HEAD_DIM_A)) for t_ in _rope_tables(jnp.full((1,), past, jnp.int32)))
    q_s, k_s, v_s = _qk_rope(proj_s, qg, kg, tabs_s, n_heads, ts, lambda i: i)
    o_a_s = _moba_decode(q_s, k_s, v_s, cache_k[0], cache_v[0], page_table, n_heads)
    o_b_s, gla_s = _gla_decode(proj_s, wa2, ba, gn, state_gla[0], cols)
    x1_s, h2_s, lg_s = _merge(o_a_s, o_b_s, proj_s, xs_, wba, wbb, wo, n2, rwh, rwl, rb, cols, ts)

    t_all = tp + ts
    h2 = jnp.concatenate([h2_p, h2_s], axis=0)
    logits = jnp.concatenate([lg_p, lg_s], axis=0)
    idx_o, w_o, rank_o, cnt = _router(logits, _largest_tile(t_all, 256, LANES))
    bm = MOE_ROW_BLOCK
    n_assign = t_all * MOE_TOPK
    n_blocks = -(-n_assign // bm) + n_exp
    n_slots = n_blocks * bm
    counts = cnt[0, :n_exp]
    pcounts = (counts + bm - 1) // bm * bm
    pend = jnp.cumsum(pcounts)
    pstart = pend - pcounts
    idx = idx_o[:, :MOE_TOPK]
    dest = (pstart[idx] + rank_o[:, :MOE_TOPK]).reshape(-1)
    slot_tok = jnp.zeros((n_slots,), jnp.int32).at[dest].set(jnp.arange(n_assign, dtype=jnp.int32) // MOE_TOPK)
    block_e = jnp.minimum(jnp.searchsorted(pend, jnp.arange(n_blocks, dtype=jnp.int32) * bm, side="right"),
                          n_exp - 1).astype(jnp.int32)
    xs_rows = _gather_rows(slot_tok, h2, bm)
    f = moe_w_gate.shape[3]
    hid = _expert_up(block_e, xs_rows, moe_w_gate[0], moe_w_up[0], moe_b_gate[0][:, None, :],
                     moe_b_up[0][:, None, :], bm, _largest_tile(f, 512, LANES))
    ys = _expert_down(block_e, hid, moe_w_down[0], moe_b_down[0][:, None, :], bm, _largest_tile(d, 512, LANES))

    pw, pg, gw = ple_w[0].astype(BF16), ple_norm_g[0][None, :], ple_gate_w[0].astype(BF16)
    tc = _largest_tile(ts, 128, 8)
    assert tp % tc == 0
    y_p = _combine(dest[:tp * MOE_TOPK], ys, x1_p, w_o[:tp], p_prompt[0].reshape(tp, -1), pw, pg, gw, tc)
    y_s = _combine(dest[tp * MOE_TOPK:], ys, x1_s, w_o[tp:], p_sample[0].reshape(ts, -1), pw, pg, gw, tc)

    hd = (n_heads, HEAD_DIM_A)
    return (y_p.reshape(b, s, d), y_s.reshape(nseq, 1, d),
            k_p.reshape(1, b, s, *hd), v_p.reshape(1, b, s, *hd), gla_p[None],
            k_s.reshape(1, nseq, 1, *hd), v_s.reshape(1, nseq, 1, *hd), gla_s[None])
```

```python
import functools

import jax
import jax.numpy as jnp
from jax import lax
from jax.experimental import pallas as pl
from jax.experimental.pallas import tpu as pltpu

HEAD_DIM_A = 128
MOBA_BLOCK = 256
MOBA_TOPK = 3
ROT_DIM = HEAD_DIM_A // 4
ROPE_THETA = 500000.0
N_HEADS_B = 4
GLA_LOWRANK = 16
GLA_TAU = 16.0
GLA_CHUNK = 64
MOE_TOPK = 4
SWIGLU_LIMIT = 7.0
SWIGLU_ALPHA = 1.702
EPS = 1e-6
NEG_INF = -1e30

LANES = 128
MOE_ROW_BLOCK = 256
VMEM_LIMIT = 56 * 1024 * 1024

F32 = jnp.float32
BF16 = jnp.bfloat16


def _params(sem):
    return pltpu.CompilerParams(dimension_semantics=sem, vmem_limit_bytes=VMEM_LIMIT)


def _dot(a, b):
    return jnp.dot(a, b, preferred_element_type=F32)


def _dot_nt(a, b):
    return lax.dot_general(a, b, (((1,), (1,)), ((), ())), preferred_element_type=F32)


def _dot_tn(a, b):
    return lax.dot_general(a, b, (((0,), (0,)), ((), ())), preferred_element_type=F32)


def _split2(x):
    hi = x.astype(BF16)
    lo = (x - hi.astype(F32)).astype(BF16)
    return hi, lo


def _rms(x):
    return x * lax.rsqrt(jnp.mean(x * x, axis=-1, keepdims=True) + EPS)


def _col_from_row(row):
    n = row.shape[1]
    r = lax.broadcasted_iota(jnp.int32, (n, n), 0)
    c = lax.broadcasted_iota(jnp.int32, (n, n), 1)
    return jnp.sum(jnp.where(r == c, jnp.broadcast_to(row, (n, n)), 0.0), axis=1, keepdims=True)


def _log_sigmoid(z):
    return jnp.minimum(z, 0.0) - jnp.log(1.0 + jnp.exp(-jnp.abs(z)))


def _pack_bf16_pairs(x16):
    half = x16.shape[1] // 2
    bits = lax.bitcast_convert_type(x16.astype(F32), jnp.uint32)
    return (bits[:, :half] >> 16) | bits[:, half:]


def _unpack_bf16_pairs(u):
    lo = lax.bitcast_convert_type(u << 16, F32).astype(BF16)
    hi = lax.bitcast_convert_type(u & jnp.uint32(0xFFFF0000), F32).astype(BF16)
    return jnp.concatenate([lo, hi], axis=1)


def _norm_matmul_kernel(x_ref, g_ref, w_ref, ws_ref, o_ref, os_ref, h_ref):
    @pl.when(pl.program_id(1) == 0)
    def _():
        h_ref[...] = (_rms(x_ref[...]) * g_ref[...]).astype(BF16)
        os_ref[...] = _dot(h_ref[...], ws_ref[...])

    o_ref[...] = _dot(h_ref[...], w_ref[...])


def _norm_matmul(x, g, w, w_side, tm, tn):
    m, d = x.shape
    n = w.shape[1]
    ns = w_side.shape[1]
    return pl.pallas_call(
        _norm_matmul_kernel,
        grid=(m // tm, n // tn),
        in_specs=[pl.BlockSpec((tm, d), lambda i, j: (i, 0)),
                  pl.BlockSpec((1, d), lambda i, j: (0, 0)),
                  pl.BlockSpec((d, tn), lambda i, j: (0, j)),
                  pl.BlockSpec((d, ns), lambda i, j: (0, 0))],
        out_specs=[pl.BlockSpec((tm, tn), lambda i, j: (i, j)), pl.BlockSpec((tm, ns), lambda i, j: (i, 0))],
        out_shape=[jax.ShapeDtypeStruct((m, n), F32), jax.ShapeDtypeStruct((m, ns), F32)],
        scratch_shapes=[pltpu.VMEM((tm, d), BF16)],
        compiler_params=_params(("parallel", "arbitrary")),
        name="in_proj",
    )(x, g, w, w_side)


def _qk_rope_kernel(q_ref, k_ref, v_ref, qg_ref, kg_ref, c_ref, sn_ref, sp_ref, qo_ref, ko_ref, vo_ref):
    c, sn, sp = c_ref[...], sn_ref[...], sp_ref[...]
    half = ROT_DIM // 2

    def norm_rope(x, g):
        y = _rms(x) * g
        up = pltpu.roll(y, LANES - half, 1)
        dn = pltpu.roll(y, half, 1)
        return y * c + up * sn + dn * sp

    qg, kg = qg_ref[...], kg_ref[...]
    for h in range(q_ref.shape[1] // HEAD_DIM_A):
        hs = slice(h * HEAD_DIM_A, (h + 1) * HEAD_DIM_A)
        qo_ref[:, hs] = norm_rope(q_ref[:, hs], qg)
        ko_ref[:, hs] = norm_rope(k_ref[:, hs], kg)
    vo_ref[...] = v_ref[...]


def _qk_rope(proj, qg, kg, tabs, n_heads, tr, tab_index):
    t = proj.shape[0]
    w = n_heads * HEAD_DIM_A
    col = lambda j: pl.BlockSpec((tr, w), lambda i: (i, j))
    vec = pl.BlockSpec((1, HEAD_DIM_A), lambda i: (0, 0))
    tab = pl.BlockSpec((tr, HEAD_DIM_A), lambda i: (tab_index(i), 0))
    return pl.pallas_call(
        _qk_rope_kernel,
        grid=(t // tr,),
        in_specs=[col(0), col(1), col(2), vec, vec, tab, tab, tab],
        out_specs=[col(0), col(0), col(0)],
        out_shape=[jax.ShapeDtypeStruct((t, w), F32)] * 3,
        compiler_params=_params(("parallel",)),
        name="qk_rope",
    )(proj, proj, proj, qg, kg, *tabs)


def _rope_tables(pos):
    half = ROT_DIM // 2
    inv = ROPE_THETA ** (-jnp.arange(half, dtype=F32) * 2.0 / ROT_DIM)
    ang = pos.astype(F32)[:, None] * inv[None, :]
    cos, sin = jnp.cos(ang), jnp.sin(ang)
    n = pos.shape[0]
    rest = HEAD_DIM_A - ROT_DIM
    c = jnp.concatenate([cos, cos, jnp.ones((n, rest), F32)], axis=1)
    sn = jnp.concatenate([-sin, jnp.zeros((n, half + rest), F32)], axis=1)
    sp = jnp.concatenate([jnp.zeros((n, half), F32), sin, jnp.zeros((n, rest), F32)], axis=1)
    return c, sn, sp


def _block_select(gate, n_cand, n_sel, max_cand):
    lane = lax.broadcasted_iota(jnp.int32, gate.shape, 1)
    gate = jnp.where(lane < n_cand, gate, -jnp.inf)
    rank = jnp.zeros(gate.shape, F32)
    for m in range(max_cand):
        gm = jnp.broadcast_to(gate[:, m:m + 1], gate.shape)
        beats = (gm > gate) | ((gm == gate) & (lane > m))
        rank = rank + jnp.where(beats, 1.0, 0.0)
    return jnp.where((rank < n_sel) & (lane < n_cand), 1.0, 0.0)


def _moba_prompt_kernel(q_ref, k_ref, v_ref, o_ref, kaug_s, v16_s, *, nb, n_sel):
    blk, d = MOBA_BLOCK, HEAD_DIM_A
    s_len = nb * blk
    kaug_s[:, :d] = k_ref[...].astype(BF16)
    key_blk = lax.broadcasted_iota(jnp.int32, (s_len, LANES), 0) // blk
    kaug_s[:, d:] = (key_blk == lax.broadcasted_iota(jnp.int32, (s_len, LANES), 1)).astype(BF16)
    v16_s[...] = v_ref[...].astype(BF16)
    means = [jnp.mean(k_ref[n * blk:(n + 1) * blk, :], axis=0, keepdims=True) for n in range(nb)]
    kmean = jnp.concatenate(means + [jnp.zeros((LANES - nb, d), F32)], axis=0)
    kh, kl = _split2(kmean)
    lane = lax.broadcasted_iota(jnp.int32, (blk, LANES), 1)
    row = lax.broadcasted_iota(jnp.int32, (blk, blk), 0)
    col = lax.broadcasted_iota(jnp.int32, (blk, blk), 1)
    scale = d ** -0.5
    for qb in range(nb):
        q = q_ref[qb * blk:(qb + 1) * blk, :]
        qh, ql = _split2(q)
        if qb > 0:
            gate = _dot_nt(qh, kh) + _dot_nt(qh, kl) + _dot_nt(ql, kh)
            keep = jnp.where(lane == qb, 1.0, _block_select(gate, qb, n_sel, qb))
        else:
            keep = jnp.where(lane == qb, 1.0, 0.0)
        bias = ((1.0 - keep) * NEG_INF).astype(BF16)
        nk = (qb + 1) * blk
        s = _dot_nt(jnp.concatenate([qh, bias], axis=1), kaug_s[:nk, :]) * scale
        s_own = jnp.where(col <= row, s[:, qb * blk:], NEG_INF)
        s = s_own if qb == 0 else jnp.concatenate([s[:, :qb * blk], s_own], axis=1)
        m = jnp.max(s, axis=-1, keepdims=True)
        p = jnp.exp(s - m)
        l = jnp.sum(p, axis=-1, keepdims=True)
        o_ref[qb * blk:(qb + 1) * blk, :] = _dot(p.astype(BF16), v16_s[:nk, :]) / l


def _moba_prompt(q, k, v, n_heads):
    b, s, _ = q.shape
    blk = MOBA_BLOCK
    assert s % blk == 0 and s // blk <= LANES
    nb = s // blk
    n_sel = max(1, min(MOBA_TOPK, nb - 1))
    d = HEAD_DIM_A
    spec = pl.BlockSpec((None, s, d), lambda bi, h: (bi, 0, h))
    return pl.pallas_call(
        functools.partial(_moba_prompt_kernel, nb=nb, n_sel=n_sel),
        grid=(b, n_heads),
        in_specs=[spec, spec, spec],
        out_specs=spec,
        out_shape=jax.ShapeDtypeStruct(q.shape, F32),
        scratch_shapes=[pltpu.VMEM((s, d + LANES), BF16), pltpu.VMEM((s, d), BF16)],
        compiler_params=_params(("parallel", "parallel")),
        name="moba_prompt",
    )(q, k, v)


def _moba_decode_kernel(pt_ref, q_ref, kn_ref, vn_ref, *refs, n_pages, page, n_sel):
    k_refs = refs[:n_pages]
    v_refs = refs[n_pages:2 * n_pages]
    o_ref = refs[2 * n_pages]
    ppb = MOBA_BLOCK // page
    nb = n_pages // ppb
    q = q_ref[...]
    scale = HEAD_DIM_A ** -0.5

    lane = lax.broadcasted_iota(jnp.int32, (q.shape[0], LANES), 1)
    gate = jnp.zeros((q.shape[0], LANES), F32)
    for n in range(nb):
        ksum = jnp.zeros(q.shape, F32)
        for j in range(ppb):
            ksum = ksum + jnp.sum(k_refs[n * ppb + j][...], axis=0)
        g = jnp.sum(q * (ksum * (1.0 / MOBA_BLOCK)), axis=-1, keepdims=True)
        gate = jnp.where(lane == n, g, gate)
    sel = _block_select(gate, nb, n_sel, nb)

    n_h, d = q.shape
    ones = jnp.ones((d, LANES), BF16)
    m = jnp.broadcast_to(jnp.sum(q * kn_ref[...], axis=-1, keepdims=True) * scale, (n_h, LANES))
    l = jnp.ones_like(m)
    acc = vn_ref[...]
    for pg in range(n_pages):
        n = pg // ppb
        keep = jnp.broadcast_to(sel[:, n:n + 1], (n_h, LANES)) > 0.5
        prod = (k_refs[pg][...] * q[None]).reshape(page * n_h, d).astype(BF16)
        s = _dot(prod, ones).reshape(page, n_h, LANES) * scale
        s = jnp.where(keep[None], s, NEG_INF)
        m_new = jnp.maximum(m, jnp.max(s, axis=0))
        alpha = jnp.exp(m - m_new)
        p = jnp.exp(s - m_new[None])
        l = alpha * l + jnp.sum(p, axis=0)
        acc = alpha * acc + jnp.sum(p * v_refs[pg][...], axis=0)
        m = m_new
    o_ref[...] = acc / l


def _moba_decode(q, k_new, v_new, cache_k, cache_v, page_table, n_heads):
    nseq, n_pages = page_table.shape
    page = cache_k.shape[1]
    d = HEAD_DIM_A
    nb_all = -(-(n_pages * page + 1) // MOBA_BLOCK)
    n_sel = max(1, min(MOBA_TOPK, nb_all - 1))
    assert MOBA_BLOCK % page == 0 and (n_pages * page) % MOBA_BLOCK == 0
    hd = pl.BlockSpec((None, n_heads, d), lambda b, pt: (b, 0, 0))

    def page_spec(pg):
        return pl.BlockSpec((None, page, n_heads, d), lambda b, pt: (pt[b * n_pages + pg], 0, 0, 0))

    specs = [page_spec(pg) for pg in range(n_pages)]
    q3 = q.reshape(nseq, n_heads, d)
    return pl.pallas_call(
        functools.partial(_moba_decode_kernel, n_pages=n_pages, page=page, n_sel=n_sel),
        grid_spec=pltpu.PrefetchScalarGridSpec(
            num_scalar_prefetch=1,
            grid=(nseq,),
            in_specs=[hd, hd, hd] + specs + specs,
            out_specs=hd),
        out_shape=jax.ShapeDtypeStruct((nseq, n_heads, d), F32),
        compiler_params=_params(("parallel",)),
        name="moba_decode",
    )(page_table.reshape(-1), q3, k_new.reshape(nseq, n_heads, d), v_new.reshape(nseq, n_heads, d),
      *([cache_k] * n_pages), *([cache_v] * n_pages)).reshape(nseq, n_heads * d)


def _gla_out(o, gn, g):
    return _rms(o) * gn * (g * jax.nn.sigmoid(g))


def _gla_prompt_kernel(q_ref, k_ref, v_ref, g_ref, alr_ref, wa2_ref, ba_ref, gn_ref, o_ref, sfin_ref, la_s, st_s,
                       *, n_chunks):
    c = GLA_CHUNK
    dk = q_ref.shape[1]
    z = _dot(alr_ref[...].astype(BF16), wa2_ref[...].astype(BF16)) + ba_ref[...]
    la_s[...] = _log_sigmoid(z) * (1.0 / GLA_TAU)
    st_s[...] = jnp.zeros(st_s.shape, F32)
    row = lax.broadcasted_iota(jnp.int32, (c, c), 0)
    col = lax.broadcasted_iota(jnp.int32, (c, c), 1)
    causal = col <= row
    tril = causal.astype(BF16)
    gn = gn_ref[...]

    def body(i, carry):
        r0 = pl.multiple_of(i * c, c)
        qc = q_ref[pl.ds(r0, c), :] * (dk ** -0.5)
        kc = k_ref[pl.ds(r0, c), :]
        vc = v_ref[pl.ds(r0, c), :].astype(BF16)
        ac = la_s[pl.ds(r0, c), :]
        a1 = ac.astype(BF16)
        r1 = ac - a1.astype(F32)
        a2 = r1.astype(BF16)
        a3 = (r1 - a2.astype(F32)).astype(BF16)
        b = _dot(tril, a1) + _dot(tril, a2) + _dot(tril, a3)
        q_t = (qc * jnp.exp(b)).astype(BF16)
        k_t = (kc * jnp.exp(-b)).astype(BF16)
        att = jnp.where(causal, _dot_nt(q_t, k_t), 0.0)
        st = st_s[...]
        o = _dot(att.astype(BF16), vc) + _dot(q_t, st.astype(BF16))
        b_last = b[c - 1:c, :]
        k_dec = (kc * jnp.exp(b_last - b)).astype(BF16)
        st_s[...] = _col_from_row(jnp.exp(b_last)) * st + _dot_tn(k_dec, vc)
        o_ref[pl.ds(r0, c), :] = _gla_out(o, gn, g_ref[pl.ds(r0, c), :])
        return carry

    lax.fori_loop(0, n_chunks, body, 0)
    sfin_ref[...] = st_s[...]


def _gla_prompt(proj3, alr3, wa2, ba, gn, cols):
    b, s, _ = proj3.shape
    dk, dv = LANES, 2 * LANES
    h = N_HEADS_B
    assert s % GLA_CHUNK == 0
    sk = lambda off: pl.BlockSpec((None, s, dk), lambda bi, hi: (bi, 0, off + hi))
    sv = lambda off: pl.BlockSpec((None, s, dv), lambda bi, hi: (bi, 0, off + hi))
    return pl.pallas_call(
        functools.partial(_gla_prompt_kernel, n_chunks=s // GLA_CHUNK),
        grid=(b, h),
        in_specs=[sk(cols["q_b"] // dk), sk(cols["k_b"] // dk), sv(cols["v_b"] // dv), sv(cols["g_b"] // dv),
                  pl.BlockSpec((None, s, LANES), lambda bi, hi: (bi, 0, 0)),
                  pl.BlockSpec((LANES, dk), lambda bi, hi: (0, hi)),
                  pl.BlockSpec((1, dk), lambda bi, hi: (0, hi)),
                  pl.BlockSpec((1, dv), lambda bi, hi: (0, 0))],
        out_specs=[pl.BlockSpec((None, s, dv), lambda bi, hi: (bi, 0, hi)),
                   pl.BlockSpec((None, None, dk, dv), lambda bi, hi: (bi, hi, 0, 0))],
        out_shape=[jax.ShapeDtypeStruct((b, s, h * dv), F32), jax.ShapeDtypeStruct((b, h, dk, dv), F32)],
        scratch_shapes=[pltpu.VMEM((s, dk), F32), pltpu.VMEM((dk, dv), F32)],
        compiler_params=_params(("parallel", "parallel")),
        name="gla_prompt",
    )(proj3, proj3, proj3, proj3, alr3, wa2, ba, gn)


def _gla_decode_kernel(q_ref, k_ref, v_ref, g_ref, alr_ref, wa2_ref, ba_ref, gn_ref, s0_ref, o_ref, s_ref):
    dk, dv = LANES, 2 * LANES
    alr = jnp.broadcast_to(alr_ref[...], (8, LANES)).astype(BF16)
    z = _dot(alr, wa2_ref[...].astype(BF16))[0:1, :] + ba_ref[...]
    a = jnp.exp(_log_sigmoid(z) * (1.0 / GLA_TAU))
    gn = gn_ref[...]
    for h in range(N_HEADS_B):
        ks = slice(h * dk, (h + 1) * dk)
        vs = slice(h * dv, (h + 1) * dv)
        a_col = _col_from_row(a[:, ks])
        k_col = _col_from_row(k_ref[:, ks])
        q_col = _col_from_row(q_ref[:, ks] * (dk ** -0.5))
        s_new = a_col * s0_ref[h] + k_col * v_ref[:, vs]
        s_ref[h] = s_new
        o = jnp.sum(q_col * s_new, axis=0, keepdims=True)
        o_ref[:, vs] = _gla_out(o, gn, g_ref[:, vs])


def _gla_decode(proj_s, alr_s, wa2, ba, gn, state, cols):
    t, n = proj_s.shape
    dk, dv = LANES, 2 * LANES
    h = N_HEADS_B
    p3 = proj_s.reshape(t, 1, n)
    a3 = alr_s.reshape(t, 1, LANES)
    blk = lambda w, off: pl.BlockSpec((None, 1, w), lambda i: (i, 0, off // w))
    full = lambda a: pl.BlockSpec(a.shape, lambda i: (0,) * a.ndim)
    st = pl.BlockSpec((None, h, dk, dv), lambda i: (i, 0, 0, 0))
    o, s_new = pl.pallas_call(
        _gla_decode_kernel,
        grid=(t,),
        in_specs=[blk(h * dk, cols["q_b"]), blk(h * dk, cols["k_b"]), blk(h * dv, cols["v_b"]),
                  blk(h * dv, cols["g_b"]), blk(LANES, 0), full(wa2), full(ba), full(gn), st],
        out_specs=[pl.BlockSpec((None, 1, h * dv), lambda i: (i, 0, 0)), st],
        out_shape=[jax.ShapeDtypeStruct((t, 1, h * dv), F32), jax.ShapeDtypeStruct(state.shape, F32)],
        compiler_params=_params(("parallel",)),
        name="gla_decode",
    )(p3, p3, p3, p3, a3, wa2, ba, gn, state)
    return o.reshape(t, h * dv), s_new


def _merge_kernel(oa_ref, ob_ref, ga_ref, gb_ref, x_ref, wa_ref, wb_ref, wo_ref, n2_ref, rwh_ref, rwl_ref, rb_ref,
                  x1_ref, h2_ref, lg_ref):
    ya = _dot(oa_ref[...].astype(BF16), wa_ref[...])
    yb = _dot(ob_ref[...].astype(BF16), wb_ref[...])
    merged = jax.nn.sigmoid(ga_ref[...]) * ya + jax.nn.sigmoid(gb_ref[...]) * yb
    x1 = x_ref[...] + _dot(merged.astype(BF16), wo_ref[...])
    x1_ref[...] = x1
    h2 = _rms(x1) * n2_ref[...]
    hh, hl = _split2(h2)
    lg_ref[...] = _dot(hh, rwh_ref[...]) + _dot(hl, rwh_ref[...]) + _dot(hh, rwl_ref[...]) + rb_ref[...]
    h2_ref[...] = _pack_bf16_pairs(hh)


def _merge(o_a, o_b, proj, x, wa, wb, wo, n2, rwh, rwl, rb, cols, tm):
    t, d = x.shape
    row = lambda w, off: pl.BlockSpec((tm, w), lambda i: (i, off // w))
    full = lambda a: pl.BlockSpec(a.shape, lambda i: (0,) * a.ndim)
    return pl.pallas_call(
        _merge_kernel,
        grid=(t // tm,),
        in_specs=[row(o_a.shape[1], 0), row(o_b.shape[1], 0), row(d, cols["gate_a"]), row(d, cols["gate_b"]),
                  row(d, 0), full(wa), full(wb), full(wo), full(n2), full(rwh), full(rwl), full(rb)],
        out_specs=[row(d, 0), row(d // 2, 0), row(LANES, 0)],
        out_shape=[jax.ShapeDtypeStruct((t, d), F32), jax.ShapeDtypeStruct((t, d // 2), jnp.uint32),
                   jax.ShapeDtypeStruct((t, LANES), F32)],
        compiler_params=_params(("parallel",)),
        name="merge_out_proj",
    )(o_a, o_b, proj, proj, x, wa, wb, wo, n2, rwh, rwl, rb)


def _router_kernel(lg_ref, idx_ref, w_ref, rank_ref, cnt_ref, carry_s):
    @pl.when(pl.program_id(0) == 0)
    def _():
        carry_s[...] = jnp.zeros(carry_s.shape, F32)

    l = lg_ref[...]
    tm = l.shape[0]
    lane = lax.broadcasted_iota(jnp.int32, l.shape, 1).astype(F32)
    vals, idxs = [], []
    for _ in range(MOE_TOPK):
        mx = jnp.max(l, axis=-1, keepdims=True)
        ix = jnp.min(jnp.where(l == mx, lane, float(LANES)), axis=-1, keepdims=True)
        vals.append(mx)
        idxs.append(ix)
        l = jnp.where(lane == ix, -jnp.inf, l)
    es = [jnp.exp(v - vals[0]) for v in vals]
    tot = es[0]
    for e in es[1:]:
        tot = tot + e
    onehot = jnp.zeros(l.shape, F32)
    for ix in idxs:
        onehot = onehot + jnp.where(lane == ix, 1.0, 0.0)
    r = lax.broadcasted_iota(jnp.int32, (tm, tm), 0)
    c = lax.broadcasted_iota(jnp.int32, (tm, tm), 1)
    before = _dot((c < r).astype(BF16), onehot.astype(BF16)) + carry_s[...]
    idx_o = jnp.zeros(l.shape, F32)
    w_o = jnp.zeros(l.shape, F32)
    rank_o = jnp.zeros(l.shape, F32)
    for k in range(MOE_TOPK):
        rk = jnp.sum(jnp.where(lane == idxs[k], before, 0.0), axis=-1, keepdims=True)
        idx_o = jnp.where(lane == k, idxs[k], idx_o)
        w_o = jnp.where(lane == k, es[k] / tot, w_o)
        rank_o = jnp.where(lane == k, rk, rank_o)
    idx_ref[...] = idx_o.astype(jnp.int32)
    w_ref[...] = w_o
    rank_ref[...] = rank_o.astype(jnp.int32)
    carry_s[...] = carry_s[...] + jnp.sum(onehot, axis=0, keepdims=True)
    cnt_ref[...] = carry_s[...].astype(jnp.int32)


def _router(logits, tm):
    t = logits.shape[0]
    row = pl.BlockSpec((tm, LANES), lambda i: (i, 0))
    return pl.pallas_call(
        _router_kernel,
        grid=(t // tm,),
        in_specs=[row],
        out_specs=[row, row, row, pl.BlockSpec((1, LANES), lambda i: (0, 0))],
        out_shape=[jax.ShapeDtypeStruct((t, LANES), jnp.int32), jax.ShapeDtypeStruct((t, LANES), F32),
                   jax.ShapeDtypeStruct((t, LANES), jnp.int32), jax.ShapeDtypeStruct((1, LANES), jnp.int32)],
        scratch_shapes=[pltpu.VMEM((1, LANES), F32)],
        compiler_params=_params(("arbitrary",)),
        name="router",
    )(logits)


ISSUE_UNROLL = 4


def _dispatch_kernel(dest_ref, x_ref, xs_in_hbm, xs_hbm, sem):
    del xs_in_hbm
    tm = x_ref.shape[0]
    base = pl.program_id(0) * tm * MOE_TOPK

    def issue(r, carry):
        for k in range(MOE_TOPK):
            dst = dest_ref[base + r * MOE_TOPK + k]
            pltpu.make_async_copy(x_ref.at[pl.ds(r, 1), :], xs_hbm.at[pl.ds(dst, 1), :], sem).start()
        return carry

    lax.fori_loop(0, tm, issue, 0, unroll=ISSUE_UNROLL)
    for _ in range(MOE_TOPK):
        pltpu.make_async_copy(x_ref, xs_hbm.at[pl.ds(0, tm), :], sem).wait()


def _dispatch(dest, x_packed, xs, tm):
    t, w = x_packed.shape
    return pl.pallas_call(
        _dispatch_kernel,
        grid_spec=pltpu.PrefetchScalarGridSpec(
            num_scalar_prefetch=1,
            grid=(t // tm,),
            in_specs=[pl.BlockSpec((tm, w), lambda i, dst: (i, 0)), pl.BlockSpec(memory_space=pl.ANY)],
            out_specs=pl.BlockSpec(memory_space=pl.ANY),
            scratch_shapes=[pltpu.SemaphoreType.DMA(())]),
        out_shape=jax.ShapeDtypeStruct(xs.shape, xs.dtype),
        input_output_aliases={2: 0},
        compiler_params=_params(("arbitrary",)),
        name="moe_dispatch",
    )(dest, x_packed, xs)


def _stream_expert_weights(first_ref, nxt_ref, be_ref, w_hbms, stages, casts, sems, tn):
    j, i = pl.program_id(0), pl.program_id(1)
    nj = pl.num_programs(0)

    def copies(e, jj):
        c0 = pl.multiple_of(jj * tn, tn)
        return [pltpu.make_async_copy(w.at[e, :, pl.ds(c0, tn)], st, sems.at[n])
                for n, (w, st) in enumerate(zip(w_hbms, stages))]

    @pl.when(first_ref[i] == 1)
    def _():
        @pl.when((j == 0) & (i == 0))
        def _():
            for c in copies(be_ref[0], 0):
                c.start()

        for c in copies(be_ref[i], j):
            c.wait()
        for st, cs in zip(stages, casts):
            cs[...] = st[...].astype(BF16)
        nxt = nxt_ref[i]

        @pl.when(nxt >= 0)
        def _():
            for c in copies(nxt, j):
                c.start()

        @pl.when((nxt < 0) & (j + 1 < nj))
        def _():
            for c in copies(be_ref[0], j + 1):
                c.start()


def _expert_up_kernel(be_ref, first_ref, nxt_ref, x_ref, wg_hbm, wu_hbm, bg_ref, bu_ref, o_ref,
                      stg_g, stg_u, wg_s, wu_s, sems):
    _stream_expert_weights(first_ref, nxt_ref, be_ref, (wg_hbm, wu_hbm), (stg_g, stg_u), (wg_s, wu_s), sems,
                           o_ref.shape[1])
    x = _unpack_bf16_pairs(x_ref[...])
    gate = jnp.minimum(_dot(x, wg_s[...]) + bg_ref[...], SWIGLU_LIMIT)
    up = jnp.clip(_dot(x, wu_s[...]) + bu_ref[...], -SWIGLU_LIMIT, SWIGLU_LIMIT)
    o_ref[...] = ((up + 1.0) * (gate * jax.nn.sigmoid(SWIGLU_ALPHA * gate))).astype(BF16)


def _expert_down_kernel(be_ref, first_ref, nxt_ref, h_ref, wd_hbm, bd_ref, o_ref, stg, wd_s, sems):
    _stream_expert_weights(first_ref, nxt_ref, be_ref, (wd_hbm,), (stg,), (wd_s,), sems, o_ref.shape[1])
    o_ref[...] = _dot(h_ref[...], wd_s[...]) + bd_ref[...]


def _expert_up(meta, xs, wg, wu, bg, bu, bm, tf):
    p, half = xs.shape
    d, f = wg.shape[1], wg.shape[2]
    bspec = pl.BlockSpec((None, 1, tf), lambda j, i, be, fi, nx: (be[i], 0, j))
    hbm = pl.BlockSpec(memory_space=pl.ANY)
    return pl.pallas_call(
        _expert_up_kernel,
        grid_spec=pltpu.PrefetchScalarGridSpec(
            num_scalar_prefetch=3,
            grid=(f // tf, p // bm),
            in_specs=[pl.BlockSpec((bm, half), lambda j, i, be, fi, nx: (i, 0)), hbm, hbm, bspec, bspec],
            out_specs=pl.BlockSpec((bm, tf), lambda j, i, be, fi, nx: (i, j)),
            scratch_shapes=[pltpu.VMEM((d, tf), F32), pltpu.VMEM((d, tf), F32),
                            pltpu.VMEM((d, tf), BF16), pltpu.VMEM((d, tf), BF16),
                            pltpu.SemaphoreType.DMA((2,))]),
        out_shape=jax.ShapeDtypeStruct((p, f), BF16),
        compiler_params=_params(("arbitrary", "arbitrary")),
        name="moe_up",
    )(*meta, xs, wg, wu, bg, bu)


def _expert_down(meta, hid, wd, bd, bm, tn):
    p, f = hid.shape
    d = wd.shape[2]
    return pl.pallas_call(
        _expert_down_kernel,
        grid_spec=pltpu.PrefetchScalarGridSpec(
            num_scalar_prefetch=3,
            grid=(d // tn, p // bm),
            in_specs=[pl.BlockSpec((bm, f), lambda j, i, be, fi, nx: (i, 0)),
                      pl.BlockSpec(memory_space=pl.ANY),
                      pl.BlockSpec((None, 1, tn), lambda j, i, be, fi, nx: (be[i], 0, j))],
            out_specs=pl.BlockSpec((bm, tn), lambda j, i, be, fi, nx: (i, j)),
            scratch_shapes=[pltpu.VMEM((f, tn), F32), pltpu.VMEM((f, tn), BF16), pltpu.SemaphoreType.DMA((1,))]),
        out_shape=jax.ShapeDtypeStruct((p, d), F32),
        compiler_params=_params(("arbitrary", "arbitrary")),
        name="moe_down",
    )(*meta, hid, wd, bd)


def _combine_kernel(dest_ref, ys_hbm, x1_ref, w_ref, p_ref, pw_ref, pg_ref, gw_ref, y_ref, buf, sems):
    tm = x1_ref.shape[0]
    i = pl.program_id(0)
    slot = i % 2

    def fetch(tile, sl):
        base = tile * tm * MOE_TOPK

        def issue(r, carry):
            for k in range(MOE_TOPK):
                dst = dest_ref[base + r * MOE_TOPK + k]
                pltpu.make_async_copy(ys_hbm.at[pl.ds(dst, 1), :], buf.at[sl, pl.ds(k * tm + r, 1), :],
                                      sems.at[sl]).start()
            return carry

        lax.fori_loop(0, tm, issue, 0, unroll=ISSUE_UNROLL)

    @pl.when(i == 0)
    def _():
        fetch(0, 0)

    @pl.when(i + 1 < pl.num_programs(0))
    def _():
        fetch(i + 1, 1 - slot)

    e = _rms(_dot(p_ref[...].astype(BF16), pw_ref[...])) * pg_ref[...]
    pltpu.make_async_copy(ys_hbm.at[pl.ds(0, MOE_TOPK * tm), :], buf.at[slot], sems.at[slot]).wait()
    w = w_ref[...]
    x2 = x1_ref[...]
    for k in range(MOE_TOPK):
        x2 = x2 + w[:, k:k + 1] * buf[slot, k * tm:(k + 1) * tm, :]
    gate = jax.nn.sigmoid(_dot(_rms(x2).astype(BF16), gw_ref[...]))
    y_ref[...] = x2 + gate * e


def _combine(dest, ys, x1, w, p, pw, pg, gw, tm):
    t, d = x1.shape
    full = lambda a: pl.BlockSpec(a.shape, lambda i, dst: (0,) * a.ndim)
    row = lambda width: pl.BlockSpec((tm, width), lambda i, dst: (i, 0))
    return pl.pallas_call(
        _combine_kernel,
        grid_spec=pltpu.PrefetchScalarGridSpec(
            num_scalar_prefetch=1,
            grid=(t // tm,),
            in_specs=[pl.BlockSpec(memory_space=pl.ANY), row(d), row(LANES), row(p.shape[1]),
                      full(pw), full(pg), full(gw)],
            out_specs=row(d),
            scratch_shapes=[pltpu.VMEM((2, MOE_TOPK * tm, d), F32), pltpu.SemaphoreType.DMA((2,))]),
        out_shape=jax.ShapeDtypeStruct((t, d), F32),
        compiler_params=_params(("arbitrary",)),
        name="moe_combine_ple",
    )(dest, ys, x1, w, p, pw, pg, gw)


def _largest_tile(n, cap, quantum):
    best = None
    for t in range(quantum, cap + 1, quantum):
        if n % t == 0:
            best = t
    assert best is not None, (n, cap, quantum)
    return best


def kernel(x_prompt, x_sample, cache_k, cache_v, state_gla, page_table, p_prompt, p_sample, norm1_g, w_in, q_norm_g, k_norm_g, gla_w_a2, gla_b_a, gla_norm_g, w_branch_a, w_branch_b, w_out, norm2_g, router_w, router_b, moe_w_gate, moe_b_gate, moe_w_up, moe_b_up, moe_w_down, moe_b_down, ple_w, ple_norm_g, ple_gate_w):
    assert norm1_g.shape[0] == 1, "one layer"
    b, s, d = x_prompt.shape
    nseq = x_sample.shape[0]
    assert x_sample.shape[1] == 1
    n_heads = d // (2 * HEAD_DIM_A)
    wa = n_heads * HEAD_DIM_A
    kb_w = N_HEADS_B * LANES
    vb_w = 2 * kb_w
    n_exp = router_w.shape[2]
    tp, ts = b * s, nseq

    sizes = [("q_a", wa), ("k_a", wa), ("v_a", wa), ("q_b", kb_w), ("k_b", kb_w), ("v_b", vb_w), ("g_b", vb_w),
             ("a_lr", GLA_LOWRANK), ("gate_a", d), ("gate_b", d)]
    cols, parts, src, off = {}, [], 0, 0
    w_in0 = w_in[0]
    for name, width in sizes:
        part = w_in0[:, src:src + width]
        src += width
        if name == "a_lr":
            w_side = jnp.pad(part, ((0, 0), (0, LANES - width))).astype(BF16)
            continue
        cols[name] = off
        parts.append(part)
        off += width
    w_main = jnp.concatenate(parts, axis=1).astype(BF16)
    n_main = w_main.shape[1]
    tn = _largest_tile(n_main, 1024, 2 * LANES)

    g1 = norm1_g[0][None, :]
    qg, kg = q_norm_g[0][None, :], k_norm_g[0][None, :]
    wa2 = jnp.pad(gla_w_a2[0], ((0, LANES - GLA_LOWRANK), (0, 0)))
    ba = gla_b_a[0][None, :]
    gn = gla_norm_g[0][None, :]
    wba, wbb, wo = w_branch_a[0].astype(BF16), w_branch_b[0].astype(BF16), w_out[0].astype(BF16)
    n2 = norm2_g[0][None, :]
    rw = jnp.pad(router_w[0], ((0, 0), (0, LANES - n_exp)))
    rwh = rw.astype(BF16)
    rwl = (rw - rwh.astype(F32)).astype(BF16)
    rb = jnp.concatenate([router_b[0], jnp.full((LANES - n_exp,), -jnp.inf, F32)])[None, :]

    xp = x_prompt.reshape(tp, d)
    proj_p, alr_p = _norm_matmul(xp, g1, w_main, w_side, _largest_tile(tp, 1024, 8), tn)
    tr = _largest_tile(s, 512, 8)
    tabs_p = _rope_tables(jnp.arange(s, dtype=jnp.int32))
    q_p, k_p, v_p = _qk_rope(proj_p, qg, kg, tabs_p, n_heads, tr, lambda i: i % (s // tr))
    o_a_p = _moba_prompt(q_p.reshape(b, s, wa), k_p.reshape(b, s, wa), v_p.reshape(b, s, wa), n_heads)
    o_b_p, gla_p = _gla_prompt(proj_p.reshape(b, s, n_main), alr_p.reshape(b, s, LANES), wa2, ba, gn, cols)
    x1_p, h2_p, lg_p = _merge(o_a_p.reshape(tp, wa), o_b_p.reshape(tp, vb_w), proj_p, xp, wba, wbb, wo, n2,
                              rwh, rwl, rb, cols, _largest_tile(tp, 128, 8))

    xs_ = x_sample.reshape(ts, d)
    past = page_table.shape[1] * cache_k.shape[2]
    proj_s, alr_s = _norm_matmul(xs_, g1, w_main, w_side, ts, tn)
    tabs_s = tuple(jnp.broadcast_to(t_, (ts, HEAD_DIM_A)) for t_ in _rope_tables(jnp.full((1,), past, jnp.int32)))
    q_s, k_s, v_s = _qk_rope(proj_s, qg, kg, tabs_s, n_heads, ts, lambda i: i)
    o_a_s = _moba_decode(q_s, k_s, v_s, cache_k[0], cache_v[0], page_table, n_heads)
    o_b_s, gla_s = _gla_decode(proj_s, alr_s, wa2, ba, gn, state_gla[0], cols)
    x1_s, h2_s, lg_s = _merge(o_a_s, o_b_s, proj_s, xs_, wba, wbb, wo, n2, rwh, rwl, rb, cols, ts)

    t_all = tp + ts
    logits = jnp.concatenate([lg_p, lg_s], axis=0)
    idx_o, w_o, rank_o, cnt = _router(logits, _largest_tile(t_all, 256, LANES))
    bm = MOE_ROW_BLOCK
    n_assign = t_all * MOE_TOPK
    n_blocks = -(-n_assign // bm) + n_exp
    counts = cnt[0, :n_exp]
    pcounts = (counts + bm - 1) // bm * bm
    pend = jnp.cumsum(pcounts)
    pstart = pend - pcounts
    dest = (pstart[idx_o[:, :MOE_TOPK]] + rank_o[:, :MOE_TOPK]).reshape(-1)
    blk_start = jnp.arange(n_blocks, dtype=jnp.int32) * bm
    block_e = jnp.minimum(jnp.sum(pend[None, :] <= blk_start[:, None], axis=1), n_exp - 1).astype(jnp.int32)
    first = jnp.concatenate([jnp.ones((1,), jnp.int32), (block_e[1:] != block_e[:-1]).astype(jnp.int32)])
    nxt_pos = jnp.sum(block_e[None, :] <= block_e[:, None], axis=1)
    nxt = jnp.where(nxt_pos < n_blocks, block_e[jnp.minimum(nxt_pos, n_blocks - 1)], -1).astype(jnp.int32)
    meta = (block_e, first, nxt)

    xs_rows = jnp.zeros((n_blocks * bm, d // 2), jnp.uint32)
    xs_rows = _dispatch(dest[:tp * MOE_TOPK], h2_p, xs_rows, _largest_tile(tp, 512, 8))
    xs_rows = _dispatch(dest[tp * MOE_TOPK:], h2_s, xs_rows, ts)
    f = moe_w_gate.shape[3]
    hid = _expert_up(meta, xs_rows, moe_w_gate[0], moe_w_up[0], moe_b_gate[0][:, None, :],
                     moe_b_up[0][:, None, :], bm, _largest_tile(f, 1024, 2 * LANES))
    ys = _expert_down(meta, hid, moe_w_down[0], moe_b_down[0][:, None, :], bm, _largest_tile(d, 2048, 2 * LANES))

    pw, pg, gw = ple_w[0].astype(BF16), ple_norm_g[0][None, :], ple_gate_w[0].astype(BF16)
    tc = _largest_tile(ts, 128, 8)
    assert tp % tc == 0
    y_p = _combine(dest[:tp * MOE_TOPK], ys, x1_p, w_o[:tp], p_prompt[0].reshape(tp, -1), pw, pg, gw, tc)
    y_s = _combine(dest[tp * MOE_TOPK:], ys, x1_s, w_o[tp:], p_sample[0].reshape(ts, -1), pw, pg, gw, tc)

    hd = (n_heads, HEAD_DIM_A)
    return (y_p.reshape(b, s, d), y_s.reshape(nseq, 1, d),
            k_p.reshape(1, b, s, *hd), v_p.reshape(1, b, s, *hd), gla_p[None],
            k_s.reshape(1, nseq, 1, *hd), v_s.reshape(1, nseq, 1, *hd), gla_s[None])
```

```python
import functools

import jax
import jax.numpy as jnp
from jax import lax
from jax.experimental import pallas as pl
from jax.experimental.pallas import tpu as pltpu

HEAD_DIM_A = 128
MOBA_BLOCK = 256
MOBA_TOPK = 3
ROT_DIM = HEAD_DIM_A // 4
ROPE_THETA = 500000.0
N_HEADS_B = 4
GLA_LOWRANK = 16
GLA_TAU = 16.0
GLA_CHUNK = 64
MOE_TOPK = 4
SWIGLU_LIMIT = 7.0
SWIGLU_ALPHA = 1.702
EPS = 1e-6
NEG_INF = -1e30

LANES = 128
MOE_ROW_BLOCK = 256
VMEM_LIMIT = 56 * 1024 * 1024

F32 = jnp.float32
BF16 = jnp.bfloat16


def _params(sem):
    return pltpu.CompilerParams(dimension_semantics=sem, vmem_limit_bytes=VMEM_LIMIT)


def _dot(a, b):
    return jnp.dot(a, b, preferred_element_type=F32)


def _dot_nt(a, b):
    return lax.dot_general(a, b, (((1,), (1,)), ((), ())), preferred_element_type=F32)


def _dot_tn(a, b):
    return lax.dot_general(a, b, (((0,), (0,)), ((), ())), preferred_element_type=F32)


def _split2(x):
    hi = x.astype(BF16)
    lo = (x - hi.astype(F32)).astype(BF16)
    return hi, lo


def _rms(x):
    return x * lax.rsqrt(jnp.mean(x * x, axis=-1, keepdims=True) + EPS)


def _col_from_row(row):
    n = row.shape[1]
    r = lax.broadcasted_iota(jnp.int32, (n, n), 0)
    c = lax.broadcasted_iota(jnp.int32, (n, n), 1)
    return jnp.sum(jnp.where(r == c, jnp.broadcast_to(row, (n, n)), 0.0), axis=1, keepdims=True)


def _log_sigmoid(z):
    return jnp.minimum(z, 0.0) - jnp.log(1.0 + jnp.exp(-jnp.abs(z)))


def _pack_bf16_pairs(x16):
    half = x16.shape[1] // 2
    bits = lax.bitcast_convert_type(x16.astype(F32), jnp.uint32)
    return (bits[:, :half] >> 16) | bits[:, half:]


def _unpack_bf16_pairs(u):
    lo = lax.bitcast_convert_type(u << 16, F32).astype(BF16)
    hi = lax.bitcast_convert_type(u & jnp.uint32(0xFFFF0000), F32).astype(BF16)
    return jnp.concatenate([lo, hi], axis=1)


def _norm_matmul_kernel(x_ref, g_ref, w_ref, ws_ref, o_ref, os_ref, h_ref):
    @pl.when(pl.program_id(1) == 0)
    def _():
        h_ref[...] = (_rms(x_ref[...]) * g_ref[...]).astype(BF16)
        os_ref[...] = _dot(h_ref[...], ws_ref[...])

    o_ref[...] = _dot(h_ref[...], w_ref[...])


def _norm_matmul(x, g, w, w_side, tm, tn):
    m, d = x.shape
    n = w.shape[1]
    ns = w_side.shape[1]
    return pl.pallas_call(
        _norm_matmul_kernel,
        grid=(m // tm, n // tn),
        in_specs=[pl.BlockSpec((tm, d), lambda i, j: (i, 0)),
                  pl.BlockSpec((1, d), lambda i, j: (0, 0)),
                  pl.BlockSpec((d, tn), lambda i, j: (0, j)),
                  pl.BlockSpec((d, ns), lambda i, j: (0, 0))],
        out_specs=[pl.BlockSpec((tm, tn), lambda i, j: (i, j)), pl.BlockSpec((tm, ns), lambda i, j: (i, 0))],
        out_shape=[jax.ShapeDtypeStruct((m, n), F32), jax.ShapeDtypeStruct((m, ns), F32)],
        scratch_shapes=[pltpu.VMEM((tm, d), BF16)],
        compiler_params=_params(("parallel", "arbitrary")),
        name="in_proj",
    )(x, g, w, w_side)


def _qk_rope_kernel(q_ref, k_ref, v_ref, qg_ref, kg_ref, c_ref, sn_ref, sp_ref, qo_ref, ko_ref, vo_ref):
    c, sn, sp = c_ref[...], sn_ref[...], sp_ref[...]
    half = ROT_DIM // 2

    def norm_rope(x, g):
        y = _rms(x) * g
        up = pltpu.roll(y, LANES - half, 1)
        dn = pltpu.roll(y, half, 1)
        return y * c + up * sn + dn * sp

    qg, kg = qg_ref[...], kg_ref[...]
    for h in range(q_ref.shape[1] // HEAD_DIM_A):
        hs = slice(h * HEAD_DIM_A, (h + 1) * HEAD_DIM_A)
        qo_ref[:, hs] = norm_rope(q_ref[:, hs], qg)
        ko_ref[:, hs] = norm_rope(k_ref[:, hs], kg)
    vo_ref[...] = v_ref[...]


def _qk_rope(proj, qg, kg, tabs, n_heads, tr, tab_index):
    t = proj.shape[0]
    w = n_heads * HEAD_DIM_A
    col = lambda j: pl.BlockSpec((tr, w), lambda i: (i, j))
    vec = pl.BlockSpec((1, HEAD_DIM_A), lambda i: (0, 0))
    tab = pl.BlockSpec((tr, HEAD_DIM_A), lambda i: (tab_index(i), 0))
    return pl.pallas_call(
        _qk_rope_kernel,
        grid=(t // tr,),
        in_specs=[col(0), col(1), col(2), vec, vec, tab, tab, tab],
        out_specs=[col(0), col(0), col(0)],
        out_shape=[jax.ShapeDtypeStruct((t, w), F32)] * 3,
        compiler_params=_params(("parallel",)),
        name="qk_rope",
    )(proj, proj, proj, qg, kg, *tabs)


def _rope_tables(pos):
    half = ROT_DIM // 2
    inv = ROPE_THETA ** (-jnp.arange(half, dtype=F32) * 2.0 / ROT_DIM)
    ang = pos.astype(F32)[:, None] * inv[None, :]
    cos, sin = jnp.cos(ang), jnp.sin(ang)
    n = pos.shape[0]
    rest = HEAD_DIM_A - ROT_DIM
    c = jnp.concatenate([cos, cos, jnp.ones((n, rest), F32)], axis=1)
    sn = jnp.concatenate([-sin, jnp.zeros((n, half + rest), F32)], axis=1)
    sp = jnp.concatenate([jnp.zeros((n, half), F32), sin, jnp.zeros((n, rest), F32)], axis=1)
    return c, sn, sp


def _block_select(gate, n_cand, n_sel, max_cand):
    lane = lax.broadcasted_iota(jnp.int32, gate.shape, 1)
    gate = jnp.where(lane < n_cand, gate, -jnp.inf)
    rank = jnp.zeros(gate.shape, F32)
    for m in range(max_cand):
        gm = jnp.broadcast_to(gate[:, m:m + 1], gate.shape)
        beats = (gm > gate) | ((gm == gate) & (lane > m))
        rank = rank + jnp.where(beats, 1.0, 0.0)
    return jnp.where((rank < n_sel) & (lane < n_cand), 1.0, 0.0)


def _moba_prompt_kernel(q_ref, k_ref, v_ref, o_ref, kaug_s, v16_s, *, nb, n_sel):
    blk, d = MOBA_BLOCK, HEAD_DIM_A
    s_len = nb * blk
    kaug_s[:, :d] = k_ref[...].astype(BF16)
    key_blk = lax.broadcasted_iota(jnp.int32, (s_len, LANES), 0) // blk
    kaug_s[:, d:] = (key_blk == lax.broadcasted_iota(jnp.int32, (s_len, LANES), 1)).astype(BF16)
    v16_s[...] = v_ref[...].astype(BF16)
    means = [jnp.mean(k_ref[n * blk:(n + 1) * blk, :], axis=0, keepdims=True) for n in range(nb)]
    kmean = jnp.concatenate(means + [jnp.zeros((LANES - nb, d), F32)], axis=0)
    kh, kl = _split2(kmean)
    lane = lax.broadcasted_iota(jnp.int32, (blk, LANES), 1)
    row = lax.broadcasted_iota(jnp.int32, (blk, blk), 0)
    col = lax.broadcasted_iota(jnp.int32, (blk, blk), 1)
    scale = d ** -0.5
    for qb in range(nb):
        q = q_ref[qb * blk:(qb + 1) * blk, :]
        qh, ql = _split2(q)
        if qb > 0:
            gate = _dot_nt(qh, kh) + _dot_nt(qh, kl) + _dot_nt(ql, kh)
            keep = jnp.where(lane == qb, 1.0, _block_select(gate, qb, n_sel, qb))
        else:
            keep = jnp.where(lane == qb, 1.0, 0.0)
        bias = ((1.0 - keep) * NEG_INF).astype(BF16)
        nk = (qb + 1) * blk
        s = _dot_nt(jnp.concatenate([qh, bias], axis=1), kaug_s[:nk, :]) * scale
        s_own = jnp.where(col <= row, s[:, qb * blk:], NEG_INF)
        s = s_own if qb == 0 else jnp.concatenate([s[:, :qb * blk], s_own], axis=1)
        m = jnp.max(s, axis=-1, keepdims=True)
        p = jnp.exp(s - m)
        l = jnp.sum(p, axis=-1, keepdims=True)
        o_ref[qb * blk:(qb + 1) * blk, :] = _dot(p.astype(BF16), v16_s[:nk, :]) / l


def _moba_prompt(q, k, v, n_heads):
    b, s, _ = q.shape
    blk = MOBA_BLOCK
    assert s % blk == 0 and s // blk <= LANES
    nb = s // blk
    n_sel = max(1, min(MOBA_TOPK, nb - 1))
    d = HEAD_DIM_A
    spec = pl.BlockSpec((None, s, d), lambda bi, h: (bi, 0, h))
    return pl.pallas_call(
        functools.partial(_moba_prompt_kernel, nb=nb, n_sel=n_sel),
        grid=(b, n_heads),
        in_specs=[spec, spec, spec],
        out_specs=spec,
        out_shape=jax.ShapeDtypeStruct(q.shape, F32),
        scratch_shapes=[pltpu.VMEM((s, d + LANES), BF16), pltpu.VMEM((s, d), BF16)],
        compiler_params=_params(("parallel", "parallel")),
        name="moba_prompt",
    )(q, k, v)


def _moba_decode_kernel(pt_ref, q_ref, kn_ref, vn_ref, *refs, n_pages, page, n_sel):
    k_refs = refs[:n_pages]
    v_refs = refs[n_pages:2 * n_pages]
    o_ref = refs[2 * n_pages]
    ppb = MOBA_BLOCK // page
    nb = n_pages // ppb
    q = q_ref[...]
    scale = HEAD_DIM_A ** -0.5

    lane = lax.broadcasted_iota(jnp.int32, (q.shape[0], LANES), 1)
    gate = jnp.zeros((q.shape[0], LANES), F32)
    for n in range(nb):
        ksum = jnp.zeros(q.shape, F32)
        for j in range(ppb):
            ksum = ksum + jnp.sum(k_refs[n * ppb + j][...], axis=0)
        g = jnp.sum(q * (ksum * (1.0 / MOBA_BLOCK)), axis=-1, keepdims=True)
        gate = jnp.where(lane == n, g, gate)
    sel = _block_select(gate, nb, n_sel, nb)

    n_h, d = q.shape
    ones = jnp.ones((d, LANES), BF16)
    m = jnp.broadcast_to(jnp.sum(q * kn_ref[...], axis=-1, keepdims=True) * scale, (n_h, LANES))
    l = jnp.ones_like(m)
    acc = vn_ref[...]
    for pg in range(n_pages):
        n = pg // ppb
        keep = jnp.broadcast_to(sel[:, n:n + 1], (n_h, LANES)) > 0.5
        prod = (k_refs[pg][...] * q[None]).reshape(page * n_h, d).astype(BF16)
        s = _dot(prod, ones).reshape(page, n_h, LANES) * scale
        s = jnp.where(keep[None], s, NEG_INF)
        m_new = jnp.maximum(m, jnp.max(s, axis=0))
        alpha = jnp.exp(m - m_new)
        p = jnp.exp(s - m_new[None])
        l = alpha * l + jnp.sum(p, axis=0)
        acc = alpha * acc + jnp.sum(p * v_refs[pg][...], axis=0)
        m = m_new
    o_ref[...] = acc / l


def _moba_decode(q, k_new, v_new, cache_k, cache_v, page_table, n_heads):
    nseq, n_pages = page_table.shape
    page = cache_k.shape[1]
    d = HEAD_DIM_A
    nb_all = -(-(n_pages * page + 1) // MOBA_BLOCK)
    n_sel = max(1, min(MOBA_TOPK, nb_all - 1))
    assert MOBA_BLOCK % page == 0 and (n_pages * page) % MOBA_BLOCK == 0
    hd = pl.BlockSpec((None, n_heads, d), lambda b, pt: (b, 0, 0))

    def page_spec(pg):
        return pl.BlockSpec((None, page, n_heads, d), lambda b, pt: (pt[b * n_pages + pg], 0, 0, 0))

    specs = [page_spec(pg) for pg in range(n_pages)]
    q3 = q.reshape(nseq, n_heads, d)
    return pl.pallas_call(
        functools.partial(_moba_decode_kernel, n_pages=n_pages, page=page, n_sel=n_sel),
        grid_spec=pltpu.PrefetchScalarGridSpec(
            num_scalar_prefetch=1,
            grid=(nseq,),
            in_specs=[hd, hd, hd] + specs + specs,
            out_specs=hd),
        out_shape=jax.ShapeDtypeStruct((nseq, n_heads, d), F32),
        compiler_params=_params(("parallel",)),
        name="moba_decode",
    )(page_table.reshape(-1), q3, k_new.reshape(nseq, n_heads, d), v_new.reshape(nseq, n_heads, d),
      *([cache_k] * n_pages), *([cache_v] * n_pages)).reshape(nseq, n_heads * d)


GLA_UNROLL = 4


def _gla_out(o, gn, g):
    return _rms(o) * gn * (g * jax.nn.sigmoid(g))


def _gla_prompt_kernel(q_ref, k_ref, v_ref, g_ref, alr_ref, wa2_ref, ba_ref, gn_ref, o_ref, sfin_ref, la_s, st_s,
                       *, n_chunks):
    c = GLA_CHUNK
    dk = q_ref.shape[1]
    z = _dot(alr_ref[...].astype(BF16), wa2_ref[...].astype(BF16)) + ba_ref[...]
    la_s[...] = _log_sigmoid(z) * (1.0 / GLA_TAU)
    st_s[...] = jnp.zeros(st_s.shape, F32)
    row = lax.broadcasted_iota(jnp.int32, (c, c), 0)
    col = lax.broadcasted_iota(jnp.int32, (c, c), 1)
    causal = col <= row
    tril = causal.astype(BF16)
    gn = gn_ref[...]

    def body(i, carry):
        r0 = pl.multiple_of(i * c, c)
        qc = q_ref[pl.ds(r0, c), :] * (dk ** -0.5)
        kc = k_ref[pl.ds(r0, c), :]
        vc = v_ref[pl.ds(r0, c), :].astype(BF16)
        ac = la_s[pl.ds(r0, c), :]
        a1 = ac.astype(BF16)
        r1 = ac - a1.astype(F32)
        a2 = r1.astype(BF16)
        a3 = (r1 - a2.astype(F32)).astype(BF16)
        b = _dot(tril, a1) + _dot(tril, a2) + _dot(tril, a3)
        q_t = (qc * jnp.exp(b)).astype(BF16)
        k_t = (kc * jnp.exp(-b)).astype(BF16)
        att = jnp.where(causal, _dot_nt(q_t, k_t), 0.0)
        b_last = b[c - 1:c, :]
        k_dec = (kc * jnp.exp(b_last - b)).astype(BF16)
        o_intra = _dot(att.astype(BF16), vc)
        st_in = _dot_tn(k_dec, vc)
        decay = _col_from_row(jnp.exp(b_last))
        st = st_s[...]
        o = o_intra + _dot(q_t, st.astype(BF16))
        st_s[...] = decay * st + st_in
        o_ref[pl.ds(r0, c), :] = _gla_out(o, gn, g_ref[pl.ds(r0, c), :])
        return carry

    lax.fori_loop(0, n_chunks, body, 0, unroll=GLA_UNROLL)
    sfin_ref[...] = st_s[...]


def _gla_prompt(proj3, alr3, wa2, ba, gn, cols):
    b, s, _ = proj3.shape
    dk, dv = LANES, 2 * LANES
    h = N_HEADS_B
    assert s % GLA_CHUNK == 0
    sk = lambda off: pl.BlockSpec((None, s, dk), lambda bi, hi: (bi, 0, off + hi))
    sv = lambda off: pl.BlockSpec((None, s, dv), lambda bi, hi: (bi, 0, off + hi))
    return pl.pallas_call(
        functools.partial(_gla_prompt_kernel, n_chunks=s // GLA_CHUNK),
        grid=(b, h),
        in_specs=[sk(cols["q_b"] // dk), sk(cols["k_b"] // dk), sv(cols["v_b"] // dv), sv(cols["g_b"] // dv),
                  pl.BlockSpec((None, s, LANES), lambda bi, hi: (bi, 0, 0)),
                  pl.BlockSpec((LANES, dk), lambda bi, hi: (0, hi)),
                  pl.BlockSpec((1, dk), lambda bi, hi: (0, hi)),
                  pl.BlockSpec((1, dv), lambda bi, hi: (0, 0))],
        out_specs=[pl.BlockSpec((None, s, dv), lambda bi, hi: (bi, 0, hi)),
                   pl.BlockSpec((None, None, dk, dv), lambda bi, hi: (bi, hi, 0, 0))],
        out_shape=[jax.ShapeDtypeStruct((b, s, h * dv), F32), jax.ShapeDtypeStruct((b, h, dk, dv), F32)],
        scratch_shapes=[pltpu.VMEM((s, dk), F32), pltpu.VMEM((dk, dv), F32)],
        compiler_params=_params(("parallel", "parallel")),
        name="gla_prompt",
    )(proj3, proj3, proj3, proj3, alr3, wa2, ba, gn)


def _gla_decode_kernel(q_ref, k_ref, v_ref, g_ref, alr_ref, wa2_ref, ba_ref, gn_ref, s0_ref, o_ref, s_ref):
    dk, dv = LANES, 2 * LANES
    alr = jnp.broadcast_to(alr_ref[...], (8, LANES)).astype(BF16)
    z = _dot(alr, wa2_ref[...].astype(BF16))[0:1, :] + ba_ref[...]
    a = jnp.exp(_log_sigmoid(z) * (1.0 / GLA_TAU))
    gn = gn_ref[...]
    for h in range(N_HEADS_B):
        ks = slice(h * dk, (h + 1) * dk)
        vs = slice(h * dv, (h + 1) * dv)
        a_col = _col_from_row(a[:, ks])
        k_col = _col_from_row(k_ref[:, ks])
        q_col = _col_from_row(q_ref[:, ks] * (dk ** -0.5))
        s_new = a_col * s0_ref[h] + k_col * v_ref[:, vs]
        s_ref[h] = s_new
        o = jnp.sum(q_col * s_new, axis=0, keepdims=True)
        o_ref[:, vs] = _gla_out(o, gn, g_ref[:, vs])


def _gla_decode(proj_s, alr_s, wa2, ba, gn, state, cols):
    t, n = proj_s.shape
    dk, dv = LANES, 2 * LANES
    h = N_HEADS_B
    p3 = proj_s.reshape(t, 1, n)
    a3 = alr_s.reshape(t, 1, LANES)
    blk = lambda w, off: pl.BlockSpec((None, 1, w), lambda i: (i, 0, off // w))
    full = lambda a: pl.BlockSpec(a.shape, lambda i: (0,) * a.ndim)
    st = pl.BlockSpec((None, h, dk, dv), lambda i: (i, 0, 0, 0))
    o, s_new = pl.pallas_call(
        _gla_decode_kernel,
        grid=(t,),
        in_specs=[blk(h * dk, cols["q_b"]), blk(h * dk, cols["k_b"]), blk(h * dv, cols["v_b"]),
                  blk(h * dv, cols["g_b"]), blk(LANES, 0), full(wa2), full(ba), full(gn), st],
        out_specs=[pl.BlockSpec((None, 1, h * dv), lambda i: (i, 0, 0)), st],
        out_shape=[jax.ShapeDtypeStruct((t, 1, h * dv), F32), jax.ShapeDtypeStruct(state.shape, F32)],
        compiler_params=_params(("parallel",)),
        name="gla_decode",
    )(p3, p3, p3, p3, a3, wa2, ba, gn, state)
    return o.reshape(t, h * dv), s_new


def _merge_kernel(oa_ref, ob_ref, ga_ref, gb_ref, x_ref, wa_ref, wb_ref, wo_ref, n2_ref, rwh_ref, rwl_ref, rb_ref,
                  x1_ref, h2_ref, lg_ref):
    ya = _dot(oa_ref[...].astype(BF16), wa_ref[...])
    yb = _dot(ob_ref[...].astype(BF16), wb_ref[...])
    merged = jax.nn.sigmoid(ga_ref[...]) * ya + jax.nn.sigmoid(gb_ref[...]) * yb
    x1 = x_ref[...] + _dot(merged.astype(BF16), wo_ref[...])
    x1_ref[...] = x1
    h2 = _rms(x1) * n2_ref[...]
    hh, hl = _split2(h2)
    lg_ref[...] = _dot(hh, rwh_ref[...]) + _dot(hl, rwh_ref[...]) + _dot(hh, rwl_ref[...]) + rb_ref[...]
    h2_ref[...] = _pack_bf16_pairs(hh)


def _merge(o_a, o_b, proj, x, wa, wb, wo, n2, rwh, rwl, rb, cols, tm):
    t, d = x.shape
    row = lambda w, off: pl.BlockSpec((tm, w), lambda i: (i, off // w))
    full = lambda a: pl.BlockSpec(a.shape, lambda i: (0,) * a.ndim)
    return pl.pallas_call(
        _merge_kernel,
        grid=(t // tm,),
        in_specs=[row(o_a.shape[1], 0), row(o_b.shape[1], 0), row(d, cols["gate_a"]), row(d, cols["gate_b"]),
                  row(d, 0), full(wa), full(wb), full(wo), full(n2), full(rwh), full(rwl), full(rb)],
        out_specs=[row(d, 0), row(d // 2, 0), row(LANES, 0)],
        out_shape=[jax.ShapeDtypeStruct((t, d), F32), jax.ShapeDtypeStruct((t, d // 2), jnp.uint32),
                   jax.ShapeDtypeStruct((t, LANES), F32)],
        compiler_params=_params(("parallel",)),
        name="merge_out_proj",
    )(o_a, o_b, proj, proj, x, wa, wb, wo, n2, rwh, rwl, rb)


def _router_kernel(lg_ref, idx_ref, w_ref, rank_ref, cnt_ref, carry_s):
    @pl.when(pl.program_id(0) == 0)
    def _():
        carry_s[...] = jnp.zeros(carry_s.shape, F32)

    l = lg_ref[...]
    tm = l.shape[0]
    lane = lax.broadcasted_iota(jnp.int32, l.shape, 1).astype(F32)
    vals, idxs = [], []
    for _ in range(MOE_TOPK):
        mx = jnp.max(l, axis=-1, keepdims=True)
        ix = jnp.min(jnp.where(l == mx, lane, float(LANES)), axis=-1, keepdims=True)
        vals.append(mx)
        idxs.append(ix)
        l = jnp.where(lane == ix, -jnp.inf, l)
    es = [jnp.exp(v - vals[0]) for v in vals]
    tot = es[0]
    for e in es[1:]:
        tot = tot + e
    onehot = jnp.zeros(l.shape, F32)
    for ix in idxs:
        onehot = onehot + jnp.where(lane == ix, 1.0, 0.0)
    r = lax.broadcasted_iota(jnp.int32, (tm, tm), 0)
    c = lax.broadcasted_iota(jnp.int32, (tm, tm), 1)
    before = _dot((c < r).astype(BF16), onehot.astype(BF16)) + carry_s[...]
    idx_o = jnp.zeros(l.shape, F32)
    w_o = jnp.zeros(l.shape, F32)
    rank_o = jnp.zeros(l.shape, F32)
    for k in range(MOE_TOPK):
        rk = jnp.sum(jnp.where(lane == idxs[k], before, 0.0), axis=-1, keepdims=True)
        idx_o = jnp.where(lane == k, idxs[k], idx_o)
        w_o = jnp.where(lane == k, es[k] / tot, w_o)
        rank_o = jnp.where(lane == k, rk, rank_o)
    idx_ref[...] = idx_o.astype(jnp.int32)
    w_ref[...] = w_o
    rank_ref[...] = rank_o.astype(jnp.int32)
    carry_s[...] = carry_s[...] + jnp.sum(onehot, axis=0, keepdims=True)
    cnt_ref[...] = carry_s[...].astype(jnp.int32)


def _router(logits, tm):
    t = logits.shape[0]
    row = pl.BlockSpec((tm, LANES), lambda i: (i, 0))
    return pl.pallas_call(
        _router_kernel,
        grid=(t // tm,),
        in_specs=[row],
        out_specs=[row, row, row, pl.BlockSpec((1, LANES), lambda i: (0, 0))],
        out_shape=[jax.ShapeDtypeStruct((t, LANES), jnp.int32), jax.ShapeDtypeStruct((t, LANES), F32),
                   jax.ShapeDtypeStruct((t, LANES), jnp.int32), jax.ShapeDtypeStruct((1, LANES), jnp.int32)],
        scratch_shapes=[pltpu.VMEM((1, LANES), F32)],
        compiler_params=_params(("arbitrary",)),
        name="router",
    )(logits)


SUBLANES = 8


def _dispatch_kernel(dest_ref, x_ref, xs_in_hbm, xs_hbm, sem):
    del xs_in_hbm
    groups = x_ref.shape[0]
    per_group = SUBLANES * MOE_TOPK
    base = pl.program_id(0) * groups * per_group

    def issue(g, carry):
        for s in range(SUBLANES):
            for k in range(MOE_TOPK):
                dst = dest_ref[base + g * per_group + s * MOE_TOPK + k]
                pltpu.make_async_copy(x_ref.at[g, pl.ds(s, 1), :], xs_hbm.at[pl.ds(dst, 1), :], sem).start()
        return carry

    lax.fori_loop(0, groups, issue, 0)
    for _ in range(per_group):
        pltpu.make_async_copy(x_ref.at[:, 0, :], xs_hbm.at[pl.ds(0, groups), :], sem).wait()


def _dispatch(dest, x_packed, xs, tm):
    t, w = x_packed.shape
    x3 = x_packed.reshape(t // SUBLANES, SUBLANES, w)
    return pl.pallas_call(
        _dispatch_kernel,
        grid_spec=pltpu.PrefetchScalarGridSpec(
            num_scalar_prefetch=1,
            grid=(t // tm,),
            in_specs=[pl.BlockSpec((tm // SUBLANES, SUBLANES, w), lambda i, dst: (i, 0, 0)),
                      pl.BlockSpec(memory_space=pl.ANY)],
            out_specs=pl.BlockSpec(memory_space=pl.ANY),
            scratch_shapes=[pltpu.SemaphoreType.DMA(())]),
        out_shape=jax.ShapeDtypeStruct(xs.shape, xs.dtype),
        input_output_aliases={2: 0},
        compiler_params=_params(("arbitrary",)),
        name="moe_dispatch",
    )(dest, x3, xs)


def _stream_expert_weights(first_ref, nxt_ref, be_ref, w_hbms, stages, casts, sems, tn):
    j, i = pl.program_id(0), pl.program_id(1)
    nj = pl.num_programs(0)

    def copies(e, jj):
        c0 = pl.multiple_of(jj * tn, tn)
        return [pltpu.make_async_copy(w.at[e, :, pl.ds(c0, tn)], st, sems.at[n])
                for n, (w, st) in enumerate(zip(w_hbms, stages))]

    @pl.when(first_ref[i] == 1)
    def _():
        @pl.when((j == 0) & (i == 0))
        def _():
            for c in copies(be_ref[0], 0):
                c.start()

        for c in copies(be_ref[i], j):
            c.wait()
        for st, cs in zip(stages, casts):
            cs[...] = st[...].astype(BF16)
        nxt = nxt_ref[i]

        @pl.when(nxt >= 0)
        def _():
            for c in copies(nxt, j):
                c.start()

        @pl.when((nxt < 0) & (j + 1 < nj))
        def _():
            for c in copies(be_ref[0], j + 1):
                c.start()


def _expert_up_kernel(be_ref, first_ref, nxt_ref, nu_ref, x_ref, wg_hbm, wu_hbm, bg_ref, bu_ref, o_ref,
                      stg_g, stg_u, wg_s, wu_s, sems):
    _stream_expert_weights(first_ref, nxt_ref, be_ref, (wg_hbm, wu_hbm), (stg_g, stg_u), (wg_s, wu_s), sems,
                           o_ref.shape[1])
    used = pl.program_id(1) < nu_ref[0]

    @pl.when(used)
    def _():
        x = _unpack_bf16_pairs(x_ref[...])
        gate = jnp.minimum(_dot(x, wg_s[...]) + bg_ref[...], SWIGLU_LIMIT)
        up = jnp.clip(_dot(x, wu_s[...]) + bu_ref[...], -SWIGLU_LIMIT, SWIGLU_LIMIT)
        o_ref[...] = ((up + 1.0) * (gate * jax.nn.sigmoid(SWIGLU_ALPHA * gate))).astype(BF16)

    @pl.when(jnp.logical_not(used))
    def _():
        o_ref[...] = jnp.zeros(o_ref.shape, o_ref.dtype)


def _expert_down_kernel(be_ref, first_ref, nxt_ref, nu_ref, h_ref, wd_hbm, bd_ref, o_ref, stg, wd_s, sems):
    _stream_expert_weights(first_ref, nxt_ref, be_ref, (wd_hbm,), (stg,), (wd_s,), sems, wd_s.shape[1])
    used = pl.program_id(1) < nu_ref[0]

    @pl.when(used)
    def _():
        o_ref[...] = _pack_bf16_pairs((_dot(h_ref[...], wd_s[...]) + bd_ref[...]).astype(BF16))

    @pl.when(jnp.logical_not(used))
    def _():
        o_ref[...] = jnp.zeros(o_ref.shape, o_ref.dtype)


def _used_block(i, nu):
    return jnp.minimum(i, nu[0] - 1)


def _expert_up(meta, xs, wg, wu, bg, bu, bm, tf):
    p, half = xs.shape
    d, f = wg.shape[1], wg.shape[2]
    bspec = pl.BlockSpec((None, 1, tf), lambda j, i, be, fi, nx, nu: (be[i], 0, j))
    hbm = pl.BlockSpec(memory_space=pl.ANY)
    return pl.pallas_call(
        _expert_up_kernel,
        grid_spec=pltpu.PrefetchScalarGridSpec(
            num_scalar_prefetch=4,
            grid=(f // tf, p // bm),
            in_specs=[pl.BlockSpec((bm, half), lambda j, i, be, fi, nx, nu: (_used_block(i, nu), 0)),
                      hbm, hbm, bspec, bspec],
            out_specs=pl.BlockSpec((bm, tf), lambda j, i, be, fi, nx, nu: (i, j)),
            scratch_shapes=[pltpu.VMEM((d, tf), F32), pltpu.VMEM((d, tf), F32),
                            pltpu.VMEM((d, tf), BF16), pltpu.VMEM((d, tf), BF16),
                            pltpu.SemaphoreType.DMA((2,))]),
        out_shape=jax.ShapeDtypeStruct((p, f), BF16),
        compiler_params=_params(("arbitrary", "arbitrary")),
        name="moe_up",
    )(*meta, xs, wg, wu, bg, bu)


def _expert_down(meta, hid, wd, bd, bm):
    p, f = hid.shape
    d = wd.shape[2]
    return pl.pallas_call(
        _expert_down_kernel,
        grid_spec=pltpu.PrefetchScalarGridSpec(
            num_scalar_prefetch=4,
            grid=(1, p // bm),
            in_specs=[pl.BlockSpec((bm, f), lambda j, i, be, fi, nx, nu: (_used_block(i, nu), 0)),
                      pl.BlockSpec(memory_space=pl.ANY),
                      pl.BlockSpec((None, 1, d), lambda j, i, be, fi, nx, nu: (be[i], 0, 0))],
            out_specs=pl.BlockSpec((bm, d // 2), lambda j, i, be, fi, nx, nu: (i, 0)),
            scratch_shapes=[pltpu.VMEM((f, d), F32), pltpu.VMEM((f, d), BF16), pltpu.SemaphoreType.DMA((1,))]),
        out_shape=jax.ShapeDtypeStruct((p, d // 2), jnp.uint32),
        compiler_params=_params(("arbitrary", "arbitrary")),
        name="moe_down",
    )(*meta, hid, wd, bd)


def _combine_kernel(dest_ref, ys_hbm, x1_ref, w_ref, p_ref, pw_ref, pg_ref, gw_ref, y_ref, buf, sems):
    tm = x1_ref.shape[0]
    groups = tm // SUBLANES
    per_group = SUBLANES * MOE_TOPK
    i = pl.program_id(0)
    slot = i % 2

    def fetch(tile, sl):
        base = tile * tm * MOE_TOPK

        def issue(g, carry):
            for s in range(SUBLANES):
                for k in range(MOE_TOPK):
                    dst = dest_ref[base + g * per_group + s * MOE_TOPK + k]
                    pltpu.make_async_copy(ys_hbm.at[pl.ds(dst, 1), :], buf.at[sl, k, g, pl.ds(s, 1), :],
                                          sems.at[sl]).start()
            return carry

        lax.fori_loop(0, groups, issue, 0)

    @pl.when(i == 0)
    def _():
        fetch(0, 0)

    @pl.when(i + 1 < pl.num_programs(0))
    def _():
        fetch(i + 1, 1 - slot)

    e = _rms(_dot(p_ref[...].astype(BF16), pw_ref[...])) * pg_ref[...]
    for k in range(MOE_TOPK):
        for s in range(SUBLANES):
            pltpu.make_async_copy(ys_hbm.at[pl.ds(0, groups), :], buf.at[slot, k, :, s, :], sems.at[slot]).wait()
    w = w_ref[...]
    x2 = x1_ref[...]
    for k in range(MOE_TOPK):
        u = buf[slot, k].reshape(tm, buf.shape[-1])
        rows = jnp.concatenate([lax.bitcast_convert_type(u << 16, F32),
                                lax.bitcast_convert_type(u & jnp.uint32(0xFFFF0000), F32)], axis=1)
        x2 = x2 + w[:, k:k + 1] * rows
    gate = jax.nn.sigmoid(_dot(_rms(x2).astype(BF16), gw_ref[...]))
    y_ref[...] = x2 + gate * e


def _combine(dest, ys, x1, w, p, pw, pg, gw, tm):
    t, d = x1.shape
    full = lambda a: pl.BlockSpec(a.shape, lambda i, dst: (0,) * a.ndim)
    row = lambda width: pl.BlockSpec((tm, width), lambda i, dst: (i, 0))
    return pl.pallas_call(
        _combine_kernel,
        grid_spec=pltpu.PrefetchScalarGridSpec(
            num_scalar_prefetch=1,
            grid=(t // tm,),
            in_specs=[pl.BlockSpec(memory_space=pl.ANY), row(d), row(LANES), row(p.shape[1]),
                      full(pw), full(pg), full(gw)],
            out_specs=row(d),
            scratch_shapes=[pltpu.VMEM((2, MOE_TOPK, tm // SUBLANES, SUBLANES, d // 2), jnp.uint32),
                            pltpu.SemaphoreType.DMA((2,))]),
        out_shape=jax.ShapeDtypeStruct((t, d), F32),
        compiler_params=_params(("arbitrary",)),
        name="moe_combine_ple",
    )(dest, ys, x1, w, p, pw, pg, gw)


def _largest_tile(n, cap, quantum):
    best = None
    for t in range(quantum, cap + 1, quantum):
        if n % t == 0:
            best = t
    assert best is not None, (n, cap, quantum)
    return best


def kernel(x_prompt, x_sample, cache_k, cache_v, state_gla, page_table, p_prompt, p_sample, norm1_g, w_in, q_norm_g, k_norm_g, gla_w_a2, gla_b_a, gla_norm_g, w_branch_a, w_branch_b, w_out, norm2_g, router_w, router_b, moe_w_gate, moe_b_gate, moe_w_up, moe_b_up, moe_w_down, moe_b_down, ple_w, ple_norm_g, ple_gate_w):
    assert norm1_g.shape[0] == 1, "one layer"
    b, s, d = x_prompt.shape
    nseq = x_sample.shape[0]
    assert x_sample.shape[1] == 1
    n_heads = d // (2 * HEAD_DIM_A)
    wa = n_heads * HEAD_DIM_A
    kb_w = N_HEADS_B * LANES
    vb_w = 2 * kb_w
    n_exp = router_w.shape[2]
    tp, ts = b * s, nseq

    sizes = [("q_a", wa), ("k_a", wa), ("v_a", wa), ("q_b", kb_w), ("k_b", kb_w), ("v_b", vb_w), ("g_b", vb_w),
             ("a_lr", GLA_LOWRANK), ("gate_a", d), ("gate_b", d)]
    cols, parts, src, off = {}, [], 0, 0
    w_in0 = w_in[0]
    for name, width in sizes:
        part = w_in0[:, src:src + width]
        src += width
        if name == "a_lr":
            w_side = jnp.pad(part, ((0, 0), (0, LANES - width))).astype(BF16)
            continue
        cols[name] = off
        parts.append(part)
        off += width
    w_main = jnp.concatenate(parts, axis=1).astype(BF16)
    n_main = w_main.shape[1]
    tn = _largest_tile(n_main, 1024, 2 * LANES)

    g1 = norm1_g[0][None, :]
    qg, kg = q_norm_g[0][None, :], k_norm_g[0][None, :]
    wa2 = jnp.pad(gla_w_a2[0], ((0, LANES - GLA_LOWRANK), (0, 0)))
    ba = gla_b_a[0][None, :]
    gn = gla_norm_g[0][None, :]
    wba, wbb, wo = w_branch_a[0].astype(BF16), w_branch_b[0].astype(BF16), w_out[0].astype(BF16)
    n2 = norm2_g[0][None, :]
    rw = jnp.pad(router_w[0], ((0, 0), (0, LANES - n_exp)))
    rwh = rw.astype(BF16)
    rwl = (rw - rwh.astype(F32)).astype(BF16)
    rb = jnp.concatenate([router_b[0], jnp.full((LANES - n_exp,), -jnp.inf, F32)])[None, :]

    xp = x_prompt.reshape(tp, d)
    proj_p, alr_p = _norm_matmul(xp, g1, w_main, w_side, _largest_tile(tp, 1024, 8), tn)
    tr = _largest_tile(s, 512, 8)
    tabs_p = _rope_tables(jnp.arange(s, dtype=jnp.int32))
    q_p, k_p, v_p = _qk_rope(proj_p, qg, kg, tabs_p, n_heads, tr, lambda i: i % (s // tr))
    o_a_p = _moba_prompt(q_p.reshape(b, s, wa), k_p.reshape(b, s, wa), v_p.reshape(b, s, wa), n_heads)
    o_b_p, gla_p = _gla_prompt(proj_p.reshape(b, s, n_main), alr_p.reshape(b, s, LANES), wa2, ba, gn, cols)
    x1_p, h2_p, lg_p = _merge(o_a_p.reshape(tp, wa), o_b_p.reshape(tp, vb_w), proj_p, xp, wba, wbb, wo, n2,
                              rwh, rwl, rb, cols, _largest_tile(tp, 128, 8))

    xs_ = x_sample.reshape(ts, d)
    past = page_table.shape[1] * cache_k.shape[2]
    proj_s, alr_s = _norm_matmul(xs_, g1, w_main, w_side, ts, tn)
    tabs_s = tuple(jnp.broadcast_to(t_, (ts, HEAD_DIM_A)) for t_ in _rope_tables(jnp.full((1,), past, jnp.int32)))
    q_s, k_s, v_s = _qk_rope(proj_s, qg, kg, tabs_s, n_heads, ts, lambda i: i)
    o_a_s = _moba_decode(q_s, k_s, v_s, cache_k[0], cache_v[0], page_table, n_heads)
    o_b_s, gla_s = _gla_decode(proj_s, alr_s, wa2, ba, gn, state_gla[0], cols)
    x1_s, h2_s, lg_s = _merge(o_a_s, o_b_s, proj_s, xs_, wba, wbb, wo, n2, rwh, rwl, rb, cols, ts)

    t_all = tp + ts
    logits = jnp.concatenate([lg_p, lg_s], axis=0)
    idx_o, w_o, rank_o, cnt = _router(logits, _largest_tile(t_all, 512, LANES))
    bm = MOE_ROW_BLOCK
    n_assign = t_all * MOE_TOPK
    n_blocks = -(-n_assign // bm) + n_exp
    counts = cnt[0, :n_exp]
    pcounts = (counts + bm - 1) // bm * bm
    pend = jnp.cumsum(pcounts)
    pstart = pend - pcounts
    expert_ids = jnp.arange(n_exp, dtype=jnp.int32)
    start_of = jnp.sum(jnp.where(idx_o[:, :MOE_TOPK, None] == expert_ids, pstart, 0), axis=-1)
    dest = (start_of + rank_o[:, :MOE_TOPK]).reshape(-1)
    n_used = pend[n_exp - 1] // bm
    blk = jnp.arange(n_blocks, dtype=jnp.int32)
    blk = jnp.minimum(blk, n_used - 1)
    block_e = jnp.sum(pend[None, :] <= (blk * bm)[:, None], axis=1).astype(jnp.int32)
    first = jnp.concatenate([jnp.ones((1,), jnp.int32), (block_e[1:] != block_e[:-1]).astype(jnp.int32)])
    nxt_pos = jnp.sum(block_e[None, :] <= block_e[:, None], axis=1)
    nxt = jnp.where(nxt_pos < n_blocks, block_e[jnp.minimum(nxt_pos, n_blocks - 1)], -1).astype(jnp.int32)
    meta = (block_e, first, nxt, n_used.reshape(1).astype(jnp.int32))

    xs_rows = jnp.zeros((n_blocks * bm, d // 2), jnp.uint32)
    xs_rows = _dispatch(dest[:tp * MOE_TOPK], h2_p, xs_rows, _largest_tile(tp, 512, 8))
    xs_rows = _dispatch(dest[tp * MOE_TOPK:], h2_s, xs_rows, ts)
    f = moe_w_gate.shape[3]
    hid = _expert_up(meta, xs_rows, moe_w_gate[0], moe_w_up[0], moe_b_gate[0][:, None, :],
                     moe_b_up[0][:, None, :], bm, _largest_tile(f, 1024, 2 * LANES))
    ys = _expert_down(meta, hid, moe_w_down[0], moe_b_down[0][:, None, :], bm)

    pw, pg, gw = ple_w[0].astype(BF16), ple_norm_g[0][None, :], ple_gate_w[0].astype(BF16)
    tc = _largest_tile(ts, 128, 8)
    assert tp % tc == 0
    y_p = _combine(dest[:tp * MOE_TOPK], ys, x1_p, w_o[:tp], p_prompt[0].reshape(tp, -1), pw, pg, gw, tc)
    y_s = _combine(dest[tp * MOE_TOPK:], ys, x1_s, w_o[tp:], p_sample[0].reshape(ts, -1), pw, pg, gw, tc)

    hd = (n_heads, HEAD_DIM_A)
    return (y_p.reshape(b, s, d), y_s.reshape(nseq, 1, d),
            k_p.reshape(1, b, s, *hd), v_p.reshape(1, b, s, *hd), gla_p[None],
            k_s.reshape(1, nseq, 1, *hd), v_s.reshape(1, nseq, 1, *hd), gla_s[None])
```

```python
import functools

import jax
import jax.numpy as jnp
from jax import lax
from jax.experimental import pallas as pl
from jax.experimental.pallas import tpu as pltpu

HEAD_DIM_A = 128
MOBA_BLOCK = 256
MOBA_TOPK = 3
ROT_DIM = HEAD_DIM_A // 4
ROPE_THETA = 500000.0
N_HEADS_B = 4
GLA_LOWRANK = 16
GLA_TAU = 16.0
GLA_CHUNK = 64
MOE_TOPK = 4
SWIGLU_LIMIT = 7.0
SWIGLU_ALPHA = 1.702
EPS = 1e-6
NEG_INF = -1e30

LANES = 128
SUBLANES = 8
MOE_ROW_BLOCK = 256
VMEM_LIMIT = 56 * 1024 * 1024

F32 = jnp.float32
BF16 = jnp.bfloat16


def _params(sem):
    return pltpu.CompilerParams(dimension_semantics=sem, vmem_limit_bytes=VMEM_LIMIT)


def _dot(a, b):
    return jnp.dot(a, b, preferred_element_type=F32)


def _dot_nt(a, b):
    return lax.dot_general(a, b, (((1,), (1,)), ((), ())), preferred_element_type=F32)


def _dot_tn(a, b):
    return lax.dot_general(a, b, (((0,), (0,)), ((), ())), preferred_element_type=F32)


def _split2(x):
    hi = x.astype(BF16)
    lo = (x - hi.astype(F32)).astype(BF16)
    return hi, lo


def _rms(x):
    return x * lax.rsqrt(jnp.mean(x * x, axis=-1, keepdims=True) + EPS)


def _col_from_row(row):
    n = row.shape[1]
    r = lax.broadcasted_iota(jnp.int32, (n, n), 0)
    c = lax.broadcasted_iota(jnp.int32, (n, n), 1)
    return jnp.sum(jnp.where(r == c, jnp.broadcast_to(row, (n, n)), 0.0), axis=1, keepdims=True)


def _log_sigmoid(z):
    return jnp.minimum(z, 0.0) - jnp.log(1.0 + jnp.exp(-jnp.abs(z)))


def _pack_bf16_pairs(x16):
    half = x16.shape[1] // 2
    bits = lax.bitcast_convert_type(x16.astype(F32), jnp.uint32)
    return (bits[:, :half] >> 16) | bits[:, half:]


def _unpack_bf16_pairs(u):
    lo = lax.bitcast_convert_type(u << 16, F32).astype(BF16)
    hi = lax.bitcast_convert_type(u & jnp.uint32(0xFFFF0000), F32).astype(BF16)
    return jnp.concatenate([lo, hi], axis=1)


def _norm_matmul_kernel(x_ref, g_ref, *refs, tiles):
    w_refs = refs[:len(tiles)]
    ws_ref, o_ref, os_ref, h_ref = refs[len(tiles):]
    j = pl.program_id(1)

    @pl.when(j == 0)
    def _():
        h_ref[...] = (_rms(x_ref[...]) * g_ref[...]).astype(BF16)
        os_ref[...] = _dot(h_ref[...], ws_ref[...])

    start = 0
    for w_ref, n_tiles in zip(w_refs, tiles):
        @pl.when((j >= start) & (j < start + n_tiles))
        def _(w_ref=w_ref):
            o_ref[...] = _dot(h_ref[...], w_ref[...])

        start += n_tiles


def _norm_matmul(x, g, w_parts, w_side, tm, tn):
    m, d = x.shape
    tiles = tuple(w.shape[1] // tn for w in w_parts)
    assert all(w.shape[1] % tn == 0 for w in w_parts)
    n = sum(tiles) * tn
    ns = w_side.shape[1]
    starts = [sum(tiles[:p]) for p in range(len(tiles))]

    def part_spec(start, n_tiles):
        return pl.BlockSpec((d, tn), lambda i, j: (0, jnp.clip(j - start, 0, n_tiles - 1)))

    return pl.pallas_call(
        functools.partial(_norm_matmul_kernel, tiles=tiles),
        grid=(m // tm, sum(tiles)),
        in_specs=[pl.BlockSpec((tm, d), lambda i, j: (i, 0)),
                  pl.BlockSpec((1, d), lambda i, j: (0, 0))]
                 + [part_spec(s, t) for s, t in zip(starts, tiles)]
                 + [pl.BlockSpec((d, ns), lambda i, j: (0, 0))],
        out_specs=[pl.BlockSpec((tm, tn), lambda i, j: (i, j)), pl.BlockSpec((tm, ns), lambda i, j: (i, 0))],
        out_shape=[jax.ShapeDtypeStruct((m, n), F32), jax.ShapeDtypeStruct((m, ns), F32)],
        scratch_shapes=[pltpu.VMEM((tm, d), BF16)],
        compiler_params=_params(("parallel", "arbitrary")),
        name="in_proj",
    )(x, g, *w_parts, w_side)


def _qk_rope_kernel(q_ref, k_ref, v_ref, qg_ref, kg_ref, c_ref, sn_ref, sp_ref, qo_ref, ko_ref, vo_ref):
    c, sn, sp = c_ref[...], sn_ref[...], sp_ref[...]
    half = ROT_DIM // 2

    def norm_rope(x, g):
        y = _rms(x) * g
        up = pltpu.roll(y, LANES - half, 1)
        dn = pltpu.roll(y, half, 1)
        return y * c + up * sn + dn * sp

    qg, kg = qg_ref[...], kg_ref[...]
    for h in range(q_ref.shape[1] // HEAD_DIM_A):
        hs = slice(h * HEAD_DIM_A, (h + 1) * HEAD_DIM_A)
        qo_ref[:, hs] = norm_rope(q_ref[:, hs], qg)
        ko_ref[:, hs] = norm_rope(k_ref[:, hs], kg)
    vo_ref[...] = v_ref[...]


def _qk_rope(proj, qg, kg, tabs, n_heads, tr, tab_index):
    t = proj.shape[0]
    w = n_heads * HEAD_DIM_A
    col = lambda j: pl.BlockSpec((tr, w), lambda i: (i, j))
    vec = pl.BlockSpec((1, HEAD_DIM_A), lambda i: (0, 0))
    tab = pl.BlockSpec((tr, HEAD_DIM_A), lambda i: (tab_index(i), 0))
    return pl.pallas_call(
        _qk_rope_kernel,
        grid=(t // tr,),
        in_specs=[col(0), col(1), col(2), vec, vec, tab, tab, tab],
        out_specs=[col(0), col(0), col(0)],
        out_shape=[jax.ShapeDtypeStruct((t, w), F32)] * 3,
        compiler_params=_params(("parallel",)),
        name="qk_rope",
    )(proj, proj, proj, qg, kg, *tabs)


def _rope_tables(pos):
    half = ROT_DIM // 2
    inv = ROPE_THETA ** (-jnp.arange(half, dtype=F32) * 2.0 / ROT_DIM)
    ang = pos.astype(F32)[:, None] * inv[None, :]
    cos, sin = jnp.cos(ang), jnp.sin(ang)
    n = pos.shape[0]
    rest = HEAD_DIM_A - ROT_DIM
    c = jnp.concatenate([cos, cos, jnp.ones((n, rest), F32)], axis=1)
    sn = jnp.concatenate([-sin, jnp.zeros((n, half + rest), F32)], axis=1)
    sp = jnp.concatenate([jnp.zeros((n, half), F32), sin, jnp.zeros((n, rest), F32)], axis=1)
    return c, sn, sp


def _block_select(gate, n_cand, n_sel, max_cand):
    lane = lax.broadcasted_iota(jnp.int32, gate.shape, 1)
    gate = jnp.where(lane < n_cand, gate, -jnp.inf)
    rank = jnp.zeros(gate.shape, F32)
    for m in range(max_cand):
        gm = jnp.broadcast_to(gate[:, m:m + 1], gate.shape)
        beats = (gm > gate) | ((gm == gate) & (lane > m))
        rank = rank + jnp.where(beats, 1.0, 0.0)
    return jnp.where((rank < n_sel) & (lane < n_cand), 1.0, 0.0)


def _moba_prompt_kernel(q_ref, k_ref, v_ref, o_ref, kaug_s, v16_s, *, nb, n_sel):
    blk, d = MOBA_BLOCK, HEAD_DIM_A
    s_len = nb * blk
    kaug_s[:, :d] = k_ref[...].astype(BF16)
    key_blk = lax.broadcasted_iota(jnp.int32, (s_len, LANES), 0) // blk
    kaug_s[:, d:] = (key_blk == lax.broadcasted_iota(jnp.int32, (s_len, LANES), 1)).astype(BF16)
    v16_s[...] = v_ref[...].astype(BF16)
    means = [jnp.mean(k_ref[n * blk:(n + 1) * blk, :], axis=0, keepdims=True) for n in range(nb)]
    kmean = jnp.concatenate(means + [jnp.zeros((LANES - nb, d), F32)], axis=0)
    kh, kl = _split2(kmean)
    lane = lax.broadcasted_iota(jnp.int32, (blk, LANES), 1)
    row = lax.broadcasted_iota(jnp.int32, (blk, blk), 0)
    col = lax.broadcasted_iota(jnp.int32, (blk, blk), 1)
    scale = d ** -0.5
    for qb in range(nb):
        q = q_ref[qb * blk:(qb + 1) * blk, :]
        qh, ql = _split2(q)
        if qb > 0:
            gate = _dot_nt(qh, kh) + _dot_nt(qh, kl) + _dot_nt(ql, kh)
            keep = jnp.where(lane == qb, 1.0, _block_select(gate, qb, n_sel, qb))
        else:
            keep = jnp.where(lane == qb, 1.0, 0.0)
        bias = ((1.0 - keep) * NEG_INF).astype(BF16)
        nk = (qb + 1) * blk
        s = _dot_nt(jnp.concatenate([qh, bias], axis=1), kaug_s[:nk, :]) * scale
        s_own = jnp.where(col <= row, s[:, qb * blk:], NEG_INF)
        s = s_own if qb == 0 else jnp.concatenate([s[:, :qb * blk], s_own], axis=1)
        m = jnp.max(s, axis=-1, keepdims=True)
        p = jnp.exp(s - m)
        l = jnp.sum(p, axis=-1, keepdims=True)
        o_ref[qb * blk:(qb + 1) * blk, :] = _dot(p.astype(BF16), v16_s[:nk, :]) / l


def _moba_prompt(q, k, v, n_heads):
    b, s, _ = q.shape
    blk = MOBA_BLOCK
    assert s % blk == 0 and s // blk <= LANES
    nb = s // blk
    n_sel = max(1, min(MOBA_TOPK, nb - 1))
    d = HEAD_DIM_A
    spec = pl.BlockSpec((None, s, d), lambda bi, h: (bi, 0, h))
    return pl.pallas_call(
        functools.partial(_moba_prompt_kernel, nb=nb, n_sel=n_sel),
        grid=(b, n_heads),
        in_specs=[spec, spec, spec],
        out_specs=spec,
        out_shape=jax.ShapeDtypeStruct(q.shape, F32),
        scratch_shapes=[pltpu.VMEM((s, d + LANES), BF16), pltpu.VMEM((s, d), BF16)],
        compiler_params=_params(("parallel", "parallel")),
        name="moba_prompt",
    )(q, k, v)


def _moba_decode_kernel(pt_ref, q_ref, kn_ref, vn_ref, *refs, n_pages, page, n_sel):
    k_refs = refs[:n_pages]
    v_refs = refs[n_pages:2 * n_pages]
    o_ref = refs[2 * n_pages]
    ppb = MOBA_BLOCK // page
    nb = n_pages // ppb
    q = q_ref[...]
    scale = HEAD_DIM_A ** -0.5

    lane = lax.broadcasted_iota(jnp.int32, (q.shape[0], LANES), 1)
    gate = jnp.zeros((q.shape[0], LANES), F32)
    for n in range(nb):
        ksum = jnp.zeros(q.shape, F32)
        for j in range(ppb):
            ksum = ksum + jnp.sum(k_refs[n * ppb + j][...], axis=0)
        g = jnp.sum(q * (ksum * (1.0 / MOBA_BLOCK)), axis=-1, keepdims=True)
        gate = jnp.where(lane == n, g, gate)
    sel = _block_select(gate, nb, n_sel, nb)

    n_h, d = q.shape
    ones = jnp.ones((d, LANES), BF16)
    m = jnp.broadcast_to(jnp.sum(q * kn_ref[...], axis=-1, keepdims=True) * scale, (n_h, LANES))
    l = jnp.ones_like(m)
    acc = vn_ref[...]
    for pg in range(n_pages):
        n = pg // ppb
        keep = jnp.broadcast_to(sel[:, n:n + 1], (n_h, LANES)) > 0.5
        prod = (k_refs[pg][...] * q[None]).reshape(page * n_h, d).astype(BF16)
        s = _dot(prod, ones).reshape(page, n_h, LANES) * scale
        s = jnp.where(keep[None], s, NEG_INF)
        m_new = jnp.maximum(m, jnp.max(s, axis=0))
        alpha = jnp.exp(m - m_new)
        p = jnp.exp(s - m_new[None])
        l = alpha * l + jnp.sum(p, axis=0)
        acc = alpha * acc + jnp.sum(p * v_refs[pg][...], axis=0)
        m = m_new
    o_ref[...] = acc / l


def _moba_decode(q, k_new, v_new, cache_k, cache_v, page_table, n_heads):
    nseq, n_pages = page_table.shape
    page = cache_k.shape[1]
    d = HEAD_DIM_A
    nb_all = -(-(n_pages * page + 1) // MOBA_BLOCK)
    n_sel = max(1, min(MOBA_TOPK, nb_all - 1))
    assert MOBA_BLOCK % page == 0 and (n_pages * page) % MOBA_BLOCK == 0
    hd = pl.BlockSpec((None, n_heads, d), lambda b, pt: (b, 0, 0))

    def page_spec(pg):
        return pl.BlockSpec((None, page, n_heads, d), lambda b, pt: (pt[b * n_pages + pg], 0, 0, 0))

    specs = [page_spec(pg) for pg in range(n_pages)]
    q3 = q.reshape(nseq, n_heads, d)
    return pl.pallas_call(
        functools.partial(_moba_decode_kernel, n_pages=n_pages, page=page, n_sel=n_sel),
        grid_spec=pltpu.PrefetchScalarGridSpec(
            num_scalar_prefetch=1,
            grid=(nseq,),
            in_specs=[hd, hd, hd] + specs + specs,
            out_specs=hd),
        out_shape=jax.ShapeDtypeStruct((nseq, n_heads, d), F32),
        compiler_params=_params(("parallel",)),
        name="moba_decode",
    )(page_table.reshape(-1), q3, k_new.reshape(nseq, n_heads, d), v_new.reshape(nseq, n_heads, d),
      *([cache_k] * n_pages), *([cache_v] * n_pages)).reshape(nseq, n_heads * d)


GLA_UNROLL = 4


def _gla_out(o, gn, g):
    return _rms(o) * gn * (g * jax.nn.sigmoid(g))


def _gla_prompt_kernel(q_ref, k_ref, v_ref, g_ref, alr_ref, wa2_ref, ba_ref, gn_ref, o_ref, sfin_ref, la_s, st_s,
                       *, n_chunks):
    c = GLA_CHUNK
    dk = q_ref.shape[1]
    z = _dot(alr_ref[...].astype(BF16), wa2_ref[...].astype(BF16)) + ba_ref[...]
    la_s[...] = _log_sigmoid(z) * (1.0 / GLA_TAU)
    st_s[...] = jnp.zeros(st_s.shape, F32)
    row = lax.broadcasted_iota(jnp.int32, (c, c), 0)
    col = lax.broadcasted_iota(jnp.int32, (c, c), 1)
    causal = col <= row
    tril = causal.astype(BF16)
    gn = gn_ref[...]

    def body(i, carry):
        r0 = pl.multiple_of(i * c, c)
        qc = q_ref[pl.ds(r0, c), :] * (dk ** -0.5)
        kc = k_ref[pl.ds(r0, c), :]
        vc = v_ref[pl.ds(r0, c), :].astype(BF16)
        ac = la_s[pl.ds(r0, c), :]
        a1 = ac.astype(BF16)
        r1 = ac - a1.astype(F32)
        a2 = r1.astype(BF16)
        a3 = (r1 - a2.astype(F32)).astype(BF16)
        b = _dot(tril, a1) + _dot(tril, a2) + _dot(tril, a3)
        q_t = (qc * jnp.exp(b)).astype(BF16)
        k_t = (kc * jnp.exp(-b)).astype(BF16)
        att = jnp.where(causal, _dot_nt(q_t, k_t), 0.0)
        b_last = b[c - 1:c, :]
        k_dec = (kc * jnp.exp(b_last - b)).astype(BF16)
        o_intra = _dot(att.astype(BF16), vc)
        st_in = _dot_tn(k_dec, vc)
        decay = _col_from_row(jnp.exp(b_last))
        st = st_s[...]
        o = o_intra + _dot(q_t, st.astype(BF16))
        st_s[...] = decay * st + st_in
        o_ref[pl.ds(r0, c), :] = _gla_out(o, gn, g_ref[pl.ds(r0, c), :])
        return carry

    lax.fori_loop(0, n_chunks, body, 0, unroll=GLA_UNROLL)
    sfin_ref[...] = st_s[...]


def _gla_prompt(proj3, alr3, wa2, ba, gn, cols):
    b, s, _ = proj3.shape
    dk, dv = LANES, 2 * LANES
    h = N_HEADS_B
    assert s % GLA_CHUNK == 0
    sk = lambda off: pl.BlockSpec((None, s, dk), lambda bi, hi: (bi, 0, off + hi))
    sv = lambda off: pl.BlockSpec((None, s, dv), lambda bi, hi: (bi, 0, off + hi))
    return pl.pallas_call(
        functools.partial(_gla_prompt_kernel, n_chunks=s // GLA_CHUNK),
        grid=(b, h),
        in_specs=[sk(cols["q_b"] // dk), sk(cols["k_b"] // dk), sv(cols["v_b"] // dv), sv(cols["g_b"] // dv),
                  pl.BlockSpec((None, s, LANES), lambda bi, hi: (bi, 0, 0)),
                  pl.BlockSpec((LANES, dk), lambda bi, hi: (0, hi)),
                  pl.BlockSpec((1, dk), lambda bi, hi: (0, hi)),
                  pl.BlockSpec((1, dv), lambda bi, hi: (0, 0))],
        out_specs=[pl.BlockSpec((None, s, dv), lambda bi, hi: (bi, 0, hi)),
                   pl.BlockSpec((None, None, dk, dv), lambda bi, hi: (bi, hi, 0, 0))],
        out_shape=[jax.ShapeDtypeStruct((b, s, h * dv), F32), jax.ShapeDtypeStruct((b, h, dk, dv), F32)],
        scratch_shapes=[pltpu.VMEM((s, dk), F32), pltpu.VMEM((dk, dv), F32)],
        compiler_params=_params(("parallel", "parallel")),
        name="gla_prompt",
    )(proj3, proj3, proj3, proj3, alr3, wa2, ba, gn)


def _gla_decode_kernel(q_ref, k_ref, v_ref, g_ref, alr_ref, wa2_ref, ba_ref, gn_ref, s0_ref, o_ref, s_ref):
    dk, dv = LANES, 2 * LANES
    wa2 = wa2_ref[...].astype(BF16)
    gn = gn_ref[...]

    def one_sequence(s, carry):
        alr = jnp.broadcast_to(alr_ref[s], (SUBLANES, LANES)).astype(BF16)
        z = _dot(alr, wa2)[0:1, :] + ba_ref[...]
        a = jnp.exp(_log_sigmoid(z) * (1.0 / GLA_TAU))
        q, k, v, g = q_ref[s], k_ref[s], v_ref[s], g_ref[s]
        outs = []
        for h in range(N_HEADS_B):
            ks = slice(h * dk, (h + 1) * dk)
            vs = slice(h * dv, (h + 1) * dv)
            a_col = _col_from_row(a[:, ks])
            k_col = _col_from_row(k[:, ks])
            q_col = _col_from_row(q[:, ks] * (dk ** -0.5))
            s_new = a_col * s0_ref[s, h] + k_col * v[:, vs]
            s_ref[s, h] = s_new
            o = jnp.sum(q_col * s_new, axis=0, keepdims=True)
            outs.append(_gla_out(o, gn, g[:, vs]))
        o_ref[s] = jnp.concatenate(outs, axis=1)
        return carry

    lax.fori_loop(0, q_ref.shape[0], one_sequence, 0)


def _gla_decode(proj_s, alr_s, wa2, ba, gn, state, cols):
    t, n = proj_s.shape
    dk, dv = LANES, 2 * LANES
    h = N_HEADS_B
    p3 = proj_s.reshape(t, 1, n)
    a3 = alr_s.reshape(t, 1, LANES)
    sb = _largest_tile(t, SUBLANES, 1)
    blk = lambda w, off: pl.BlockSpec((sb, 1, w), lambda i: (i, 0, off // w))
    full = lambda a: pl.BlockSpec(a.shape, lambda i: (0,) * a.ndim)
    st = pl.BlockSpec((sb, h, dk, dv), lambda i: (i, 0, 0, 0))
    o, s_new = pl.pallas_call(
        _gla_decode_kernel,
        grid=(t // sb,),
        in_specs=[blk(h * dk, cols["q_b"]), blk(h * dk, cols["k_b"]), blk(h * dv, cols["v_b"]),
                  blk(h * dv, cols["g_b"]), blk(LANES, 0), full(wa2), full(ba), full(gn), st],
        out_specs=[pl.BlockSpec((sb, 1, h * dv), lambda i: (i, 0, 0)), st],
        out_shape=[jax.ShapeDtypeStruct((t, 1, h * dv), F32), jax.ShapeDtypeStruct(state.shape, F32)],
        compiler_params=_params(("parallel",)),
        name="gla_decode",
    )(p3, p3, p3, p3, a3, wa2, ba, gn, state)
    return o.reshape(t, h * dv), s_new


def _merge_kernel(oa_ref, ob_ref, ga_ref, gb_ref, x_ref, wa_ref, wb_ref, wo_ref, n2_ref, rwh_ref, rwl_ref, rb_ref,
                  x1_ref, h2_ref, lg_ref):
    ya = _dot(oa_ref[...].astype(BF16), wa_ref[...])
    yb = _dot(ob_ref[...].astype(BF16), wb_ref[...])
    merged = jax.nn.sigmoid(ga_ref[...]) * ya + jax.nn.sigmoid(gb_ref[...]) * yb
    x1 = x_ref[...] + _dot(merged.astype(BF16), wo_ref[...])
    x1_ref[...] = x1
    h2 = _rms(x1) * n2_ref[...]
    hh, hl = _split2(h2)
    lg_ref[...] = _dot(hh, rwh_ref[...]) + _dot(hl, rwh_ref[...]) + _dot(hh, rwl_ref[...]) + rb_ref[...]
    h2_ref[...] = _pack_bf16_pairs(hh)


def _merge(o_a, o_b, proj, x, wa, wb, wo, n2, rwh, rwl, rb, cols, tm):
    t, d = x.shape
    row = lambda w, off: pl.BlockSpec((tm, w), lambda i: (i, off // w))
    full = lambda a: pl.BlockSpec(a.shape, lambda i: (0,) * a.ndim)
    return pl.pallas_call(
        _merge_kernel,
        grid=(t // tm,),
        in_specs=[row(o_a.shape[1], 0), row(o_b.shape[1], 0), row(d, cols["gate_a"]), row(d, cols["gate_b"]),
                  row(d, 0), full(wa), full(wb), full(wo), full(n2), full(rwh), full(rwl), full(rb)],
        out_specs=[row(d, 0), row(d // 2, 0), row(LANES, 0)],
        out_shape=[jax.ShapeDtypeStruct((t, d), F32), jax.ShapeDtypeStruct((t, d // 2), jnp.uint32),
                   jax.ShapeDtypeStruct((t, LANES), F32)],
        compiler_params=_params(("parallel",)),
        name="merge_out_proj",
    )(o_a, o_b, proj, proj, x, wa, wb, wo, n2, rwh, rwl, rb)


def _router_kernel(lg_ref, idx_ref, w_ref, rank_ref, cnt_ref, carry_s):
    @pl.when(pl.program_id(0) == 0)
    def _():
        carry_s[...] = jnp.zeros(carry_s.shape, F32)

    l = lg_ref[...]
    tm = l.shape[0]
    lane = lax.broadcasted_iota(jnp.int32, l.shape, 1).astype(F32)
    vals, idxs = [], []
    for _ in range(MOE_TOPK):
        mx = jnp.max(l, axis=-1, keepdims=True)
        ix = jnp.min(jnp.where(l == mx, lane, float(LANES)), axis=-1, keepdims=True)
        vals.append(mx)
        idxs.append(ix)
        l = jnp.where(lane == ix, -jnp.inf, l)
    es = [jnp.exp(v - vals[0]) for v in vals]
    tot = es[0]
    for e in es[1:]:
        tot = tot + e
    onehot = jnp.zeros(l.shape, F32)
    for ix in idxs:
        onehot = onehot + jnp.where(lane == ix, 1.0, 0.0)
    r = lax.broadcasted_iota(jnp.int32, (tm, tm), 0)
    c = lax.broadcasted_iota(jnp.int32, (tm, tm), 1)
    before = _dot((c < r).astype(BF16), onehot.astype(BF16)) + carry_s[...]
    idx_o = jnp.zeros(l.shape, F32)
    w_o = jnp.zeros(l.shape, F32)
    rank_o = jnp.zeros(l.shape, F32)
    for k in range(MOE_TOPK):
        rk = jnp.sum(jnp.where(lane == idxs[k], before, 0.0), axis=-1, keepdims=True)
        idx_o = jnp.where(lane == k, idxs[k], idx_o)
        w_o = jnp.where(lane == k, es[k] / tot, w_o)
        rank_o = jnp.where(lane == k, rk, rank_o)
    idx_ref[...] = idx_o.astype(jnp.int32)
    w_ref[...] = w_o
    rank_ref[...] = rank_o.astype(jnp.int32)
    carry_s[...] = carry_s[...] + jnp.sum(onehot, axis=0, keepdims=True)
    cnt_ref[...] = carry_s[...].astype(jnp.int32)


def _router(logits, tm):
    t = logits.shape[0]
    row = pl.BlockSpec((tm, LANES), lambda i: (i, 0))
    return pl.pallas_call(
        _router_kernel,
        grid=(t // tm,),
        in_specs=[row],
        out_specs=[row, row, row, pl.BlockSpec((1, LANES), lambda i: (0, 0))],
        out_shape=[jax.ShapeDtypeStruct((t, LANES), jnp.int32), jax.ShapeDtypeStruct((t, LANES), F32),
                   jax.ShapeDtypeStruct((t, LANES), jnp.int32), jax.ShapeDtypeStruct((1, LANES), jnp.int32)],
        scratch_shapes=[pltpu.VMEM((1, LANES), F32)],
        compiler_params=_params(("arbitrary",)),
        name="router",
    )(logits)


def _scatter_rows(dest_ref, x_ref, xs_hbm, sem):
    groups = x_ref.shape[0]
    per_group = SUBLANES * MOE_TOPK
    base = pl.program_id(0) * groups * per_group

    def issue(g, carry):
        for s in range(SUBLANES):
            for k in range(MOE_TOPK):
                dst = dest_ref[base + g * per_group + s * MOE_TOPK + k]
                pltpu.make_async_copy(x_ref.at[g, pl.ds(s, 1), :], xs_hbm.at[pl.ds(dst, 1), :], sem).start()
        return carry

    lax.fori_loop(0, groups, issue, 0)
    for _ in range(per_group):
        pltpu.make_async_copy(x_ref.at[:, 0, :], xs_hbm.at[pl.ds(0, groups), :], sem).wait()


def _dispatch_kernel(dest_ref, x_ref, xs_in_hbm, xs_hbm, sem):
    del xs_in_hbm
    _scatter_rows(dest_ref, x_ref, xs_hbm, sem)


def _dispatch(dest, x_packed, xs, tm):
    t, w = x_packed.shape
    x3 = x_packed.reshape(t // SUBLANES, SUBLANES, w)
    hbm = pl.BlockSpec(memory_space=pl.ANY)
    return pl.pallas_call(
        _dispatch_kernel,
        grid_spec=pltpu.PrefetchScalarGridSpec(
            num_scalar_prefetch=1,
            grid=(t // tm,),
            in_specs=[pl.BlockSpec((tm // SUBLANES, SUBLANES, w), lambda i, dst: (i, 0, 0)), hbm],
            out_specs=hbm,
            scratch_shapes=[pltpu.SemaphoreType.DMA(())]),
        out_shape=jax.ShapeDtypeStruct(xs.shape, xs.dtype),
        input_output_aliases={2: 0},
        compiler_params=_params(("arbitrary",)),
        name="moe_dispatch",
    )(dest, x3, xs)


def _stream_expert_weights(first_ref, nxt_ref, be_ref, w_hbms, stages, casts, sems, tn):
    j, i = pl.program_id(0), pl.program_id(1)
    nj = pl.num_programs(0)

    def copies(e, jj):
        c0 = pl.multiple_of(jj * tn, tn)
        return [pltpu.make_async_copy(w.at[e, :, pl.ds(c0, tn)], st, sems.at[n])
                for n, (w, st) in enumerate(zip(w_hbms, stages))]

    @pl.when(first_ref[i] == 1)
    def _():
        @pl.when((j == 0) & (i == 0))
        def _():
            for c in copies(be_ref[0], 0):
                c.start()

        for c in copies(be_ref[i], j):
            c.wait()
        for st, cs in zip(stages, casts):
            cs[...] = st[...].astype(BF16)
        nxt = nxt_ref[i]

        @pl.when(nxt >= 0)
        def _():
            for c in copies(nxt, j):
                c.start()

        @pl.when((nxt < 0) & (j + 1 < nj))
        def _():
            for c in copies(be_ref[0], j + 1):
                c.start()


def _expert_up_kernel(be_ref, first_ref, nxt_ref, nu_ref, x_ref, wg_hbm, wu_hbm, bg_ref, bu_ref, o_ref,
                      stg_g, stg_u, wg_s, wu_s, sems):
    _stream_expert_weights(first_ref, nxt_ref, be_ref, (wg_hbm, wu_hbm), (stg_g, stg_u), (wg_s, wu_s), sems,
                           o_ref.shape[1])
    used = pl.program_id(1) < nu_ref[0]

    @pl.when(used)
    def _():
        x = _unpack_bf16_pairs(x_ref[...])
        half = o_ref.shape[1] // 2
        for cs in (slice(0, half), slice(half, 2 * half)):
            gate = jnp.minimum(_dot(x, wg_s[:, cs]) + bg_ref[:, cs], SWIGLU_LIMIT)
            up = jnp.clip(_dot(x, wu_s[:, cs]) + bu_ref[:, cs], -SWIGLU_LIMIT, SWIGLU_LIMIT)
            o_ref[:, cs] = ((up + 1.0) * (gate * jax.nn.sigmoid(SWIGLU_ALPHA * gate))).astype(BF16)

    @pl.when(jnp.logical_not(used))
    def _():
        o_ref[...] = jnp.zeros(o_ref.shape, o_ref.dtype)


def _expert_down_kernel(be_ref, first_ref, nxt_ref, nu_ref, h_ref, wd_hbm, bd_ref, o_ref, stg, wd_s, sems):
    _stream_expert_weights(first_ref, nxt_ref, be_ref, (wd_hbm,), (stg,), (wd_s,), sems, wd_s.shape[1])
    used = pl.program_id(1) < nu_ref[0]

    @pl.when(used)
    def _():
        o_ref[...] = _pack_bf16_pairs((_dot(h_ref[...], wd_s[...]) + bd_ref[...]).astype(BF16))

    @pl.when(jnp.logical_not(used))
    def _():
        o_ref[...] = jnp.zeros(o_ref.shape, o_ref.dtype)


def _used_block(i, nu):
    return jnp.minimum(i, nu[0] - 1)


def _expert_up(meta, xs, wg, wu, bg, bu, bm, tf):
    p, half = xs.shape
    d, f = wg.shape[1], wg.shape[2]
    bspec = pl.BlockSpec((None, 1, tf), lambda j, i, be, fi, nx, nu: (be[i], 0, j))
    hbm = pl.BlockSpec(memory_space=pl.ANY)
    return pl.pallas_call(
        _expert_up_kernel,
        grid_spec=pltpu.PrefetchScalarGridSpec(
            num_scalar_prefetch=4,
            grid=(f // tf, p // bm),
            in_specs=[pl.BlockSpec((bm, half), lambda j, i, be, fi, nx, nu: (_used_block(i, nu), 0)),
                      hbm, hbm, bspec, bspec],
            out_specs=pl.BlockSpec((bm, tf), lambda j, i, be, fi, nx, nu: (i, j)),
            scratch_shapes=[pltpu.VMEM((d, tf), F32), pltpu.VMEM((d, tf), F32),
                            pltpu.VMEM((d, tf), BF16), pltpu.VMEM((d, tf), BF16),
                            pltpu.SemaphoreType.DMA((2,))]),
        out_shape=jax.ShapeDtypeStruct((p, f), BF16),
        compiler_params=_params(("arbitrary", "arbitrary")),
        name="moe_up",
    )(*meta, xs, wg, wu, bg, bu)


def _expert_down(meta, hid, wd, bd, bm):
    p, f = hid.shape
    d = wd.shape[2]
    return pl.pallas_call(
        _expert_down_kernel,
        grid_spec=pltpu.PrefetchScalarGridSpec(
            num_scalar_prefetch=4,
            grid=(1, p // bm),
            in_specs=[pl.BlockSpec((bm, f), lambda j, i, be, fi, nx, nu: (_used_block(i, nu), 0)),
                      pl.BlockSpec(memory_space=pl.ANY),
                      pl.BlockSpec((None, 1, d), lambda j, i, be, fi, nx, nu: (be[i], 0, 0))],
            out_specs=pl.BlockSpec((bm, d // 2), lambda j, i, be, fi, nx, nu: (i, 0)),
            scratch_shapes=[pltpu.VMEM((f, d), F32), pltpu.VMEM((f, d), BF16), pltpu.SemaphoreType.DMA((1,))]),
        out_shape=jax.ShapeDtypeStruct((p, d // 2), jnp.uint32),
        compiler_params=_params(("arbitrary", "arbitrary")),
        name="moe_down",
    )(*meta, hid, wd, bd)


def _combine_kernel(dest_ref, ys_hbm, x1_ref, w_ref, p_ref, pw_ref, pg_ref, gw_ref, y_ref, buf, sems):
    tm = x1_ref.shape[0]
    groups = tm // SUBLANES
    per_group = SUBLANES * MOE_TOPK
    i = pl.program_id(0)
    slot = i % 2

    def issue_group(tile, sl, g):
        base = tile * tm * MOE_TOPK
        for s in range(SUBLANES):
            for k in range(MOE_TOPK):
                dst = dest_ref[base + g * per_group + s * MOE_TOPK + k]
                pltpu.make_async_copy(ys_hbm.at[pl.ds(dst, 1), :], buf.at[sl, k, g, pl.ds(s, 1), :],
                                      sems.at[sl]).start()

    def wait_slot(sl):
        for k in range(MOE_TOPK):
            for s in range(SUBLANES):
                pltpu.make_async_copy(ys_hbm.at[pl.ds(0, groups), :], buf.at[sl, k, :, s, :], sems.at[sl]).wait()

    last = pl.num_programs(0) - 1

    @pl.when(i == 0)
    def _():
        lax.fori_loop(0, groups, lambda g, c: (issue_group(0, 0, g), c)[1], 0)

    nxt_tile = jnp.minimum(i + 1, last)
    d = y_ref.shape[1]
    n_col = d // (2 * LANES)
    tc = d // n_col
    n_phase = 2 + MOE_TOPK + n_col
    issued = [0]

    def issue_share(phase):
        upto = groups * (phase + 1) // n_phase
        for g in range(issued[0], upto):
            issue_group(nxt_tile, 1 - slot, g)
        issued[0] = upto

    issue_share(0)
    e = _rms(_dot(p_ref[...].astype(BF16), pw_ref[...])) * pg_ref[...]
    issue_share(1)
    wait_slot(slot)
    w = w_ref[...]
    x2 = x1_ref[...]
    for k in range(MOE_TOPK):
        u = buf[slot, k].reshape(tm, buf.shape[-1])
        rows = jnp.concatenate([lax.bitcast_convert_type(u << 16, F32),
                                lax.bitcast_convert_type(u & jnp.uint32(0xFFFF0000), F32)], axis=1)
        x2 = x2 + w[:, k:k + 1] * rows
        issue_share(2 + k)
    xn = _rms(x2).astype(BF16)
    for t in range(n_col):
        cs = slice(t * tc, (t + 1) * tc)
        gate = jax.nn.sigmoid(_dot(xn, gw_ref[:, cs]))
        y_ref[:, cs] = x2[:, cs] + gate * e[:, cs]
        issue_share(2 + MOE_TOPK + t)

    @pl.when(i == last)
    def _():
        wait_slot(1 - slot)


def _combine(dest, ys, x1, w, p, pw, pg, gw, tm):
    t, d = x1.shape
    full = lambda a: pl.BlockSpec(a.shape, lambda i, dst: (0,) * a.ndim)
    row = lambda width: pl.BlockSpec((tm, width), lambda i, dst: (i, 0))
    return pl.pallas_call(
        _combine_kernel,
        grid_spec=pltpu.PrefetchScalarGridSpec(
            num_scalar_prefetch=1,
            grid=(t // tm,),
            in_specs=[pl.BlockSpec(memory_space=pl.ANY), row(d), row(LANES), row(p.shape[1]),
                      full(pw), full(pg), full(gw)],
            out_specs=row(d),
            scratch_shapes=[pltpu.VMEM((2, MOE_TOPK, tm // SUBLANES, SUBLANES, d // 2), jnp.uint32),
                            pltpu.SemaphoreType.DMA((2,))]),
        out_shape=jax.ShapeDtypeStruct((t, d), F32),
        compiler_params=_params(("arbitrary",)),
        name="moe_combine_ple",
    )(dest, ys, x1, w, p, pw, pg, gw)


def _largest_tile(n, cap, quantum):
    best = None
    for t in range(quantum, cap + 1, quantum):
        if n % t == 0:
            best = t
    assert best is not None, (n, cap, quantum)
    return best


def kernel(x_prompt, x_sample, cache_k, cache_v, state_gla, page_table, p_prompt, p_sample, norm1_g, w_in, q_norm_g, k_norm_g, gla_w_a2, gla_b_a, gla_norm_g, w_branch_a, w_branch_b, w_out, norm2_g, router_w, router_b, moe_w_gate, moe_b_gate, moe_w_up, moe_b_up, moe_w_down, moe_b_down, ple_w, ple_norm_g, ple_gate_w):
    assert norm1_g.shape[0] == 1, "one layer"
    b, s, d = x_prompt.shape
    nseq = x_sample.shape[0]
    assert x_sample.shape[1] == 1
    n_heads = d // (2 * HEAD_DIM_A)
    wa = n_heads * HEAD_DIM_A
    kb_w = N_HEADS_B * LANES
    vb_w = 2 * kb_w
    n_exp = router_w.shape[2]
    tp, ts = b * s, nseq

    sizes = [("q_a", wa), ("k_a", wa), ("v_a", wa), ("q_b", kb_w), ("k_b", kb_w), ("v_b", vb_w), ("g_b", vb_w),
             ("a_lr", GLA_LOWRANK), ("gate_a", d), ("gate_b", d)]
    cols, src, off = {}, 0, 0
    w_in0 = w_in[0]
    for name, width in sizes:
        if name == "a_lr":
            w_side = jnp.pad(w_in0[:, src:src + width], ((0, 0), (0, LANES - width))).astype(BF16)
            split = src
        else:
            cols[name] = off
            off += width
        src += width
    w_parts = (w_in0[:, :split].astype(BF16), w_in0[:, split + GLA_LOWRANK:].astype(BF16))
    n_main = off
    tn = 1024
    assert split % tn == 0 and (n_main - split) % tn == 0

    g1 = norm1_g[0][None, :]
    qg, kg = q_norm_g[0][None, :], k_norm_g[0][None, :]
    wa2 = jnp.pad(gla_w_a2[0], ((0, LANES - GLA_LOWRANK), (0, 0)))
    ba = gla_b_a[0][None, :]
    gn = gla_norm_g[0][None, :]
    wba, wbb, wo = w_branch_a[0].astype(BF16), w_branch_b[0].astype(BF16), w_out[0].astype(BF16)
    n2 = norm2_g[0][None, :]
    rw = jnp.pad(router_w[0], ((0, 0), (0, LANES - n_exp)))
    rwh = rw.astype(BF16)
    rwl = (rw - rwh.astype(F32)).astype(BF16)
    rb = jnp.concatenate([router_b[0], jnp.full((LANES - n_exp,), -jnp.inf, F32)])[None, :]

    xp = x_prompt.reshape(tp, d)
    proj_p, alr_p = _norm_matmul(xp, g1, w_parts, w_side, _largest_tile(tp, 1024, 8), tn)
    tr = _largest_tile(s, 512, 8)
    tabs_p = _rope_tables(jnp.arange(s, dtype=jnp.int32))
    q_p, k_p, v_p = _qk_rope(proj_p, qg, kg, tabs_p, n_heads, tr, lambda i: i % (s // tr))
    o_a_p = _moba_prompt(q_p.reshape(b, s, wa), k_p.reshape(b, s, wa), v_p.reshape(b, s, wa), n_heads)
    o_b_p, gla_p = _gla_prompt(proj_p.reshape(b, s, n_main), alr_p.reshape(b, s, LANES), wa2, ba, gn, cols)
    x1_p, h2_p, lg_p = _merge(o_a_p.reshape(tp, wa), o_b_p.reshape(tp, vb_w), proj_p, xp, wba, wbb, wo, n2,
                              rwh, rwl, rb, cols, _largest_tile(tp, 128, 8))

    xs_ = x_sample.reshape(ts, d)
    past = page_table.shape[1] * cache_k.shape[2]
    proj_s, alr_s = _norm_matmul(xs_, g1, w_parts, w_side, ts, tn)
    tabs_s = tuple(jnp.broadcast_to(t_, (ts, HEAD_DIM_A)) for t_ in _rope_tables(jnp.full((1,), past, jnp.int32)))
    q_s, k_s, v_s = _qk_rope(proj_s, qg, kg, tabs_s, n_heads, ts, lambda i: i)
    o_a_s = _moba_decode(q_s, k_s, v_s, cache_k[0], cache_v[0], page_table, n_heads)
    o_b_s, gla_s = _gla_decode(proj_s, alr_s, wa2, ba, gn, state_gla[0], cols)
    x1_s, h2_s, lg_s = _merge(o_a_s, o_b_s, proj_s, xs_, wba, wbb, wo, n2, rwh, rwl, rb, cols, ts)

    t_all = tp + ts
    logits = jnp.concatenate([lg_p, lg_s], axis=0)
    idx_o, w_o, rank_o, cnt = _router(logits, _largest_tile(t_all, 512, LANES))
    bm = MOE_ROW_BLOCK
    n_assign = t_all * MOE_TOPK
    n_blocks = -(-n_assign // bm) + n_exp
    counts = cnt[0, :n_exp]
    pcounts = (counts + bm - 1) // bm * bm
    pend = jnp.cumsum(pcounts)
    pstart = pend - pcounts
    expert_ids = jnp.arange(n_exp, dtype=jnp.int32)
    start_of = jnp.sum(jnp.where(idx_o[:, :MOE_TOPK, None] == expert_ids, pstart, 0), axis=-1)
    dest = (start_of + rank_o[:, :MOE_TOPK]).reshape(-1)
    n_used = pend[n_exp - 1] // bm
    blk = jnp.arange(n_blocks, dtype=jnp.int32)
    blk = jnp.minimum(blk, n_used - 1)
    block_e = jnp.sum(pend[None, :] <= (blk * bm)[:, None], axis=1).astype(jnp.int32)
    first = jnp.concatenate([jnp.ones((1,), jnp.int32), (block_e[1:] != block_e[:-1]).astype(jnp.int32)])
    nxt_pos = jnp.sum(block_e[None, :] <= block_e[:, None], axis=1)
    nxt = jnp.where(nxt_pos < n_blocks, block_e[jnp.minimum(nxt_pos, n_blocks - 1)], -1).astype(jnp.int32)
    meta = (block_e, first, nxt, n_used.reshape(1).astype(jnp.int32))

    xs_rows = jnp.zeros((n_blocks * bm, d // 2), jnp.uint32)
    xs_rows = _dispatch(dest[:tp * MOE_TOPK], h2_p, xs_rows, _largest_tile(tp, 512, 8))
    xs_rows = _dispatch(dest[tp * MOE_TOPK:], h2_s, xs_rows, ts)
    f = moe_w_gate.shape[3]
    hid = _expert_up(meta, xs_rows, moe_w_gate[0], moe_w_up[0], moe_b_gate[0][:, None, :],
                     moe_b_up[0][:, None, :], bm, _largest_tile(f, 1024, 2 * LANES))
    ys = _expert_down(meta, hid, moe_w_down[0], moe_b_down[0][:, None, :], bm)

    pw, pg, gw = ple_w[0].astype(BF16), ple_norm_g[0][None, :], ple_gate_w[0].astype(BF16)
    tc = _largest_tile(ts, 128, 8)
    assert tp % tc == 0
    y_p = _combine(dest[:tp * MOE_TOPK], ys, x1_p, w_o[:tp], p_prompt[0].reshape(tp, -1), pw, pg, gw, tc)
    y_s = _combine(dest[tp * MOE_TOPK:], ys, x1_s, w_o[tp:], p_sample[0].reshape(ts, -1), pw, pg, gw, tc)

    hd = (n_heads, HEAD_DIM_A)
    return (y_p.reshape(b, s, d), y_s.reshape(nseq, 1, d),
            k_p.reshape(1, b, s, *hd), v_p.reshape(1, b, s, *hd), gla_p[None],
            k_s.reshape(1, nseq, 1, *hd), v_s.reshape(1, nseq, 1, *hd), gla_s[None])
```

```python
import functools

import jax
import jax.numpy as jnp
from jax import lax
from jax.experimental import pallas as pl
from jax.experimental.pallas import tpu as pltpu

HEAD_DIM_A = 128
MOBA_BLOCK = 256
MOBA_TOPK = 3
ROT_DIM = HEAD_DIM_A // 4
ROPE_THETA = 500000.0
N_HEADS_B = 4
GLA_LOWRANK = 16
GLA_TAU = 16.0
GLA_CHUNK = 64
MOE_TOPK = 4
SWIGLU_LIMIT = 7.0
SWIGLU_ALPHA = 1.702
EPS = 1e-6
NEG_INF = -1e30

LANES = 128
SUBLANES = 8
MOE_ROW_BLOCK = 256
VMEM_LIMIT = 56 * 1024 * 1024

F32 = jnp.float32
BF16 = jnp.bfloat16


def _params(sem):
    return pltpu.CompilerParams(dimension_semantics=sem, vmem_limit_bytes=VMEM_LIMIT)


def _dot(a, b):
    return jnp.dot(a, b, preferred_element_type=F32)


def _dot_nt(a, b):
    return lax.dot_general(a, b, (((1,), (1,)), ((), ())), preferred_element_type=F32)


def _dot_tn(a, b):
    return lax.dot_general(a, b, (((0,), (0,)), ((), ())), preferred_element_type=F32)


def _split2(x):
    hi = x.astype(BF16)
    lo = (x - hi.astype(F32)).astype(BF16)
    return hi, lo


def _rms(x):
    return x * lax.rsqrt(jnp.mean(x * x, axis=-1, keepdims=True) + EPS)


def _col_from_row(row):
    n = row.shape[1]
    r = lax.broadcasted_iota(jnp.int32, (n, n), 0)
    c = lax.broadcasted_iota(jnp.int32, (n, n), 1)
    return jnp.sum(jnp.where(r == c, jnp.broadcast_to(row, (n, n)), 0.0), axis=1, keepdims=True)


def _log_sigmoid(z):
    return jnp.minimum(z, 0.0) - jnp.log(1.0 + jnp.exp(-jnp.abs(z)))


def _pack_bf16_pairs(x16):
    half = x16.shape[1] // 2
    bits = lax.bitcast_convert_type(x16.astype(F32), jnp.uint32)
    return (bits[:, :half] >> 16) | bits[:, half:]


def _unpack_bf16_pairs(u):
    lo = lax.bitcast_convert_type(u << 16, F32).astype(BF16)
    hi = lax.bitcast_convert_type(u & jnp.uint32(0xFFFF0000), F32).astype(BF16)
    return jnp.concatenate([lo, hi], axis=1)


def _norm_matmul_kernel(x_ref, g_ref, w_ref, ws_ref, o_ref, os_ref, h_ref):
    @pl.when(pl.program_id(1) == 0)
    def _():
        h_ref[...] = (_rms(x_ref[...]) * g_ref[...]).astype(BF16)
        os_ref[...] = _dot(h_ref[...], ws_ref[...])

    o_ref[...] = _dot(h_ref[...], w_ref[...])


def _norm_matmul(x, g, w_tiles, w_side):
    m, d = x.shape
    n_tiles, _, tn = w_tiles.shape
    ns = w_side.shape[1]
    tm = _largest_tile(m, 1024, SUBLANES)
    return pl.pallas_call(
        _norm_matmul_kernel,
        grid=(m // tm, n_tiles),
        in_specs=[pl.BlockSpec((tm, d), lambda i, j: (i, 0)),
                  pl.BlockSpec((1, d), lambda i, j: (0, 0)),
                  pl.BlockSpec((None, d, tn), lambda i, j: (j, 0, 0)),
                  pl.BlockSpec((d, ns), lambda i, j: (0, 0))],
        out_specs=[pl.BlockSpec((tm, tn), lambda i, j: (i, j)), pl.BlockSpec((tm, ns), lambda i, j: (i, 0))],
        out_shape=[jax.ShapeDtypeStruct((m, n_tiles * tn), F32), jax.ShapeDtypeStruct((m, ns), F32)],
        scratch_shapes=[pltpu.VMEM((tm, d), BF16)],
        compiler_params=_params(("parallel", "arbitrary")),
        name="in_proj",
    )(x, g, w_tiles, w_side)


def _qk_rope_kernel(q_ref, k_ref, v_ref, qg_ref, kg_ref, c_ref, sn_ref, sp_ref, qo_ref, ko_ref, vo_ref):
    c, sn, sp = c_ref[...], sn_ref[...], sp_ref[...]
    half = ROT_DIM // 2

    def norm_rope(x, g):
        y = _rms(x) * g
        up = pltpu.roll(y, LANES - half, 1)
        dn = pltpu.roll(y, half, 1)
        return y * c + up * sn + dn * sp

    qg, kg = qg_ref[...], kg_ref[...]
    for h in range(q_ref.shape[1] // HEAD_DIM_A):
        hs = slice(h * HEAD_DIM_A, (h + 1) * HEAD_DIM_A)
        qo_ref[:, hs] = norm_rope(q_ref[:, hs], qg)
        ko_ref[:, hs] = norm_rope(k_ref[:, hs], kg)
    vo_ref[...] = v_ref[...]


def _qk_rope(proj, qg, kg, tabs, n_heads, tr, tab_index):
    t = proj.shape[0]
    w = n_heads * HEAD_DIM_A
    col = lambda j: pl.BlockSpec((tr, w), lambda i: (i, j))
    vec = pl.BlockSpec((1, HEAD_DIM_A), lambda i: (0, 0))
    tab = pl.BlockSpec((tr, HEAD_DIM_A), lambda i: (tab_index(i), 0))
    return pl.pallas_call(
        _qk_rope_kernel,
        grid=(t // tr,),
        in_specs=[col(0), col(1), col(2), vec, vec, tab, tab, tab],
        out_specs=[col(0), col(0), col(0)],
        out_shape=[jax.ShapeDtypeStruct((t, w), F32)] * 3,
        compiler_params=_params(("parallel",)),
        name="qk_rope",
    )(proj, proj, proj, qg, kg, *tabs)


def _rope_tables(pos):
    half = ROT_DIM // 2
    inv = ROPE_THETA ** (-jnp.arange(half, dtype=F32) * 2.0 / ROT_DIM)
    ang = pos.astype(F32)[:, None] * inv[None, :]
    cos, sin = jnp.cos(ang), jnp.sin(ang)
    n = pos.shape[0]
    rest = HEAD_DIM_A - ROT_DIM
    c = jnp.concatenate([cos, cos, jnp.ones((n, rest), F32)], axis=1)
    sn = jnp.concatenate([-sin, jnp.zeros((n, half + rest), F32)], axis=1)
    sp = jnp.concatenate([jnp.zeros((n, half), F32), sin, jnp.zeros((n, rest), F32)], axis=1)
    return c, sn, sp


def _block_select(gate, n_cand, n_sel, max_cand):
    lane = lax.broadcasted_iota(jnp.int32, gate.shape, 1)
    gate = jnp.where(lane < n_cand, gate, -jnp.inf)
    rank = jnp.zeros(gate.shape, F32)
    for m in range(max_cand):
        gm = jnp.broadcast_to(gate[:, m:m + 1], gate.shape)
        beats = (gm > gate) | ((gm == gate) & (lane > m))
        rank = rank + jnp.where(beats, 1.0, 0.0)
    return jnp.where((rank < n_sel) & (lane < n_cand), 1.0, 0.0)


def _block_select_rows(gate_t, n_cand, n_sel):
    blk_id = lax.broadcasted_iota(jnp.int32, gate_t.shape, 0)
    gate_t = jnp.where(blk_id < n_cand, gate_t, -jnp.inf)
    rank = jnp.zeros(gate_t.shape, F32)
    for m in range(n_cand):
        gm = jnp.broadcast_to(gate_t[m:m + 1, :], gate_t.shape)
        beats = (gm > gate_t) | ((gm == gate_t) & (blk_id > m))
        rank = rank + jnp.where(beats, 1.0, 0.0)
    return jnp.where((rank < n_sel) & (blk_id < n_cand), 1.0, 0.0)


def _moba_prompt_kernel(q_ref, k_ref, v_ref, o_ref, kaug_s, v16_s, *, nb, n_sel):
    blk, d = MOBA_BLOCK, HEAD_DIM_A
    s_len = nb * blk
    kaug_s[:, :d] = k_ref[...].astype(BF16)
    key_blk = lax.broadcasted_iota(jnp.int32, (s_len, LANES), 0) // blk
    kaug_s[:, d:] = (key_blk == lax.broadcasted_iota(jnp.int32, (s_len, LANES), 1)).astype(BF16)
    v16_s[...] = v_ref[...].astype(BF16)
    nb_pad = -(-nb // 16) * 16
    means = [jnp.mean(k_ref[n * blk:(n + 1) * blk, :], axis=0, keepdims=True) for n in range(nb)]
    kmean = jnp.concatenate(means + [jnp.zeros((nb_pad - nb, d), F32)], axis=0)
    kh, kl = _split2(kmean)
    blk_id = lax.broadcasted_iota(jnp.int32, (nb_pad, blk), 0)
    to_lanes = (lax.broadcasted_iota(jnp.int32, (nb_pad, LANES), 0)
                == lax.broadcasted_iota(jnp.int32, (nb_pad, LANES), 1)).astype(BF16)
    row = lax.broadcasted_iota(jnp.int32, (blk, blk), 0)
    col = lax.broadcasted_iota(jnp.int32, (blk, blk), 1)
    scale = d ** -0.5
    for qb in range(nb):
        q = q_ref[qb * blk:(qb + 1) * blk, :]
        keep_t = jnp.where(blk_id == qb, 1.0, 0.0)
        if qb > 0:
            qh, ql = _split2(q)
            gate_t = _dot_nt(kh, qh) + _dot_nt(kl, qh) + _dot_nt(kh, ql)
            keep_t = keep_t + _block_select_rows(gate_t, qb, n_sel)
        keep = _dot_tn(keep_t.astype(BF16), to_lanes)
        bias = ((1.0 - keep) * NEG_INF).astype(BF16)
        nk = (qb + 1) * blk
        s = _dot_nt(jnp.concatenate([(q * scale).astype(BF16), bias], axis=1), kaug_s[:nk, :])
        s_own = jnp.where(col <= row, s[:, qb * blk:], NEG_INF)
        s = s_own if qb == 0 else jnp.concatenate([s[:, :qb * blk], s_own], axis=1)
        m = jnp.max(s, axis=-1, keepdims=True)
        p = jnp.exp(s - m)
        l = jnp.sum(p, axis=-1, keepdims=True)
        o_ref[qb * blk:(qb + 1) * blk, :] = _dot(p.astype(BF16), v16_s[:nk, :]) / l


def _moba_prompt(q, k, v, n_heads):
    b, s, _ = q.shape
    blk = MOBA_BLOCK
    assert s % blk == 0 and s // blk <= LANES
    nb = s // blk
    n_sel = max(1, min(MOBA_TOPK, nb - 1))
    d = HEAD_DIM_A
    spec = pl.BlockSpec((None, s, d), lambda bi, h: (bi, 0, h))
    return pl.pallas_call(
        functools.partial(_moba_prompt_kernel, nb=nb, n_sel=n_sel),
        grid=(b, n_heads),
        in_specs=[spec, spec, spec],
        out_specs=spec,
        out_shape=jax.ShapeDtypeStruct(q.shape, F32),
        scratch_shapes=[pltpu.VMEM((s, d + LANES), BF16), pltpu.VMEM((s, d), BF16)],
        compiler_params=_params(("parallel", "parallel")),
        name="moba_prompt",
    )(q, k, v)


def _moba_decode_kernel(pt_ref, q_ref, kn_ref, vn_ref, *refs, n_pages, page, n_sel):
    k_refs = refs[:n_pages]
    v_refs = refs[n_pages:2 * n_pages]
    o_ref = refs[2 * n_pages]
    ppb = MOBA_BLOCK // page
    nb = n_pages // ppb
    q = q_ref[...]
    scale = HEAD_DIM_A ** -0.5

    lane = lax.broadcasted_iota(jnp.int32, (q.shape[0], LANES), 1)
    gate = jnp.zeros((q.shape[0], LANES), F32)
    for n in range(nb):
        ksum = jnp.zeros(q.shape, F32)
        for j in range(ppb):
            ksum = ksum + jnp.sum(k_refs[n * ppb + j][...], axis=0)
        g = jnp.sum(q * (ksum * (1.0 / MOBA_BLOCK)), axis=-1, keepdims=True)
        gate = jnp.where(lane == n, g, gate)
    sel = _block_select(gate, nb, n_sel, nb)

    n_h, d = q.shape
    qs = q * scale
    ones = jnp.ones((d, LANES), BF16)
    m = jnp.broadcast_to(jnp.sum(qs * kn_ref[...], axis=-1, keepdims=True), (n_h, LANES))
    l = jnp.ones_like(m)
    acc = vn_ref[...]
    for pg in range(n_pages):
        n = pg // ppb
        keep = jnp.broadcast_to(sel[:, n:n + 1], (n_h, LANES)) > 0.5
        prod = (k_refs[pg][...] * qs[None]).reshape(page * n_h, d).astype(BF16)
        s = _dot(prod, ones).reshape(page, n_h, LANES)
        s = jnp.where(keep[None], s, NEG_INF)
        m_new = jnp.maximum(m, jnp.max(s, axis=0))
        alpha = jnp.exp(m - m_new)
        p = jnp.exp(s - m_new[None])
        l = alpha * l + jnp.sum(p, axis=0)
        acc = alpha * acc + jnp.sum(p * v_refs[pg][...], axis=0)
        m = m_new
    o_ref[...] = acc / l


def _moba_decode(q, k_new, v_new, cache_k, cache_v, page_table, n_heads):
    nseq, n_pages = page_table.shape
    page = cache_k.shape[1]
    d = HEAD_DIM_A
    nb_all = -(-(n_pages * page + 1) // MOBA_BLOCK)
    n_sel = max(1, min(MOBA_TOPK, nb_all - 1))
    assert MOBA_BLOCK % page == 0 and (n_pages * page) % MOBA_BLOCK == 0
    hd = pl.BlockSpec((None, n_heads, d), lambda b, pt: (b, 0, 0))

    def page_spec(pg):
        return pl.BlockSpec((None, page, n_heads, d), lambda b, pt: (pt[b * n_pages + pg], 0, 0, 0))

    specs = [page_spec(pg) for pg in range(n_pages)]
    q3 = q.reshape(nseq, n_heads, d)
    return pl.pallas_call(
        functools.partial(_moba_decode_kernel, n_pages=n_pages, page=page, n_sel=n_sel),
        grid_spec=pltpu.PrefetchScalarGridSpec(
            num_scalar_prefetch=1,
            grid=(nseq,),
            in_specs=[hd, hd, hd] + specs + specs,
            out_specs=hd),
        out_shape=jax.ShapeDtypeStruct((nseq, n_heads, d), F32),
        compiler_params=_params(("parallel",)),
        name="moba_decode",
    )(page_table.reshape(-1), q3, k_new.reshape(nseq, n_heads, d), v_new.reshape(nseq, n_heads, d),
      *([cache_k] * n_pages), *([cache_v] * n_pages)).reshape(nseq, n_heads * d)


GLA_UNROLL = 8


def _gla_out(o, gn, g):
    return _rms(o) * gn * (g * jax.nn.sigmoid(g))


def _gla_prompt_kernel(q_ref, k_ref, v_ref, g_ref, alr_ref, wa2_ref, ba_ref, gn_ref, o_ref, sfin_ref, la_s,
                       *, n_chunks):
    c = GLA_CHUNK
    dk = q_ref.shape[1]
    z = _dot(alr_ref[...].astype(BF16), wa2_ref[...].astype(BF16)) + ba_ref[...]
    la_s[...] = _log_sigmoid(z) * (1.0 / GLA_TAU)
    row = lax.broadcasted_iota(jnp.int32, (c, c), 0)
    col = lax.broadcasted_iota(jnp.int32, (c, c), 1)
    causal = col <= row
    tril = causal.astype(BF16)
    gn = gn_ref[...]

    def body(i, st):
        r0 = pl.multiple_of(i * c, c)
        qc = q_ref[pl.ds(r0, c), :] * (dk ** -0.5)
        kc = k_ref[pl.ds(r0, c), :]
        vc = v_ref[pl.ds(r0, c), :].astype(BF16)
        ac = la_s[pl.ds(r0, c), :]
        a1 = ac.astype(BF16)
        r1 = ac - a1.astype(F32)
        a2 = r1.astype(BF16)
        a3 = (r1 - a2.astype(F32)).astype(BF16)
        b = _dot(tril, a1) + _dot(tril, a2) + _dot(tril, a3)
        q_t = (qc * jnp.exp(b)).astype(BF16)
        k_t = (kc * jnp.exp(-b)).astype(BF16)
        att = jnp.where(causal, _dot_nt(q_t, k_t), 0.0)
        b_last = b[c - 1:c, :]
        k_dec = (kc * jnp.exp(b_last - b)).astype(BF16)
        o_intra = _dot(att.astype(BF16), vc)
        st_in = _dot_tn(k_dec, vc)
        decay = _col_from_row(jnp.exp(b_last))
        o = o_intra + _dot(q_t, st.astype(BF16))
        o_ref[pl.ds(r0, c), :] = _gla_out(o, gn, g_ref[pl.ds(r0, c), :])
        return decay * st + st_in

    sfin_ref[...] = lax.fori_loop(0, n_chunks, body, jnp.zeros(sfin_ref.shape, F32), unroll=GLA_UNROLL)


def _gla_prompt(proj3, alr3, wa2, ba, gn, cols):
    b, s, _ = proj3.shape
    dk, dv = LANES, 2 * LANES
    h = N_HEADS_B
    assert s % GLA_CHUNK == 0
    sk = lambda off: pl.BlockSpec((None, s, dk), lambda bi, hi: (bi, 0, off + hi))
    sv = lambda off: pl.BlockSpec((None, s, dv), lambda bi, hi: (bi, 0, off + hi))
    return pl.pallas_call(
        functools.partial(_gla_prompt_kernel, n_chunks=s // GLA_CHUNK),
        grid=(b, h),
        in_specs=[sk(cols["q_b"] // dk), sk(cols["k_b"] // dk), sv(cols["v_b"] // dv), sv(cols["g_b"] // dv),
                  pl.BlockSpec((None, s, LANES), lambda bi, hi: (bi, 0, 0)),
                  pl.BlockSpec((LANES, dk), lambda bi, hi: (0, hi)),
                  pl.BlockSpec((1, dk), lambda bi, hi: (0, hi)),
                  pl.BlockSpec((1, dv), lambda bi, hi: (0, 0))],
        out_specs=[pl.BlockSpec((None, s, dv), lambda bi, hi: (bi, 0, hi)),
                   pl.BlockSpec((None, None, dk, dv), lambda bi, hi: (bi, hi, 0, 0))],
        out_shape=[jax.ShapeDtypeStruct((b, s, h * dv), F32), jax.ShapeDtypeStruct((b, h, dk, dv), F32)],
        scratch_shapes=[pltpu.VMEM((s, dk), F32)],
        compiler_params=_params(("parallel", "parallel")),
        name="gla_prompt",
    )(proj3, proj3, proj3, proj3, alr3, wa2, ba, gn)


def _gla_decode_kernel(q_ref, k_ref, v_ref, g_ref, alr_ref, wa2_ref, ba_ref, gn_ref, s0_ref, o_ref, s_ref):
    dk, dv = LANES, 2 * LANES
    wa2 = wa2_ref[...].astype(BF16)
    gn = gn_ref[...]

    def one_sequence(s, carry):
        alr = jnp.broadcast_to(alr_ref[s], (SUBLANES, LANES)).astype(BF16)
        z = _dot(alr, wa2)[0:1, :] + ba_ref[...]
        a = jnp.exp(_log_sigmoid(z) * (1.0 / GLA_TAU))
        q, k, v, g = q_ref[s], k_ref[s], v_ref[s], g_ref[s]
        outs = []
        for h in range(N_HEADS_B):
            ks = slice(h * dk, (h + 1) * dk)
            vs = slice(h * dv, (h + 1) * dv)
            a_col = _col_from_row(a[:, ks])
            k_col = _col_from_row(k[:, ks])
            q_col = _col_from_row(q[:, ks] * (dk ** -0.5))
            s_new = a_col * s0_ref[s, h] + k_col * v[:, vs]
            s_ref[s, h] = s_new
            o = jnp.sum(q_col * s_new, axis=0, keepdims=True)
            outs.append(_gla_out(o, gn, g[:, vs]))
        o_ref[s] = jnp.concatenate(outs, axis=1)
        return carry

    lax.fori_loop(0, q_ref.shape[0], one_sequence, 0)


def _gla_decode(proj_s, alr_s, wa2, ba, gn, state, cols):
    t, n = proj_s.shape
    dk, dv = LANES, 2 * LANES
    h = N_HEADS_B
    p3 = proj_s.reshape(t, 1, n)
    a3 = alr_s.reshape(t, 1, LANES)
    sb = _largest_tile(t, SUBLANES, 1)
    blk = lambda w, off: pl.BlockSpec((sb, 1, w), lambda i: (i, 0, off // w))
    full = lambda a: pl.BlockSpec(a.shape, lambda i: (0,) * a.ndim)
    st = pl.BlockSpec((sb, h, dk, dv), lambda i: (i, 0, 0, 0))
    o, s_new = pl.pallas_call(
        _gla_decode_kernel,
        grid=(t // sb,),
        in_specs=[blk(h * dk, cols["q_b"]), blk(h * dk, cols["k_b"]), blk(h * dv, cols["v_b"]),
                  blk(h * dv, cols["g_b"]), blk(LANES, 0), full(wa2), full(ba), full(gn), st],
        out_specs=[pl.BlockSpec((sb, 1, h * dv), lambda i: (i, 0, 0)), st],
        out_shape=[jax.ShapeDtypeStruct((t, 1, h * dv), F32), jax.ShapeDtypeStruct(state.shape, F32)],
        compiler_params=_params(("parallel",)),
        name="gla_decode",
    )(p3, p3, p3, p3, a3, wa2, ba, gn, state)
    return o.reshape(t, h * dv), s_new


def _merge_kernel(oa_ref, ob_ref, ga_ref, gb_ref, x_ref, wa_ref, wb_ref, wo_ref, n2_ref, rwh_ref, rwl_ref, rb_ref,
                  x1_ref, h2_ref, lg_ref):
    ya = _dot(oa_ref[...].astype(BF16), wa_ref[...])
    yb = _dot(ob_ref[...].astype(BF16), wb_ref[...])
    merged = jax.nn.sigmoid(ga_ref[...]) * ya + jax.nn.sigmoid(gb_ref[...]) * yb
    x1 = x_ref[...] + _dot(merged.astype(BF16), wo_ref[...])
    x1_ref[...] = x1
    h2 = _rms(x1) * n2_ref[...]
    hh, hl = _split2(h2)
    lg_ref[...] = _dot(hh, rwh_ref[...]) + _dot(hl, rwh_ref[...]) + _dot(hh, rwl_ref[...]) + rb_ref[...]
    h2_ref[...] = _pack_bf16_pairs(hh)


def _merge(o_a, o_b, proj, x, wa, wb, wo, n2, rwh, rwl, rb, cols, tm):
    t, d = x.shape
    row = lambda w, off: pl.BlockSpec((tm, w), lambda i: (i, off // w))
    full = lambda a: pl.BlockSpec(a.shape, lambda i: (0,) * a.ndim)
    return pl.pallas_call(
        _merge_kernel,
        grid=(t // tm,),
        in_specs=[row(o_a.shape[1], 0), row(o_b.shape[1], 0), row(d, cols["gate_a"]), row(d, cols["gate_b"]),
                  row(d, 0), full(wa), full(wb), full(wo), full(n2), full(rwh), full(rwl), full(rb)],
        out_specs=[row(d, 0), row(d // 2, 0), row(LANES, 0)],
        out_shape=[jax.ShapeDtypeStruct((t, d), F32), jax.ShapeDtypeStruct((t, d // 2), jnp.uint32),
                   jax.ShapeDtypeStruct((t, LANES), F32)],
        compiler_params=_params(("parallel",)),
        name="merge_out_proj",
    )(o_a, o_b, proj, proj, x, wa, wb, wo, n2, rwh, rwl, rb)


def _router_kernel(lg_ref, idx_ref, w_ref, rank_ref, cnt_ref, carry_s):
    @pl.when(pl.program_id(0) == 0)
    def _():
        carry_s[...] = jnp.zeros(carry_s.shape, F32)

    l = lg_ref[...]
    tm = l.shape[0]
    lane = lax.broadcasted_iota(jnp.int32, l.shape, 1).astype(F32)
    vals, idxs = [], []
    for _ in range(MOE_TOPK):
        mx = jnp.max(l, axis=-1, keepdims=True)
        ix = jnp.min(jnp.where(l == mx, lane, float(LANES)), axis=-1, keepdims=True)
        vals.append(mx)
        idxs.append(ix)
        l = jnp.where(lane == ix, -jnp.inf, l)
    es = [jnp.exp(v - vals[0]) for v in vals]
    tot = es[0]
    for e in es[1:]:
        tot = tot + e
    onehot = jnp.zeros(l.shape, F32)
    for ix in idxs:
        onehot = onehot + jnp.where(lane == ix, 1.0, 0.0)
    r = lax.broadcasted_iota(jnp.int32, (tm, tm), 0)
    c = lax.broadcasted_iota(jnp.int32, (tm, tm), 1)
    before = _dot((c < r).astype(BF16), onehot.astype(BF16)) + carry_s[...]
    idx_o = jnp.zeros(l.shape, F32)
    w_o = jnp.zeros(l.shape, F32)
    rank_o = jnp.zeros(l.shape, F32)
    for k in range(MOE_TOPK):
        rk = jnp.sum(jnp.where(lane == idxs[k], before, 0.0), axis=-1, keepdims=True)
        idx_o = jnp.where(lane == k, idxs[k], idx_o)
        w_o = jnp.where(lane == k, es[k] / tot, w_o)
        rank_o = jnp.where(lane == k, rk, rank_o)
    idx_ref[...] = idx_o.astype(jnp.int32)
    w_ref[...] = w_o
    rank_ref[...] = rank_o.astype(jnp.int32)
    carry_s[...] = carry_s[...] + jnp.sum(onehot, axis=0, keepdims=True)
    cnt_ref[...] = carry_s[...].astype(jnp.int32)


def _router(logits, tm):
    t = logits.shape[0]
    row = pl.BlockSpec((tm, LANES), lambda i: (i, 0))
    return pl.pallas_call(
        _router_kernel,
        grid=(t // tm,),
        in_specs=[row],
        out_specs=[row, row, row, pl.BlockSpec((1, LANES), lambda i: (0, 0))],
        out_shape=[jax.ShapeDtypeStruct((t, LANES), jnp.int32), jax.ShapeDtypeStruct((t, LANES), F32),
                   jax.ShapeDtypeStruct((t, LANES), jnp.int32), jax.ShapeDtypeStruct((1, LANES), jnp.int32)],
        scratch_shapes=[pltpu.VMEM((1, LANES), F32)],
        compiler_params=_params(("arbitrary",)),
        name="router",
    )(logits)


def _scatter_rows(dest_ref, x_ref, xs_hbm, sem):
    groups = x_ref.shape[0]
    per_group = SUBLANES * MOE_TOPK
    base = pl.program_id(0) * groups * per_group

    def issue(g, carry):
        for s in range(SUBLANES):
            for k in range(MOE_TOPK):
                dst = dest_ref[base + g * per_group + s * MOE_TOPK + k]
                pltpu.make_async_copy(x_ref.at[g, pl.ds(s, 1), :], xs_hbm.at[pl.ds(dst, 1), :], sem).start()
        return carry

    lax.fori_loop(0, groups, issue, 0)
    for _ in range(per_group):
        pltpu.make_async_copy(x_ref.at[:, 0, :], xs_hbm.at[pl.ds(0, groups), :], sem).wait()


def _dispatch_kernel(dest_ref, x_ref, xs_in_hbm, xs_hbm, sem):
    del xs_in_hbm
    _scatter_rows(dest_ref, x_ref, xs_hbm, sem)


def _dispatch(dest, x_packed, xs, tm):
    t, w = x_packed.shape
    x3 = x_packed.reshape(t // SUBLANES, SUBLANES, w)
    hbm = pl.BlockSpec(memory_space=pl.ANY)
    return pl.pallas_call(
        _dispatch_kernel,
        grid_spec=pltpu.PrefetchScalarGridSpec(
            num_scalar_prefetch=1,
            grid=(t // tm,),
            in_specs=[pl.BlockSpec((tm // SUBLANES, SUBLANES, w), lambda i, dst: (i, 0, 0)), hbm],
            out_specs=hbm,
            scratch_shapes=[pltpu.SemaphoreType.DMA(())]),
        out_shape=jax.ShapeDtypeStruct(xs.shape, xs.dtype),
        input_output_aliases={2: 0},
        compiler_params=_params(("arbitrary",)),
        name="moe_dispatch",
    )(dest, x3, xs)


def _stream_expert_weights(first_ref, nxt_ref, be_ref, w_hbms, stages, casts, sems, tn):
    j, i = pl.program_id(0), pl.program_id(1)
    nj = pl.num_programs(0)

    def copies(e, jj):
        c0 = pl.multiple_of(jj * tn, tn)
        return [pltpu.make_async_copy(w.at[e, :, pl.ds(c0, tn)], st, sems.at[n])
                for n, (w, st) in enumerate(zip(w_hbms, stages))]

    @pl.when(first_ref[i] == 1)
    def _():
        @pl.when((j == 0) & (i == 0))
        def _():
            for c in copies(be_ref[0], 0):
                c.start()

        for c in copies(be_ref[i], j):
            c.wait()
        for st, cs in zip(stages, casts):
            cs[...] = st[...].astype(BF16)
        nxt = nxt_ref[i]

        @pl.when(nxt >= 0)
        def _():
            for c in copies(nxt, j):
                c.start()

        @pl.when((nxt < 0) & (j + 1 < nj))
        def _():
            for c in copies(be_ref[0], j + 1):
                c.start()


def _expert_up_kernel(be_ref, first_ref, nxt_ref, nu_ref, x_ref, wg_hbm, wu_hbm, bg_ref, bu_ref, o_ref,
                      stg_g, stg_u, wg_s, wu_s, sems):
    _stream_expert_weights(first_ref, nxt_ref, be_ref, (wg_hbm, wu_hbm), (stg_g, stg_u), (wg_s, wu_s), sems,
                           o_ref.shape[1])
    used = pl.program_id(1) < nu_ref[0]

    @pl.when(used)
    def _():
        x = _unpack_bf16_pairs(x_ref[...])
        half = o_ref.shape[1] // 2
        for cs in (slice(0, half), slice(half, 2 * half)):
            gate = jnp.minimum(_dot(x, wg_s[:, cs]) + bg_ref[:, cs], SWIGLU_LIMIT)
            up = jnp.clip(_dot(x, wu_s[:, cs]) + bu_ref[:, cs], -SWIGLU_LIMIT, SWIGLU_LIMIT)
            o_ref[:, cs] = ((up + 1.0) * (gate * jax.nn.sigmoid(SWIGLU_ALPHA * gate))).astype(BF16)

    @pl.when(jnp.logical_not(used))
    def _():
        o_ref[...] = jnp.zeros(o_ref.shape, o_ref.dtype)


def _expert_down_kernel(be_ref, first_ref, nxt_ref, nu_ref, h_ref, wd_hbm, bd_ref, o_ref, stg, wd_s, sems):
    _stream_expert_weights(first_ref, nxt_ref, be_ref, (wd_hbm,), (stg,), (wd_s,), sems, wd_s.shape[1])
    used = pl.program_id(1) < nu_ref[0]

    @pl.when(used)
    def _():
        o_ref[...] = _pack_bf16_pairs((_dot(h_ref[...], wd_s[...]) + bd_ref[...]).astype(BF16))

    @pl.when(jnp.logical_not(used))
    def _():
        o_ref[...] = jnp.zeros(o_ref.shape, o_ref.dtype)


def _used_block(i, nu):
    return jnp.minimum(i, nu[0] - 1)


def _expert_up(meta, xs, wg, wu, bg, bu, bm, tf):
    p, half = xs.shape
    d, f = wg.shape[1], wg.shape[2]
    bspec = pl.BlockSpec((None, 1, tf), lambda j, i, be, fi, nx, nu: (be[i], 0, j))
    hbm = pl.BlockSpec(memory_space=pl.ANY)
    return pl.pallas_call(
        _expert_up_kernel,
        grid_spec=pltpu.PrefetchScalarGridSpec(
            num_scalar_prefetch=4,
            grid=(f // tf, p // bm),
            in_specs=[pl.BlockSpec((bm, half), lambda j, i, be, fi, nx, nu: (_used_block(i, nu), 0)),
                      hbm, hbm, bspec, bspec],
            out_specs=pl.BlockSpec((bm, tf), lambda j, i, be, fi, nx, nu: (i, j)),
            scratch_shapes=[pltpu.VMEM((d, tf), F32), pltpu.VMEM((d, tf), F32),
                            pltpu.VMEM((d, tf), BF16), pltpu.VMEM((d, tf), BF16),
                            pltpu.SemaphoreType.DMA((2,))]),
        out_shape=jax.ShapeDtypeStruct((p, f), BF16),
        compiler_params=_params(("arbitrary", "arbitrary")),
        name="moe_up",
    )(*meta, xs, wg, wu, bg, bu)


def _expert_down(meta, hid, wd, bd, bm):
    p, f = hid.shape
    d = wd.shape[2]
    return pl.pallas_call(
        _expert_down_kernel,
        grid_spec=pltpu.PrefetchScalarGridSpec(
            num_scalar_prefetch=4,
            grid=(1, p // bm),
            in_specs=[pl.BlockSpec((bm, f), lambda j, i, be, fi, nx, nu: (_used_block(i, nu), 0)),
                      pl.BlockSpec(memory_space=pl.ANY),
                      pl.BlockSpec((None, 1, d), lambda j, i, be, fi, nx, nu: (be[i], 0, 0))],
            out_specs=pl.BlockSpec((bm, d // 2), lambda j, i, be, fi, nx, nu: (i, 0)),
            scratch_shapes=[pltpu.VMEM((f, d), F32), pltpu.VMEM((f, d), BF16), pltpu.SemaphoreType.DMA((1,))]),
        out_shape=jax.ShapeDtypeStruct((p, d // 2), jnp.uint32),
        compiler_params=_params(("arbitrary", "arbitrary")),
        name="moe_down",
    )(*meta, hid, wd, bd)


def _combine_kernel(dest_ref, ys_hbm, x1_ref, w_ref, p_ref, pw_ref, pg_ref, gw_ref, y_ref, buf, sems):
    tm = x1_ref.shape[0]
    groups = tm // SUBLANES
    per_group = SUBLANES * MOE_TOPK
    i = pl.program_id(0)
    slot = i % 2

    def issue_group(tile, sl, g):
        base = tile * tm * MOE_TOPK
        for s in range(SUBLANES):
            for k in range(MOE_TOPK):
                dst = dest_ref[base + g * per_group + s * MOE_TOPK + k]
                pltpu.make_async_copy(ys_hbm.at[pl.ds(dst, 1), :], buf.at[sl, k, g, pl.ds(s, 1), :],
                                      sems.at[sl]).start()

    def wait_slot(sl):
        for k in range(MOE_TOPK):
            for s in range(SUBLANES):
                pltpu.make_async_copy(ys_hbm.at[pl.ds(0, groups), :], buf.at[sl, k, :, s, :], sems.at[sl]).wait()

    last = pl.num_programs(0) - 1

    @pl.when(i == 0)
    def _():
        lax.fori_loop(0, groups, lambda g, c: (issue_group(0, 0, g), c)[1], 0)

    nxt_tile = jnp.minimum(i + 1, last)
    d = y_ref.shape[1]
    n_col = d // (2 * LANES)
    tc = d // n_col
    n_phase = 2 + MOE_TOPK + n_col
    issued = [0]

    def issue_share(phase):
        upto = groups * (phase + 1) // n_phase
        for g in range(issued[0], upto):
            issue_group(nxt_tile, 1 - slot, g)
        issued[0] = upto

    issue_share(0)
    e = _rms(_dot(p_ref[...].astype(BF16), pw_ref[...])) * pg_ref[...]
    issue_share(1)
    wait_slot(slot)
    w = w_ref[...]
    x2 = x1_ref[...]
    for k in range(MOE_TOPK):
        u = buf[slot, k].reshape(tm, buf.shape[-1])
        rows = jnp.concatenate([lax.bitcast_convert_type(u << 16, F32),
                                lax.bitcast_convert_type(u & jnp.uint32(0xFFFF0000), F32)], axis=1)
        x2 = x2 + w[:, k:k + 1] * rows
        issue_share(2 + k)
    xn = _rms(x2).astype(BF16)
    for t in range(n_col):
        cs = slice(t * tc, (t + 1) * tc)
        gate = jax.nn.sigmoid(_dot(xn, gw_ref[:, cs]))
        y_ref[:, cs] = x2[:, cs] + gate * e[:, cs]
        issue_share(2 + MOE_TOPK + t)

    @pl.when(i == last)
    def _():
        wait_slot(1 - slot)


def _combine(dest, ys, x1, w, p, pw, pg, gw, tm):
    t, d = x1.shape
    full = lambda a: pl.BlockSpec(a.shape, lambda i, dst: (0,) * a.ndim)
    row = lambda width: pl.BlockSpec((tm, width), lambda i, dst: (i, 0))
    return pl.pallas_call(
        _combine_kernel,
        grid_spec=pltpu.PrefetchScalarGridSpec(
            num_scalar_prefetch=1,
            grid=(t // tm,),
            in_specs=[pl.BlockSpec(memory_space=pl.ANY), row(d), row(LANES), row(p.shape[1]),
                      full(pw), full(pg), full(gw)],
            out_specs=row(d),
            scratch_shapes=[pltpu.VMEM((2, MOE_TOPK, tm // SUBLANES, SUBLANES, d // 2), jnp.uint32),
                            pltpu.SemaphoreType.DMA((2,))]),
        out_shape=jax.ShapeDtypeStruct((t, d), F32),
        compiler_params=_params(("arbitrary",)),
        name="moe_combine_ple",
    )(dest, ys, x1, w, p, pw, pg, gw)


def _largest_tile(n, cap, quantum):
    best = None
    for t in range(quantum, cap + 1, quantum):
        if n % t == 0:
            best = t
    assert best is not None, (n, cap, quantum)
    return best


def kernel(x_prompt, x_sample, cache_k, cache_v, state_gla, page_table, p_prompt, p_sample, norm1_g, w_in, q_norm_g, k_norm_g, gla_w_a2, gla_b_a, gla_norm_g, w_branch_a, w_branch_b, w_out, norm2_g, router_w, router_b, moe_w_gate, moe_b_gate, moe_w_up, moe_b_up, moe_w_down, moe_b_down, ple_w, ple_norm_g, ple_gate_w):
    assert norm1_g.shape[0] == 1, "one layer"
    b, s, d = x_prompt.shape
    nseq = x_sample.shape[0]
    assert x_sample.shape[1] == 1
    n_heads = d // (2 * HEAD_DIM_A)
    wa = n_heads * HEAD_DIM_A
    kb_w = N_HEADS_B * LANES
    vb_w = 2 * kb_w
    n_exp = router_w.shape[2]
    tp, ts = b * s, nseq

    sizes = [("q_a", wa), ("k_a", wa), ("v_a", wa), ("q_b", kb_w), ("k_b", kb_w), ("v_b", vb_w), ("g_b", vb_w),
             ("a_lr", GLA_LOWRANK), ("gate_a", d), ("gate_b", d)]
    cols, src, off = {}, 0, 0
    w_in0 = w_in[0]
    for name, width in sizes:
        if name == "a_lr":
            w_side = jnp.pad(w_in0[:, src:src + width], ((0, 0), (0, LANES - width))).astype(BF16)
            split = src
        else:
            cols[name] = off
            off += width
        src += width
    n_main = off
    tn = _largest_tile(n_main, 1024, 2 * LANES)
    w_main = jnp.concatenate([w_in0[:, :split], w_in0[:, split + GLA_LOWRANK:]], axis=1).astype(BF16)
    w_tiles = w_main.reshape(d, n_main // tn, tn).transpose(1, 0, 2)

    g1 = norm1_g[0][None, :]
    qg, kg = q_norm_g[0][None, :], k_norm_g[0][None, :]
    wa2 = jnp.pad(gla_w_a2[0], ((0, LANES - GLA_LOWRANK), (0, 0)))
    ba = gla_b_a[0][None, :]
    gn = gla_norm_g[0][None, :]
    wba, wbb, wo = w_branch_a[0].astype(BF16), w_branch_b[0].astype(BF16), w_out[0].astype(BF16)
    n2 = norm2_g[0][None, :]
    rw = jnp.pad(router_w[0], ((0, 0), (0, LANES - n_exp)))
    rwh = rw.astype(BF16)
    rwl = (rw - rwh.astype(F32)).astype(BF16)
    rb = jnp.concatenate([router_b[0], jnp.full((LANES - n_exp,), -jnp.inf, F32)])[None, :]

    xp = x_prompt.reshape(tp, d)
    proj_p, alr_p = _norm_matmul(xp, g1, w_tiles, w_side)
    tr = _largest_tile(s, 512, 8)
    tabs_p = _rope_tables(jnp.arange(s, dtype=jnp.int32))
    q_p, k_p, v_p = _qk_rope(proj_p, qg, kg, tabs_p, n_heads, tr, lambda i: i % (s // tr))
    o_a_p = _moba_prompt(q_p.reshape(b, s, wa), k_p.reshape(b, s, wa), v_p.reshape(b, s, wa), n_heads)
    o_b_p, gla_p = _gla_prompt(proj_p.reshape(b, s, n_main), alr_p.reshape(b, s, LANES), wa2, ba, gn, cols)
    x1_p, h2_p, lg_p = _merge(o_a_p.reshape(tp, wa), o_b_p.reshape(tp, vb_w), proj_p, xp, wba, wbb, wo, n2,
                              rwh, rwl, rb, cols, _largest_tile(tp, 128, 8))

    xs_ = x_sample.reshape(ts, d)
    past = page_table.shape[1] * cache_k.shape[2]
    proj_s, alr_s = _norm_matmul(xs_, g1, w_tiles, w_side)
    tabs_s = tuple(jnp.broadcast_to(t_, (ts, HEAD_DIM_A)) for t_ in _rope_tables(jnp.full((1,), past, jnp.int32)))
    q_s, k_s, v_s = _qk_rope(proj_s, qg, kg, tabs_s, n_heads, ts, lambda i: i)
    o_a_s = _moba_decode(q_s, k_s, v_s, cache_k[0], cache_v[0], page_table, n_heads)
    o_b_s, gla_s = _gla_decode(proj_s, alr_s, wa2, ba, gn, state_gla[0], cols)
    x1_s, h2_s, lg_s = _merge(o_a_s, o_b_s, proj_s, xs_, wba, wbb, wo, n2, rwh, rwl, rb, cols, ts)

    t_all = tp + ts
    logits = jnp.concatenate([lg_p, lg_s], axis=0)
    idx_o, w_o, rank_o, cnt = _router(logits, _largest_tile(t_all, 512, LANES))
    bm = MOE_ROW_BLOCK
    n_assign = t_all * MOE_TOPK
    n_blocks = -(-n_assign // bm) + n_exp
    counts = cnt[0, :n_exp]
    pcounts = (counts + bm - 1) // bm * bm
    pend = jnp.cumsum(pcounts)
    pstart = pend - pcounts
    expert_ids = jnp.arange(n_exp, dtype=jnp.int32)
    start_of = jnp.sum(jnp.where(idx_o[:, :MOE_TOPK, None] == expert_ids, pstart, 0), axis=-1)
    dest = (start_of + rank_o[:, :MOE_TOPK]).reshape(-1)
    n_used = pend[n_exp - 1] // bm
    blk = jnp.arange(n_blocks, dtype=jnp.int32)
    blk = jnp.minimum(blk, n_used - 1)
    block_e = jnp.sum(pend[None, :] <= (blk * bm)[:, None], axis=1).astype(jnp.int32)
    first = jnp.concatenate([jnp.ones((1,), jnp.int32), (block_e[1:] != block_e[:-1]).astype(jnp.int32)])
    nxt_pos = jnp.sum(block_e[None, :] <= block_e[:, None], axis=1)
    nxt = jnp.where(nxt_pos < n_blocks, block_e[jnp.minimum(nxt_pos, n_blocks - 1)], -1).astype(jnp.int32)
    meta = (block_e, first, nxt, n_used.reshape(1).astype(jnp.int32))

    xs_rows = jnp.zeros((n_blocks * bm, d // 2), jnp.uint32)
    xs_rows = _dispatch(dest[:tp * MOE_TOPK], h2_p, xs_rows, _largest_tile(tp, 512, 8))
    xs_rows = _dispatch(dest[tp * MOE_TOPK:], h2_s, xs_rows, ts)
    f = moe_w_gate.shape[3]
    hid = _expert_up(meta, xs_rows, moe_w_gate[0], moe_w_up[0], moe_b_gate[0][:, None, :],
                     moe_b_up[0][:, None, :], bm, _largest_tile(f, 1024, 2 * LANES))
    ys = _expert_down(meta, hid, moe_w_down[0], moe_b_down[0][:, None, :], bm)

    pw, pg, gw = ple_w[0].astype(BF16), ple_norm_g[0][None, :], ple_gate_w[0].astype(BF16)
    tc = _largest_tile(ts, 128, 8)
    assert tp % tc == 0
    y_p = _combine(dest[:tp * MOE_TOPK], ys, x1_p, w_o[:tp], p_prompt[0].reshape(tp, -1), pw, pg, gw, tc)
    y_s = _combine(dest[tp * MOE_TOPK:], ys, x1_s, w_o[tp:], p_sample[0].reshape(ts, -1), pw, pg, gw, tc)

    hd = (n_heads, HEAD_DIM_A)
    return (y_p.reshape(b, s, d), y_s.reshape(nseq, 1, d),
            k_p.reshape(1, b, s, *hd), v_p.reshape(1, b, s, *hd), gla_p[None],
            k_s.reshape(1, nseq, 1, *hd), v_s.reshape(1, nseq, 1, *hd), gla_s[None])
```

```python
import functools

import jax
import jax.numpy as jnp
from jax import lax
from jax.experimental import pallas as pl
from jax.experimental.pallas import tpu as pltpu

HEAD_DIM_A = 128
MOBA_BLOCK = 256
MOBA_TOPK = 3
ROT_DIM = HEAD_DIM_A // 4
ROPE_THETA = 500000.0
N_HEADS_B = 4
GLA_LOWRANK = 16
GLA_TAU = 16.0
GLA_CHUNK = 64
MOE_TOPK = 4
SWIGLU_LIMIT = 7.0
SWIGLU_ALPHA = 1.702
EPS = 1e-6
NEG_INF = -1e30

LANES = 128
SUBLANES = 8
MOE_ROW_BLOCK = 256
VMEM_LIMIT = 56 * 1024 * 1024

F32 = jnp.float32
BF16 = jnp.bfloat16


def _params(sem):
    return pltpu.CompilerParams(dimension_semantics=sem, vmem_limit_bytes=VMEM_LIMIT)


def _dot(a, b):
    return jnp.dot(a, b, preferred_element_type=F32)


def _dot_nt(a, b):
    return lax.dot_general(a, b, (((1,), (1,)), ((), ())), preferred_element_type=F32)


def _dot_tn(a, b):
    return lax.dot_general(a, b, (((0,), (0,)), ((), ())), preferred_element_type=F32)


def _split2(x):
    hi = x.astype(BF16)
    lo = (x - hi.astype(F32)).astype(BF16)
    return hi, lo


def _rms(x):
    return x * lax.rsqrt(jnp.mean(x * x, axis=-1, keepdims=True) + EPS)


def _col_from_row(row):
    n = row.shape[1]
    r = lax.broadcasted_iota(jnp.int32, (n, n), 0)
    c = lax.broadcasted_iota(jnp.int32, (n, n), 1)
    return jnp.sum(jnp.where(r == c, jnp.broadcast_to(row, (n, n)), 0.0), axis=1, keepdims=True)


def _log_sigmoid(z):
    return jnp.minimum(z, 0.0) - jnp.log(1.0 + jnp.exp(-jnp.abs(z)))


def _pack_bf16_pairs(x16):
    half = x16.shape[1] // 2
    bits = lax.bitcast_convert_type(x16.astype(F32), jnp.uint32)
    return (bits[:, :half] >> 16) | bits[:, half:]


def _unpack_bf16_pairs(u):
    lo = lax.bitcast_convert_type(u << 16, F32).astype(BF16)
    hi = lax.bitcast_convert_type(u & jnp.uint32(0xFFFF0000), F32).astype(BF16)
    return jnp.concatenate([lo, hi], axis=1)


def _norm_matmul_kernel(x_ref, g_ref, w_ref, ws_ref, o_ref, os_ref, h_ref):
    @pl.when(pl.program_id(1) == 0)
    def _():
        h_ref[...] = (_rms(x_ref[...]) * g_ref[...]).astype(BF16)
        os_ref[...] = _dot(h_ref[...], ws_ref[...])

    o_ref[...] = _dot(h_ref[...], w_ref[...])


def _norm_matmul(x, g, w, w_side, tn):
    m, d = x.shape
    n = w.shape[1]
    ns = w_side.shape[1]
    tm = _largest_tile(m, 1024, SUBLANES)
    return pl.pallas_call(
        _norm_matmul_kernel,
        grid=(m // tm, n // tn),
        in_specs=[pl.BlockSpec((tm, d), lambda i, j: (i, 0)),
                  pl.BlockSpec((1, d), lambda i, j: (0, 0)),
                  pl.BlockSpec((d, tn), lambda i, j: (0, j)),
                  pl.BlockSpec((d, ns), lambda i, j: (0, 0))],
        out_specs=[pl.BlockSpec((tm, tn), lambda i, j: (i, j)), pl.BlockSpec((tm, ns), lambda i, j: (i, 0))],
        out_shape=[jax.ShapeDtypeStruct((m, n), F32), jax.ShapeDtypeStruct((m, ns), F32)],
        scratch_shapes=[pltpu.VMEM((tm, d), BF16)],
        compiler_params=_params(("parallel", "arbitrary")),
        name="in_proj",
    )(x, g, w, w_side)


def _qk_rope_kernel(q_ref, k_ref, v_ref, qg_ref, kg_ref, c_ref, sn_ref, sp_ref, qo_ref, ko_ref, vo_ref):
    c, sn, sp = c_ref[...], sn_ref[...], sp_ref[...]
    half = ROT_DIM // 2

    def norm_rope(x, g):
        y = _rms(x) * g
        up = pltpu.roll(y, LANES - half, 1)
        dn = pltpu.roll(y, half, 1)
        return y * c + up * sn + dn * sp

    qg, kg = qg_ref[...], kg_ref[...]
    for h in range(q_ref.shape[1] // HEAD_DIM_A):
        hs = slice(h * HEAD_DIM_A, (h + 1) * HEAD_DIM_A)
        qo_ref[:, hs] = norm_rope(q_ref[:, hs], qg)
        ko_ref[:, hs] = norm_rope(k_ref[:, hs], kg)
    vo_ref[...] = v_ref[...]


def _qk_rope(proj, qg, kg, tabs, n_heads, tr, tab_index):
    t = proj.shape[0]
    w = n_heads * HEAD_DIM_A
    col = lambda j: pl.BlockSpec((tr, w), lambda i: (i, j))
    vec = pl.BlockSpec((1, HEAD_DIM_A), lambda i: (0, 0))
    tab = pl.BlockSpec((tr, HEAD_DIM_A), lambda i: (tab_index(i), 0))
    return pl.pallas_call(
        _qk_rope_kernel,
        grid=(t // tr,),
        in_specs=[col(0), col(1), col(2), vec, vec, tab, tab, tab],
        out_specs=[col(0), col(0), col(0)],
        out_shape=[jax.ShapeDtypeStruct((t, w), F32)] * 3,
        compiler_params=_params(("parallel",)),
        name="qk_rope",
    )(proj, proj, proj, qg, kg, *tabs)


def _rope_tables(pos):
    half = ROT_DIM // 2
    inv = ROPE_THETA ** (-jnp.arange(half, dtype=F32) * 2.0 / ROT_DIM)
    ang = pos.astype(F32)[:, None] * inv[None, :]
    cos, sin = jnp.cos(ang), jnp.sin(ang)
    n = pos.shape[0]
    rest = HEAD_DIM_A - ROT_DIM
    c = jnp.concatenate([cos, cos, jnp.ones((n, rest), F32)], axis=1)
    sn = jnp.concatenate([-sin, jnp.zeros((n, half + rest), F32)], axis=1)
    sp = jnp.concatenate([jnp.zeros((n, half), F32), sin, jnp.zeros((n, rest), F32)], axis=1)
    return c, sn, sp


def _block_select(gate, n_cand, n_sel, max_cand):
    lane = lax.broadcasted_iota(jnp.int32, gate.shape, 1)
    gate = jnp.where(lane < n_cand, gate, -jnp.inf)
    rank = jnp.zeros(gate.shape, F32)
    for m in range(max_cand):
        gm = jnp.broadcast_to(gate[:, m:m + 1], gate.shape)
        beats = (gm > gate) | ((gm == gate) & (lane > m))
        rank = rank + jnp.where(beats, 1.0, 0.0)
    return jnp.where((rank < n_sel) & (lane < n_cand), 1.0, 0.0)


def _block_select_rows(gate_t, n_cand, n_sel):
    blk_id = lax.broadcasted_iota(jnp.int32, gate_t.shape, 0)
    gate_t = jnp.where(blk_id < n_cand, gate_t, -jnp.inf)
    rank = jnp.zeros(gate_t.shape, F32)
    for m in range(n_cand):
        gm = jnp.broadcast_to(gate_t[m:m + 1, :], gate_t.shape)
        beats = (gm > gate_t) | ((gm == gate_t) & (blk_id > m))
        rank = rank + jnp.where(beats, 1.0, 0.0)
    return jnp.where((rank < n_sel) & (blk_id < n_cand), 1.0, 0.0)


def _moba_prompt_kernel(q_ref, k_ref, v_ref, o_ref, kaug_s, v16_s, *, nb, n_sel):
    blk, d = MOBA_BLOCK, HEAD_DIM_A
    s_len = nb * blk
    kaug_s[:, :d] = k_ref[...].astype(BF16)
    key_blk = lax.broadcasted_iota(jnp.int32, (s_len, LANES), 0) // blk
    kaug_s[:, d:] = (key_blk == lax.broadcasted_iota(jnp.int32, (s_len, LANES), 1)).astype(BF16)
    v16_s[...] = v_ref[...].astype(BF16)
    nb_pad = -(-nb // 16) * 16
    means = [jnp.mean(k_ref[n * blk:(n + 1) * blk, :], axis=0, keepdims=True) for n in range(nb)]
    kmean = jnp.concatenate(means + [jnp.zeros((nb_pad - nb, d), F32)], axis=0)
    kh, kl = _split2(kmean)
    blk_id = lax.broadcasted_iota(jnp.int32, (nb_pad, blk), 0)
    to_lanes = (lax.broadcasted_iota(jnp.int32, (nb_pad, LANES), 0)
                == lax.broadcasted_iota(jnp.int32, (nb_pad, LANES), 1)).astype(BF16)
    row = lax.broadcasted_iota(jnp.int32, (blk, blk), 0)
    col = lax.broadcasted_iota(jnp.int32, (blk, blk), 1)
    scale = d ** -0.5
    for qb in range(nb):
        q = q_ref[qb * blk:(qb + 1) * blk, :]
        keep_t = jnp.where(blk_id == qb, 1.0, 0.0)
        if qb > 0:
            qh, ql = _split2(q)
            gate_t = _dot_nt(kh, qh) + _dot_nt(kl, qh) + _dot_nt(kh, ql)
            keep_t = keep_t + _block_select_rows(gate_t, qb, n_sel)
        keep = _dot_tn(keep_t.astype(BF16), to_lanes)
        bias = ((1.0 - keep) * NEG_INF).astype(BF16)
        nk = (qb + 1) * blk
        s = _dot_nt(jnp.concatenate([(q * scale).astype(BF16), bias], axis=1), kaug_s[:nk, :])
        s_own = jnp.where(col <= row, s[:, qb * blk:], NEG_INF)
        s = s_own if qb == 0 else jnp.concatenate([s[:, :qb * blk], s_own], axis=1)
        m = jnp.max(s, axis=-1, keepdims=True)
        p = jnp.exp(s - m)
        l = jnp.sum(p, axis=-1, keepdims=True)
        o_ref[qb * blk:(qb + 1) * blk, :] = _dot(p.astype(BF16), v16_s[:nk, :]) / l


def _moba_prompt(q, k, v, n_heads):
    b, s, _ = q.shape
    blk = MOBA_BLOCK
    assert s % blk == 0 and s // blk <= LANES
    nb = s // blk
    n_sel = max(1, min(MOBA_TOPK, nb - 1))
    d = HEAD_DIM_A
    spec = pl.BlockSpec((None, s, d), lambda bi, h: (bi, 0, h))
    return pl.pallas_call(
        functools.partial(_moba_prompt_kernel, nb=nb, n_sel=n_sel),
        grid=(b, n_heads),
        in_specs=[spec, spec, spec],
        out_specs=spec,
        out_shape=jax.ShapeDtypeStruct(q.shape, F32),
        scratch_shapes=[pltpu.VMEM((s, d + LANES), BF16), pltpu.VMEM((s, d), BF16)],
        compiler_params=_params(("parallel", "parallel")),
        name="moba_prompt",
    )(q, k, v)


def _moba_decode_kernel(pt_ref, q_ref, kn_ref, vn_ref, *refs, n_pages, page, n_sel):
    k_refs = refs[:n_pages]
    v_refs = refs[n_pages:2 * n_pages]
    o_ref = refs[2 * n_pages]
    ppb = MOBA_BLOCK // page
    nb = n_pages // ppb
    q = q_ref[...]
    scale = HEAD_DIM_A ** -0.5

    lane = lax.broadcasted_iota(jnp.int32, (q.shape[0], LANES), 1)
    gate = jnp.zeros((q.shape[0], LANES), F32)
    for n in range(nb):
        ksum = jnp.zeros(q.shape, F32)
        for j in range(ppb):
            ksum = ksum + jnp.sum(k_refs[n * ppb + j][...], axis=0)
        g = jnp.sum(q * (ksum * (1.0 / MOBA_BLOCK)), axis=-1, keepdims=True)
        gate = jnp.where(lane == n, g, gate)
    sel = _block_select(gate, nb, n_sel, nb)

    n_h, d = q.shape
    qs = q * scale
    ones = jnp.ones((d, LANES), BF16)
    m = jnp.broadcast_to(jnp.sum(qs * kn_ref[...], axis=-1, keepdims=True), (n_h, LANES))
    l = jnp.ones_like(m)
    acc = vn_ref[...]
    for pg in range(n_pages):
        n = pg // ppb
        keep = jnp.broadcast_to(sel[:, n:n + 1], (n_h, LANES)) > 0.5
        prod = (k_refs[pg][...] * qs[None]).reshape(page * n_h, d).astype(BF16)
        s = _dot(prod, ones).reshape(page, n_h, LANES)
        s = jnp.where(keep[None], s, NEG_INF)
        m_new = jnp.maximum(m, jnp.max(s, axis=0))
        alpha = jnp.exp(m - m_new)
        p = jnp.exp(s - m_new[None])
        l = alpha * l + jnp.sum(p, axis=0)
        acc = alpha * acc + jnp.sum(p * v_refs[pg][...], axis=0)
        m = m_new
    o_ref[...] = acc / l


def _moba_decode(q, k_new, v_new, cache_k, cache_v, page_table, n_heads):
    nseq, n_pages = page_table.shape
    page = cache_k.shape[1]
    d = HEAD_DIM_A
    nb_all = -(-(n_pages * page + 1) // MOBA_BLOCK)
    n_sel = max(1, min(MOBA_TOPK, nb_all - 1))
    assert MOBA_BLOCK % page == 0 and (n_pages * page) % MOBA_BLOCK == 0
    hd = pl.BlockSpec((None, n_heads, d), lambda b, pt: (b, 0, 0))

    def page_spec(pg):
        return pl.BlockSpec((None, page, n_heads, d), lambda b, pt: (pt[b * n_pages + pg], 0, 0, 0))

    specs = [page_spec(pg) for pg in range(n_pages)]
    q3 = q.reshape(nseq, n_heads, d)
    return pl.pallas_call(
        functools.partial(_moba_decode_kernel, n_pages=n_pages, page=page, n_sel=n_sel),
        grid_spec=pltpu.PrefetchScalarGridSpec(
            num_scalar_prefetch=1,
            grid=(nseq,),
            in_specs=[hd, hd, hd] + specs + specs,
            out_specs=hd),
        out_shape=jax.ShapeDtypeStruct((nseq, n_heads, d), F32),
        compiler_params=_params(("parallel",)),
        name="moba_decode",
    )(page_table.reshape(-1), q3, k_new.reshape(nseq, n_heads, d), v_new.reshape(nseq, n_heads, d),
      *([cache_k] * n_pages), *([cache_v] * n_pages)).reshape(nseq, n_heads * d)


GLA_UNROLL = 8


def _gla_out(o, gn, g):
    return _rms(o) * gn * (g * jax.nn.sigmoid(g))


def _gla_prompt_kernel(q_ref, k_ref, v_ref, g_ref, alr_ref, wa2_ref, ba_ref, gn_ref, o_ref, sfin_ref, la_s,
                       *, n_chunks):
    c = GLA_CHUNK
    dk = q_ref.shape[1]
    z = _dot(alr_ref[...].astype(BF16), wa2_ref[...].astype(BF16)) + ba_ref[...]
    la_s[...] = _log_sigmoid(z) * (1.0 / GLA_TAU)
    row = lax.broadcasted_iota(jnp.int32, (c, c), 0)
    col = lax.broadcasted_iota(jnp.int32, (c, c), 1)
    causal = col <= row
    tril = causal.astype(BF16)
    gn = gn_ref[...]

    def body(i, st):
        r0 = pl.multiple_of(i * c, c)
        qc = q_ref[pl.ds(r0, c), :] * (dk ** -0.5)
        kc = k_ref[pl.ds(r0, c), :]
        vc = v_ref[pl.ds(r0, c), :].astype(BF16)
        ac = la_s[pl.ds(r0, c), :]
        a1 = ac.astype(BF16)
        r1 = ac - a1.astype(F32)
        a2 = r1.astype(BF16)
        a3 = (r1 - a2.astype(F32)).astype(BF16)
        b = _dot(tril, a1) + _dot(tril, a2) + _dot(tril, a3)
        q_t = (qc * jnp.exp(b)).astype(BF16)
        k_t = (kc * jnp.exp(-b)).astype(BF16)
        att = jnp.where(causal, _dot_nt(q_t, k_t), 0.0)
        b_last = b[c - 1:c, :]
        k_dec = (kc * jnp.exp(b_last - b)).astype(BF16)
        o_intra = _dot(att.astype(BF16), vc)
        st_in = _dot_tn(k_dec, vc)
        decay = _col_from_row(jnp.exp(b_last))
        o = o_intra + _dot(q_t, st.astype(BF16))
        o_ref[pl.ds(r0, c), :] = _gla_out(o, gn, g_ref[pl.ds(r0, c), :])
        return decay * st + st_in

    sfin_ref[...] = lax.fori_loop(0, n_chunks, body, jnp.zeros(sfin_ref.shape, F32), unroll=GLA_UNROLL)


def _gla_prompt(proj3, alr3, wa2, ba, gn, cols):
    b, s, _ = proj3.shape
    dk, dv = LANES, 2 * LANES
    h = N_HEADS_B
    assert s % GLA_CHUNK == 0
    sk = lambda off: pl.BlockSpec((None, s, dk), lambda bi, hi: (bi, 0, off + hi))
    sv = lambda off: pl.BlockSpec((None, s, dv), lambda bi, hi: (bi, 0, off + hi))
    return pl.pallas_call(
        functools.partial(_gla_prompt_kernel, n_chunks=s // GLA_CHUNK),
        grid=(b, h),
        in_specs=[sk(cols["q_b"] // dk), sk(cols["k_b"] // dk), sv(cols["v_b"] // dv), sv(cols["g_b"] // dv),
                  pl.BlockSpec((None, s, LANES), lambda bi, hi: (bi, 0, 0)),
                  pl.BlockSpec((LANES, dk), lambda bi, hi: (0, hi)),
                  pl.BlockSpec((1, dk), lambda bi, hi: (0, hi)),
                  pl.BlockSpec((1, dv), lambda bi, hi: (0, 0))],
        out_specs=[pl.BlockSpec((None, s, dv), lambda bi, hi: (bi, 0, hi)),
                   pl.BlockSpec((None, None, dk, dv), lambda bi, hi: (bi, hi, 0, 0))],
        out_shape=[jax.ShapeDtypeStruct((b, s, h * dv), F32), jax.ShapeDtypeStruct((b, h, dk, dv), F32)],
        scratch_shapes=[pltpu.VMEM((s, dk), F32)],
        compiler_params=_params(("parallel", "parallel")),
        name="gla_prompt",
    )(proj3, proj3, proj3, proj3, alr3, wa2, ba, gn)


def _gla_decode_kernel(q_ref, k_ref, v_ref, g_ref, alr_ref, wa2_ref, ba_ref, gn_ref, s0_ref, o_ref, s_ref):
    dk, dv = LANES, 2 * LANES
    wa2 = wa2_ref[...].astype(BF16)
    gn = gn_ref[...]

    def one_sequence(s, carry):
        alr = jnp.broadcast_to(alr_ref[s], (SUBLANES, LANES)).astype(BF16)
        z = _dot(alr, wa2)[0:1, :] + ba_ref[...]
        a = jnp.exp(_log_sigmoid(z) * (1.0 / GLA_TAU))
        q, k, v, g = q_ref[s], k_ref[s], v_ref[s], g_ref[s]
        outs = []
        for h in range(N_HEADS_B):
            ks = slice(h * dk, (h + 1) * dk)
            vs = slice(h * dv, (h + 1) * dv)
            a_col = _col_from_row(a[:, ks])
            k_col = _col_from_row(k[:, ks])
            q_col = _col_from_row(q[:, ks] * (dk ** -0.5))
            s_new = a_col * s0_ref[s, h] + k_col * v[:, vs]
            s_ref[s, h] = s_new
            o = jnp.sum(q_col * s_new, axis=0, keepdims=True)
            outs.append(_gla_out(o, gn, g[:, vs]))
        o_ref[s] = jnp.concatenate(outs, axis=1)
        return carry

    lax.fori_loop(0, q_ref.shape[0], one_sequence, 0)


def _gla_decode(proj_s, alr_s, wa2, ba, gn, state, cols):
    t, n = proj_s.shape
    dk, dv = LANES, 2 * LANES
    h = N_HEADS_B
    p3 = proj_s.reshape(t, 1, n)
    a3 = alr_s.reshape(t, 1, LANES)
    sb = _largest_tile(t, SUBLANES, 1)
    blk = lambda w, off: pl.BlockSpec((sb, 1, w), lambda i: (i, 0, off // w))
    full = lambda a: pl.BlockSpec(a.shape, lambda i: (0,) * a.ndim)
    st = pl.BlockSpec((sb, h, dk, dv), lambda i: (i, 0, 0, 0))
    o, s_new = pl.pallas_call(
        _gla_decode_kernel,
        grid=(t // sb,),
        in_specs=[blk(h * dk, cols["q_b"]), blk(h * dk, cols["k_b"]), blk(h * dv, cols["v_b"]),
                  blk(h * dv, cols["g_b"]), blk(LANES, 0), full(wa2), full(ba), full(gn), st],
        out_specs=[pl.BlockSpec((sb, 1, h * dv), lambda i: (i, 0, 0)), st],
        out_shape=[jax.ShapeDtypeStruct((t, 1, h * dv), F32), jax.ShapeDtypeStruct(state.shape, F32)],
        compiler_params=_params(("parallel",)),
        name="gla_decode",
    )(p3, p3, p3, p3, a3, wa2, ba, gn, state)
    return o.reshape(t, h * dv), s_new


def _merge_kernel(oa_ref, ob_ref, ga_ref, gb_ref, x_ref, wa_ref, wb_ref, wo_ref, n2_ref, rwh_ref, rwl_ref, rb_ref,
                  x1_ref, h2_ref, lg_ref):
    ya = _dot(oa_ref[...].astype(BF16), wa_ref[...])
    yb = _dot(ob_ref[...].astype(BF16), wb_ref[...])
    merged = jax.nn.sigmoid(ga_ref[...]) * ya + jax.nn.sigmoid(gb_ref[...]) * yb
    x1 = x_ref[...] + _dot(merged.astype(BF16), wo_ref[...])
    x1_ref[...] = x1
    h2 = _rms(x1) * n2_ref[...]
    hh, hl = _split2(h2)
    lg_ref[...] = _dot(hh, rwh_ref[...]) + _dot(hl, rwh_ref[...]) + _dot(hh, rwl_ref[...]) + rb_ref[...]
    h2_ref[...] = _pack_bf16_pairs(hh)


def _merge(o_a, o_b, proj, x, wa, wb, wo, n2, rwh, rwl, rb, cols, tm):
    t, d = x.shape
    row = lambda w, off: pl.BlockSpec((tm, w), lambda i: (i, off // w))
    full = lambda a: pl.BlockSpec(a.shape, lambda i: (0,) * a.ndim)
    return pl.pallas_call(
        _merge_kernel,
        grid=(t // tm,),
        in_specs=[row(o_a.shape[1], 0), row(o_b.shape[1], 0), row(d, cols["gate_a"]), row(d, cols["gate_b"]),
                  row(d, 0), full(wa), full(wb), full(wo), full(n2), full(rwh), full(rwl), full(rb)],
        out_specs=[row(d, 0), row(d // 2, 0), row(LANES, 0)],
        out_shape=[jax.ShapeDtypeStruct((t, d), F32), jax.ShapeDtypeStruct((t, d // 2), jnp.uint32),
                   jax.ShapeDtypeStruct((t, LANES), F32)],
        compiler_params=_params(("parallel",)),
        name="merge_out_proj",
    )(o_a, o_b, proj, proj, x, wa, wb, wo, n2, rwh, rwl, rb)


def _router_kernel(lg_ref, idx_ref, w_ref, rank_ref, cnt_ref, carry_s):
    @pl.when(pl.program_id(0) == 0)
    def _():
        carry_s[...] = jnp.zeros(carry_s.shape, F32)

    l = lg_ref[...]
    tm = l.shape[0]
    lane = lax.broadcasted_iota(jnp.int32, l.shape, 1).astype(F32)
    vals, idxs = [], []
    for _ in range(MOE_TOPK):
        mx = jnp.max(l, axis=-1, keepdims=True)
        ix = jnp.min(jnp.where(l == mx, lane, float(LANES)), axis=-1, keepdims=True)
        vals.append(mx)
        idxs.append(ix)
        l = jnp.where(lane == ix, -jnp.inf, l)
    es = [jnp.exp(v - vals[0]) for v in vals]
    tot = es[0]
    for e in es[1:]:
        tot = tot + e
    onehot = jnp.zeros(l.shape, F32)
    for ix in idxs:
        onehot = onehot + jnp.where(lane == ix, 1.0, 0.0)
    r = lax.broadcasted_iota(jnp.int32, (tm, tm), 0)
    c = lax.broadcasted_iota(jnp.int32, (tm, tm), 1)
    before = _dot((c < r).astype(BF16), onehot.astype(BF16)) + carry_s[...]
    idx_o = jnp.zeros(l.shape, F32)
    w_o = jnp.zeros(l.shape, F32)
    rank_o = jnp.zeros(l.shape, F32)
    for k in range(MOE_TOPK):
        rk = jnp.sum(jnp.where(lane == idxs[k], before, 0.0), axis=-1, keepdims=True)
        idx_o = jnp.where(lane == k, idxs[k], idx_o)
        w_o = jnp.where(lane == k, es[k] / tot, w_o)
        rank_o = jnp.where(lane == k, rk, rank_o)
    idx_ref[...] = idx_o.astype(jnp.int32)
    w_ref[...] = w_o
    rank_ref[...] = rank_o.astype(jnp.int32)
    carry_s[...] = carry_s[...] + jnp.sum(onehot, axis=0, keepdims=True)
    cnt_ref[...] = carry_s[...].astype(jnp.int32)


def _router(logits, tm):
    t = logits.shape[0]
    row = pl.BlockSpec((tm, LANES), lambda i: (i, 0))
    return pl.pallas_call(
        _router_kernel,
        grid=(t // tm,),
        in_specs=[row],
        out_specs=[row, row, row, pl.BlockSpec((1, LANES), lambda i: (0, 0))],
        out_shape=[jax.ShapeDtypeStruct((t, LANES), jnp.int32), jax.ShapeDtypeStruct((t, LANES), F32),
                   jax.ShapeDtypeStruct((t, LANES), jnp.int32), jax.ShapeDtypeStruct((1, LANES), jnp.int32)],
        scratch_shapes=[pltpu.VMEM((1, LANES), F32)],
        compiler_params=_params(("arbitrary",)),
        name="router",
    )(logits)


def _scatter_rows(dest_ref, base, x_ref, xs_hbm, sem):
    groups = x_ref.shape[0]
    per_group = SUBLANES * MOE_TOPK

    def issue(g, carry):
        for s in range(SUBLANES):
            for k in range(MOE_TOPK):
                dst = dest_ref[base + g * per_group + s * MOE_TOPK + k]
                pltpu.make_async_copy(x_ref.at[g, pl.ds(s, 1), :], xs_hbm.at[pl.ds(dst, 1), :], sem).start()
        return carry

    lax.fori_loop(0, groups, issue, 0)
    for _ in range(per_group):
        pltpu.make_async_copy(x_ref.at[:, 0, :], xs_hbm.at[pl.ds(0, groups), :], sem).wait()


def _dispatch_kernel(dest_ref, gap_start_ref, gap_len_ref, xa_ref, xb_ref, xs_hbm, zero_s, sem, sem_z):
    i = pl.program_id(0)
    last = pl.num_programs(0) - 1
    tile_copies = xa_ref.shape[0] * SUBLANES * MOE_TOPK

    def each_gap_row(fn):
        def per_gap(e, carry):
            start = gap_start_ref[e]
            lax.fori_loop(0, gap_len_ref[e], lambda r, c: (fn(start + r), c)[1], 0)
            return carry

        lax.fori_loop(0, gap_start_ref.shape[0], per_gap, 0)

    @pl.when(i == 0)
    def _():
        zero_s[...] = jnp.zeros(zero_s.shape, zero_s.dtype)
        each_gap_row(lambda row: pltpu.make_async_copy(zero_s, xs_hbm.at[pl.ds(row, 1), :], sem_z).start())

    @pl.when(i < last)
    def _():
        _scatter_rows(dest_ref, i * tile_copies, xa_ref, xs_hbm, sem)

    @pl.when(i == last)
    def _():
        _scatter_rows(dest_ref, last * tile_copies, xb_ref, xs_hbm, sem)
        each_gap_row(lambda row: pltpu.make_async_copy(zero_s, xs_hbm.at[pl.ds(row, 1), :], sem_z).wait())


def _dispatch(dest, xa_packed, xb_packed, gaps, n_rows, tm):
    ta, w = xa_packed.shape
    tb = xb_packed.shape[0]
    xa3 = xa_packed.reshape(ta // SUBLANES, SUBLANES, w)
    xb3 = xb_packed.reshape(tb // SUBLANES, SUBLANES, w)
    n_a = ta // tm
    return pl.pallas_call(
        _dispatch_kernel,
        grid_spec=pltpu.PrefetchScalarGridSpec(
            num_scalar_prefetch=3,
            grid=(n_a + 1,),
            in_specs=[pl.BlockSpec((tm // SUBLANES, SUBLANES, w), lambda i, *_: (jnp.minimum(i, n_a - 1), 0, 0)),
                      pl.BlockSpec(xb3.shape, lambda i, *_: (0, 0, 0))],
            out_specs=pl.BlockSpec(memory_space=pl.ANY),
            scratch_shapes=[pltpu.VMEM((1, w), xa_packed.dtype), pltpu.SemaphoreType.DMA(()),
                            pltpu.SemaphoreType.DMA(())]),
        out_shape=jax.ShapeDtypeStruct((n_rows, w), xa_packed.dtype),
        compiler_params=_params(("arbitrary",)),
        name="moe_dispatch",
    )(dest, gaps[0], gaps[1], xa3, xb3)


def _stream_expert_weights(first_ref, nxt_ref, be_ref, w_hbms, stages, casts, sems, tn):
    j, i = pl.program_id(0), pl.program_id(1)
    nj = pl.num_programs(0)

    def copies(e, jj):
        c0 = pl.multiple_of(jj * tn, tn)
        return [pltpu.make_async_copy(w.at[e, :, pl.ds(c0, tn)], st, sems.at[n])
                for n, (w, st) in enumerate(zip(w_hbms, stages))]

    @pl.when(first_ref[i] == 1)
    def _():
        @pl.when((j == 0) & (i == 0))
        def _():
            for c in copies(be_ref[0], 0):
                c.start()

        for c in copies(be_ref[i], j):
            c.wait()
        for st, cs in zip(stages, casts):
            cs[...] = st[...].astype(BF16)
        nxt = nxt_ref[i]

        @pl.when(nxt >= 0)
        def _():
            for c in copies(nxt, j):
                c.start()

        @pl.when((nxt < 0) & (j + 1 < nj))
        def _():
            for c in copies(be_ref[0], j + 1):
                c.start()


def _expert_up_kernel(be_ref, first_ref, nxt_ref, nu_ref, x_ref, wg_hbm, wu_hbm, bg_ref, bu_ref, o_ref,
                      stg_g, stg_u, wg_s, wu_s, sems):
    _stream_expert_weights(first_ref, nxt_ref, be_ref, (wg_hbm, wu_hbm), (stg_g, stg_u), (wg_s, wu_s), sems,
                           o_ref.shape[1])
    used = pl.program_id(1) < nu_ref[0]

    @pl.when(used)
    def _():
        x = _unpack_bf16_pairs(x_ref[...])
        tf = o_ref.shape[1]
        half = tf // 2
        e = be_ref[pl.program_id(1)]
        c0 = pl.multiple_of(pl.program_id(0) * tf, tf)
        bg = bg_ref[e, :, pl.ds(c0, tf)]
        bu = bu_ref[e, :, pl.ds(c0, tf)]
        for cs in (slice(0, half), slice(half, 2 * half)):
            gate = jnp.minimum(_dot(x, wg_s[:, cs]) + bg[:, cs], SWIGLU_LIMIT)
            up = jnp.clip(_dot(x, wu_s[:, cs]) + bu[:, cs], -SWIGLU_LIMIT, SWIGLU_LIMIT)
            o_ref[:, cs] = ((up + 1.0) * (gate * jax.nn.sigmoid(SWIGLU_ALPHA * gate))).astype(BF16)

    @pl.when(jnp.logical_not(used))
    def _():
        o_ref[...] = jnp.zeros(o_ref.shape, o_ref.dtype)


def _expert_down_kernel(be_ref, first_ref, nxt_ref, nu_ref, h_ref, wd_hbm, bd_ref, o_ref, stg, wd_s, sems):
    _stream_expert_weights(first_ref, nxt_ref, be_ref, (wd_hbm,), (stg,), (wd_s,), sems, wd_s.shape[1])
    used = pl.program_id(1) < nu_ref[0]

    @pl.when(used)
    def _():
        bd = bd_ref[be_ref[pl.program_id(1)]]
        o_ref[...] = _pack_bf16_pairs((_dot(h_ref[...], wd_s[...]) + bd).astype(BF16))

    @pl.when(jnp.logical_not(used))
    def _():
        o_ref[...] = jnp.zeros(o_ref.shape, o_ref.dtype)


def _used_block(i, nu):
    return jnp.minimum(i, nu[0] - 1)


def _expert_up(meta, xs, wg, wu, bg, bu, bm, tf):
    p, half = xs.shape
    d, f = wg.shape[1], wg.shape[2]
    bspec = pl.BlockSpec(bg.shape, lambda j, i, be, fi, nx, nu: (0, 0, 0))
    hbm = pl.BlockSpec(memory_space=pl.ANY)
    return pl.pallas_call(
        _expert_up_kernel,
        grid_spec=pltpu.PrefetchScalarGridSpec(
            num_scalar_prefetch=4,
            grid=(f // tf, p // bm),
            in_specs=[pl.BlockSpec((bm, half), lambda j, i, be, fi, nx, nu: (_used_block(i, nu), 0)),
                      hbm, hbm, bspec, bspec],
            out_specs=pl.BlockSpec((bm, tf), lambda j, i, be, fi, nx, nu: (i, j)),
            scratch_shapes=[pltpu.VMEM((d, tf), F32), pltpu.VMEM((d, tf), F32),
                            pltpu.VMEM((d, tf), BF16), pltpu.VMEM((d, tf), BF16),
                            pltpu.SemaphoreType.DMA((2,))]),
        out_shape=jax.ShapeDtypeStruct((p, f), BF16),
        compiler_params=_params(("arbitrary", "arbitrary")),
        name="moe_up",
    )(*meta, xs, wg, wu, bg, bu)


def _expert_down(meta, hid, wd, bd, bm):
    p, f = hid.shape
    d = wd.shape[2]
    return pl.pallas_call(
        _expert_down_kernel,
        grid_spec=pltpu.PrefetchScalarGridSpec(
            num_scalar_prefetch=4,
            grid=(1, p // bm),
            in_specs=[pl.BlockSpec((bm, f), lambda j, i, be, fi, nx, nu: (_used_block(i, nu), 0)),
                      pl.BlockSpec(memory_space=pl.ANY),
                      pl.BlockSpec(bd.shape, lambda j, i, be, fi, nx, nu: (0, 0, 0))],
            out_specs=pl.BlockSpec((bm, d // 2), lambda j, i, be, fi, nx, nu: (i, 0)),
            scratch_shapes=[pltpu.VMEM((f, d), F32), pltpu.VMEM((f, d), BF16), pltpu.SemaphoreType.DMA((1,))]),
        out_shape=jax.ShapeDtypeStruct((p, d // 2), jnp.uint32),
        compiler_params=_params(("arbitrary", "arbitrary")),
        name="moe_down",
    )(*meta, hid, wd, bd)


def _combine_kernel(dest_ref, ys_hbm, x1_ref, w_ref, p_ref, pw_ref, pg_ref, gw_ref, y_ref, buf, sems):
    tm = x1_ref.shape[0]
    groups = tm // SUBLANES
    per_group = SUBLANES * MOE_TOPK
    i = pl.program_id(0)
    slot = i % 2

    def issue_group(tile, sl, g):
        base = tile * tm * MOE_TOPK
        for s in range(SUBLANES):
            for k in range(MOE_TOPK):
                dst = dest_ref[base + g * per_group + s * MOE_TOPK + k]
                pltpu.make_async_copy(ys_hbm.at[pl.ds(dst, 1), :], buf.at[sl, k, g, pl.ds(s, 1), :],
                                      sems.at[sl]).start()

    def wait_slot(sl):
        for k in range(MOE_TOPK):
            for s in range(SUBLANES):
                pltpu.make_async_copy(ys_hbm.at[pl.ds(0, groups), :], buf.at[sl, k, :, s, :], sems.at[sl]).wait()

    last = pl.num_programs(0) - 1

    @pl.when(i == 0)
    def _():
        lax.fori_loop(0, groups, lambda g, c: (issue_group(0, 0, g), c)[1], 0)

    nxt_tile = jnp.minimum(i + 1, last)
    d = y_ref.shape[1]
    n_col = d // (2 * LANES)
    tc = d // n_col
    n_phase = 2 + MOE_TOPK + n_col
    issued = [0]

    def issue_share(phase):
        upto = groups * (phase + 1) // n_phase
        for g in range(issued[0], upto):
            issue_group(nxt_tile, 1 - slot, g)
        issued[0] = upto

    issue_share(0)
    e = _rms(_dot(p_ref[...].astype(BF16), pw_ref[...])) * pg_ref[...]
    issue_share(1)
    wait_slot(slot)
    w = w_ref[...]
    x2 = x1_ref[...]
    for k in range(MOE_TOPK):
        u = buf[slot, k].reshape(tm, buf.shape[-1])
        rows = jnp.concatenate([lax.bitcast_convert_type(u << 16, F32),
                                lax.bitcast_convert_type(u & jnp.uint32(0xFFFF0000), F32)], axis=1)
        x2 = x2 + w[:, k:k + 1] * rows
        issue_share(2 + k)
    xn = _rms(x2).astype(BF16)
    for t in range(n_col):
        cs = slice(t * tc, (t + 1) * tc)
        gate = jax.nn.sigmoid(_dot(xn, gw_ref[:, cs]))
        y_ref[:, cs] = x2[:, cs] + gate * e[:, cs]
        issue_share(2 + MOE_TOPK + t)

    @pl.when(i == last)
    def _():
        wait_slot(1 - slot)


def _combine(dest, ys, x1, w, p, pw, pg, gw, tm):
    t, d = x1.shape
    full = lambda a: pl.BlockSpec(a.shape, lambda i, dst: (0,) * a.ndim)
    row = lambda width: pl.BlockSpec((tm, width), lambda i, dst: (i, 0))
    return pl.pallas_call(
        _combine_kernel,
        grid_spec=pltpu.PrefetchScalarGridSpec(
            num_scalar_prefetch=1,
            grid=(t // tm,),
            in_specs=[pl.BlockSpec(memory_space=pl.ANY), row(d), row(LANES), row(p.shape[1]),
                      full(pw), full(pg), full(gw)],
            out_specs=row(d),
            scratch_shapes=[pltpu.VMEM((2, MOE_TOPK, tm // SUBLANES, SUBLANES, d // 2), jnp.uint32),
                            pltpu.SemaphoreType.DMA((2,))]),
        out_shape=jax.ShapeDtypeStruct((t, d), F32),
        compiler_params=_params(("arbitrary",)),
        name="moe_combine_ple",
    )(dest, ys, x1, w, p, pw, pg, gw)


def _largest_tile(n, cap, quantum):
    best = None
    for t in range(quantum, cap + 1, quantum):
        if n % t == 0:
            best = t
    assert best is not None, (n, cap, quantum)
    return best


def kernel(x_prompt, x_sample, cache_k, cache_v, state_gla, page_table, p_prompt, p_sample, norm1_g, w_in, q_norm_g, k_norm_g, gla_w_a2, gla_b_a, gla_norm_g, w_branch_a, w_branch_b, w_out, norm2_g, router_w, router_b, moe_w_gate, moe_b_gate, moe_w_up, moe_b_up, moe_w_down, moe_b_down, ple_w, ple_norm_g, ple_gate_w):
    assert norm1_g.shape[0] == 1, "one layer"
    b, s, d = x_prompt.shape
    nseq = x_sample.shape[0]
    assert x_sample.shape[1] == 1
    n_heads = d // (2 * HEAD_DIM_A)
    wa = n_heads * HEAD_DIM_A
    kb_w = N_HEADS_B * LANES
    vb_w = 2 * kb_w
    n_exp = router_w.shape[2]
    tp, ts = b * s, nseq

    sizes = [("q_a", wa), ("k_a", wa), ("v_a", wa), ("q_b", kb_w), ("k_b", kb_w), ("v_b", vb_w), ("g_b", vb_w),
             ("a_lr", GLA_LOWRANK), ("gate_a", d), ("gate_b", d)]
    cols, src, off = {}, 0, 0
    w_in0 = w_in[0]
    for name, width in sizes:
        if name == "a_lr":
            w_side = jnp.pad(w_in0[:, src:src + width], ((0, 0), (0, LANES - width))).astype(BF16)
            split = src
        else:
            cols[name] = off
            off += width
        src += width
    n_main = off
    tn = _largest_tile(n_main, 1024, 2 * LANES)
    w_main = jnp.concatenate([w_in0[:, :split], w_in0[:, split + GLA_LOWRANK:]], axis=1).astype(BF16)

    g1 = norm1_g[0][None, :]
    qg, kg = q_norm_g[0][None, :], k_norm_g[0][None, :]
    wa2 = jnp.pad(gla_w_a2[0], ((0, LANES - GLA_LOWRANK), (0, 0)))
    ba = gla_b_a[0][None, :]
    gn = gla_norm_g[0][None, :]
    wba, wbb, wo = w_branch_a[0].astype(BF16), w_branch_b[0].astype(BF16), w_out[0].astype(BF16)
    n2 = norm2_g[0][None, :]
    rw = jnp.pad(router_w[0], ((0, 0), (0, LANES - n_exp)))
    rwh = rw.astype(BF16)
    rwl = (rw - rwh.astype(F32)).astype(BF16)
    rb = jnp.concatenate([router_b[0], jnp.full((LANES - n_exp,), -jnp.inf, F32)])[None, :]

    xp = x_prompt.reshape(tp, d)
    proj_p, alr_p = _norm_matmul(xp, g1, w_main, w_side, tn)
    tr = _largest_tile(s, 512, 8)
    tabs_p = _rope_tables(jnp.arange(s, dtype=jnp.int32))
    q_p, k_p, v_p = _qk_rope(proj_p, qg, kg, tabs_p, n_heads, tr, lambda i: i % (s // tr))
    o_a_p = _moba_prompt(q_p.reshape(b, s, wa), k_p.reshape(b, s, wa), v_p.reshape(b, s, wa), n_heads)
    o_b_p, gla_p = _gla_prompt(proj_p.reshape(b, s, n_main), alr_p.reshape(b, s, LANES), wa2, ba, gn, cols)
    x1_p, h2_p, lg_p = _merge(o_a_p.reshape(tp, wa), o_b_p.reshape(tp, vb_w), proj_p, xp, wba, wbb, wo, n2,
                              rwh, rwl, rb, cols, _largest_tile(tp, 128, 8))

    xs_ = x_sample.reshape(ts, d)
    past = page_table.shape[1] * cache_k.shape[2]
    proj_s, alr_s = _norm_matmul(xs_, g1, w_main, w_side, tn)
    tabs_s = tuple(jnp.broadcast_to(t_, (ts, HEAD_DIM_A)) for t_ in _rope_tables(jnp.full((1,), past, jnp.int32)))
    q_s, k_s, v_s = _qk_rope(proj_s, qg, kg, tabs_s, n_heads, ts, lambda i: i)
    o_a_s = _moba_decode(q_s, k_s, v_s, cache_k[0], cache_v[0], page_table, n_heads)
    o_b_s, gla_s = _gla_decode(proj_s, alr_s, wa2, ba, gn, state_gla[0], cols)
    x1_s, h2_s, lg_s = _merge(o_a_s, o_b_s, proj_s, xs_, wba, wbb, wo, n2, rwh, rwl, rb, cols, ts)

    t_all = tp + ts
    logits = jnp.concatenate([lg_p, lg_s], axis=0)
    idx_o, w_o, rank_o, cnt = _router(logits, _largest_tile(t_all, 512, LANES))
    bm = MOE_ROW_BLOCK
    n_assign = t_all * MOE_TOPK
    n_blocks = -(-n_assign // bm) + n_exp
    counts = cnt[0, :n_exp]
    pcounts = (counts + bm - 1) // bm * bm
    pend = jnp.cumsum(pcounts)
    pstart = pend - pcounts
    expert_ids = jnp.arange(n_exp, dtype=jnp.int32)
    start_of = jnp.sum(jnp.where(idx_o[:, :MOE_TOPK, None] == expert_ids, pstart, 0), axis=-1)
    dest = (start_of + rank_o[:, :MOE_TOPK]).reshape(-1)
    n_used = pend[n_exp - 1] // bm
    blk = jnp.arange(n_blocks, dtype=jnp.int32)
    blk = jnp.minimum(blk, n_used - 1)
    block_e = jnp.sum(pend[None, :] <= (blk * bm)[:, None], axis=1).astype(jnp.int32)
    first = jnp.concatenate([jnp.ones((1,), jnp.int32), (block_e[1:] != block_e[:-1]).astype(jnp.int32)])
    nxt_pos = jnp.sum(block_e[None, :] <= block_e[:, None], axis=1)
    nxt = jnp.where(nxt_pos < n_blocks, block_e[jnp.minimum(nxt_pos, n_blocks - 1)], -1).astype(jnp.int32)
    meta = (block_e, first, nxt, n_used.reshape(1).astype(jnp.int32))

    n_rows = n_blocks * bm
    gaps = (jnp.concatenate([pstart + counts, pend[n_exp - 1:]]).astype(jnp.int32),
            jnp.concatenate([pcounts - counts, n_rows - pend[n_exp - 1:]]).astype(jnp.int32))
    xs_rows = _dispatch(dest, h2_p, h2_s, gaps, n_rows, _largest_tile(tp, 512, 8))
    f = moe_w_gate.shape[3]
    hid = _expert_up(meta, xs_rows, moe_w_gate[0], moe_w_up[0], moe_b_gate[0][:, None, :],
                     moe_b_up[0][:, None, :], bm, _largest_tile(f, 1024, 2 * LANES))
    ys = _expert_down(meta, hid, moe_w_down[0], moe_b_down[0][:, None, :], bm)

    pw, pg, gw = ple_w[0].astype(BF16), ple_norm_g[0][None, :], ple_gate_w[0].astype(BF16)
    tc = _largest_tile(ts, 128, 8)
    assert tp % tc == 0
    y_p = _combine(dest[:tp * MOE_TOPK], ys, x1_p, w_o[:tp], p_prompt[0].reshape(tp, -1), pw, pg, gw, tc)
    y_s = _combine(dest[tp * MOE_TOPK:], ys, x1_s, w_o[tp:], p_sample[0].reshape(ts, -1), pw, pg, gw, tc)

    hd = (n_heads, HEAD_DIM_A)
    return (y_p.reshape(b, s, d), y_s.reshape(nseq, 1, d),
            k_p.reshape(1, b, s, *hd), v_p.reshape(1, b, s, *hd), gla_p[None],
            k_s.reshape(1, nseq, 1, *hd), v_s.reshape(1, nseq, 1, *hd), gla_s[None])
```

```python
import functools

import jax
import jax.numpy as jnp
from jax import lax
from jax.experimental import pallas as pl
from jax.experimental.pallas import tpu as pltpu

HEAD_DIM_A = 128
MOBA_BLOCK = 256
MOBA_TOPK = 3
ROT_DIM = HEAD_DIM_A // 4
ROPE_THETA = 500000.0
N_HEADS_B = 4
GLA_LOWRANK = 16
GLA_TAU = 16.0
GLA_CHUNK = 64
MOE_TOPK = 4
SWIGLU_LIMIT = 7.0
SWIGLU_ALPHA = 1.702
EPS = 1e-6
NEG_INF = -1e30

LANES = 128
SUBLANES = 8
MOE_ROW_BLOCK = 256
VMEM_LIMIT = 56 * 1024 * 1024

F32 = jnp.float32
BF16 = jnp.bfloat16


def _params(sem):
    return pltpu.CompilerParams(dimension_semantics=sem, vmem_limit_bytes=VMEM_LIMIT)


def _dot(a, b):
    return jnp.dot(a, b, preferred_element_type=F32)


def _dot_nt(a, b):
    return lax.dot_general(a, b, (((1,), (1,)), ((), ())), preferred_element_type=F32)


def _dot_tn(a, b):
    return lax.dot_general(a, b, (((0,), (0,)), ((), ())), preferred_element_type=F32)


def _split2(x):
    hi = x.astype(BF16)
    lo = (x - hi.astype(F32)).astype(BF16)
    return hi, lo


def _rms(x):
    return x * lax.rsqrt(jnp.mean(x * x, axis=-1, keepdims=True) + EPS)


def _col_from_row(row):
    n = row.shape[1]
    r = lax.broadcasted_iota(jnp.int32, (n, n), 0)
    c = lax.broadcasted_iota(jnp.int32, (n, n), 1)
    return jnp.sum(jnp.where(r == c, jnp.broadcast_to(row, (n, n)), 0.0), axis=1, keepdims=True)


def _log_sigmoid(z):
    return jnp.minimum(z, 0.0) - jnp.log(1.0 + jnp.exp(-jnp.abs(z)))


def _pack_bf16_pairs(x16):
    half = x16.shape[1] // 2
    bits = lax.bitcast_convert_type(x16.astype(F32), jnp.uint32)
    return (bits[:, :half] >> 16) | bits[:, half:]


def _unpack_bf16_pairs(u):
    lo = lax.bitcast_convert_type(u << 16, F32).astype(BF16)
    hi = lax.bitcast_convert_type(u & jnp.uint32(0xFFFF0000), F32).astype(BF16)
    return jnp.concatenate([lo, hi], axis=1)


def _norm_matmul_kernel(x_ref, g_ref, w_ref, ws_ref, o_ref, os_ref, h_ref):
    @pl.when(pl.program_id(1) == 0)
    def _():
        h_ref[...] = (_rms(x_ref[...]) * g_ref[...]).astype(BF16)
        os_ref[...] = _dot(h_ref[...], ws_ref[...])

    o_ref[...] = _dot(h_ref[...], w_ref[...])


def _norm_matmul(x, g, w, w_side, tn):
    m, d = x.shape
    n = w.shape[1]
    ns = w_side.shape[1]
    tm = _largest_tile(m, 1024, SUBLANES)
    return pl.pallas_call(
        _norm_matmul_kernel,
        grid=(m // tm, n // tn),
        in_specs=[pl.BlockSpec((tm, d), lambda i, j: (i, 0)),
                  pl.BlockSpec((1, d), lambda i, j: (0, 0)),
                  pl.BlockSpec((d, tn), lambda i, j: (0, j)),
                  pl.BlockSpec((d, ns), lambda i, j: (0, 0))],
        out_specs=[pl.BlockSpec((tm, tn), lambda i, j: (i, j)), pl.BlockSpec((tm, ns), lambda i, j: (i, 0))],
        out_shape=[jax.ShapeDtypeStruct((m, n), F32), jax.ShapeDtypeStruct((m, ns), F32)],
        scratch_shapes=[pltpu.VMEM((tm, d), BF16)],
        compiler_params=_params(("parallel", "arbitrary")),
        name="in_proj",
    )(x, g, w, w_side)


def _qk_rope_kernel(q_ref, k_ref, v_ref, qg_ref, kg_ref, c_ref, sn_ref, sp_ref, qo_ref, ko_ref, vo_ref):
    c, sn, sp = c_ref[...], sn_ref[...], sp_ref[...]
    half = ROT_DIM // 2

    def norm_rope(x, g):
        y = _rms(x) * g
        up = pltpu.roll(y, LANES - half, 1)
        dn = pltpu.roll(y, half, 1)
        return y * c + up * sn + dn * sp

    qg, kg = qg_ref[...], kg_ref[...]
    for h in range(q_ref.shape[1] // HEAD_DIM_A):
        hs = slice(h * HEAD_DIM_A, (h + 1) * HEAD_DIM_A)
        qo_ref[:, hs] = norm_rope(q_ref[:, hs], qg)
        ko_ref[:, hs] = norm_rope(k_ref[:, hs], kg)
    vo_ref[...] = v_ref[...]


def _qk_rope(proj, qg, kg, tabs, n_heads, tr, tab_index):
    t = proj.shape[0]
    w = n_heads * HEAD_DIM_A
    col = lambda j: pl.BlockSpec((tr, w), lambda i: (i, j))
    vec = pl.BlockSpec((1, HEAD_DIM_A), lambda i: (0, 0))
    tab = pl.BlockSpec((tr, HEAD_DIM_A), lambda i: (tab_index(i), 0))
    return pl.pallas_call(
        _qk_rope_kernel,
        grid=(t // tr,),
        in_specs=[col(0), col(1), col(2), vec, vec, tab, tab, tab],
        out_specs=[col(0), col(0), col(0)],
        out_shape=[jax.ShapeDtypeStruct((t, w), F32)] * 3,
        compiler_params=_params(("parallel",)),
        name="qk_rope",
    )(proj, proj, proj, qg, kg, *tabs)


def _rope_tables(pos):
    half = ROT_DIM // 2
    inv = ROPE_THETA ** (-jnp.arange(half, dtype=F32) * 2.0 / ROT_DIM)
    ang = pos.astype(F32)[:, None] * inv[None, :]
    cos, sin = jnp.cos(ang), jnp.sin(ang)
    n = pos.shape[0]
    rest = HEAD_DIM_A - ROT_DIM
    c = jnp.concatenate([cos, cos, jnp.ones((n, rest), F32)], axis=1)
    sn = jnp.concatenate([-sin, jnp.zeros((n, half + rest), F32)], axis=1)
    sp = jnp.concatenate([jnp.zeros((n, half), F32), sin, jnp.zeros((n, rest), F32)], axis=1)
    return c, sn, sp


def _block_select(gate, n_cand, n_sel, max_cand):
    lane = lax.broadcasted_iota(jnp.int32, gate.shape, 1)
    gate = jnp.where(lane < n_cand, gate, -jnp.inf)
    rank = jnp.zeros(gate.shape, F32)
    for m in range(max_cand):
        gm = jnp.broadcast_to(gate[:, m:m + 1], gate.shape)
        beats = (gm > gate) | ((gm == gate) & (lane > m))
        rank = rank + jnp.where(beats, 1.0, 0.0)
    return jnp.where((rank < n_sel) & (lane < n_cand), 1.0, 0.0)


def _block_select_rows(gate_t, n_cand, n_sel):
    blk_id = lax.broadcasted_iota(jnp.int32, gate_t.shape, 0)
    gate_t = jnp.where(blk_id < n_cand, gate_t, -jnp.inf)
    rank = jnp.zeros(gate_t.shape, F32)
    for m in range(n_cand):
        gm = jnp.broadcast_to(gate_t[m:m + 1, :], gate_t.shape)
        beats = (gm > gate_t) | ((gm == gate_t) & (blk_id > m))
        rank = rank + jnp.where(beats, 1.0, 0.0)
    return jnp.where((rank < n_sel) & (blk_id < n_cand), 1.0, 0.0)


def _moba_prompt_kernel(q_ref, k_ref, v_ref, o_ref, kaug_s, v16_s, *, nb, n_sel):
    blk, d = MOBA_BLOCK, HEAD_DIM_A
    s_len = nb * blk
    kaug_s[:, :d] = k_ref[...].astype(BF16)
    key_blk = lax.broadcasted_iota(jnp.int32, (s_len, LANES), 0) // blk
    kaug_s[:, d:] = (key_blk == lax.broadcasted_iota(jnp.int32, (s_len, LANES), 1)).astype(BF16)
    v16_s[...] = v_ref[...].astype(BF16)
    nb_pad = -(-nb // 16) * 16
    means = [jnp.mean(k_ref[n * blk:(n + 1) * blk, :], axis=0, keepdims=True) for n in range(nb)]
    kmean = jnp.concatenate(means + [jnp.zeros((nb_pad - nb, d), F32)], axis=0)
    kh, kl = _split2(kmean)
    blk_id = lax.broadcasted_iota(jnp.int32, (nb_pad, blk), 0)
    to_lanes = (lax.broadcasted_iota(jnp.int32, (nb_pad, LANES), 0)
                == lax.broadcasted_iota(jnp.int32, (nb_pad, LANES), 1)).astype(BF16)
    row = lax.broadcasted_iota(jnp.int32, (blk, blk), 0)
    col = lax.broadcasted_iota(jnp.int32, (blk, blk), 1)
    scale = d ** -0.5
    for qb in range(nb):
        q = q_ref[qb * blk:(qb + 1) * blk, :]
        keep_t = jnp.where(blk_id == qb, 1.0, 0.0)
        if qb > 0:
            qh, ql = _split2(q)
            gate_t = _dot_nt(kh, qh) + _dot_nt(kl, qh) + _dot_nt(kh, ql)
            keep_t = keep_t + _block_select_rows(gate_t, qb, n_sel)
        keep = _dot_tn(keep_t.astype(BF16), to_lanes)
        bias = ((1.0 - keep) * NEG_INF).astype(BF16)
        nk = (qb + 1) * blk
        s = _dot_nt(jnp.concatenate([(q * scale).astype(BF16), bias], axis=1), kaug_s[:nk, :])
        s_own = jnp.where(col <= row, s[:, qb * blk:], NEG_INF)
        s = s_own if qb == 0 else jnp.concatenate([s[:, :qb * blk], s_own], axis=1)
        m = jnp.max(s, axis=-1, keepdims=True)
        p = jnp.exp(s - m)
        l = jnp.sum(p, axis=-1, keepdims=True)
        o_ref[qb * blk:(qb + 1) * blk, :] = _dot(p.astype(BF16), v16_s[:nk, :]) / l


def _moba_prompt(q, k, v, n_heads):
    b, s, _ = q.shape
    blk = MOBA_BLOCK
    assert s % blk == 0 and s // blk <= LANES
    nb = s // blk
    n_sel = max(1, min(MOBA_TOPK, nb - 1))
    d = HEAD_DIM_A
    spec = pl.BlockSpec((None, s, d), lambda bi, h: (bi, 0, h))
    return pl.pallas_call(
        functools.partial(_moba_prompt_kernel, nb=nb, n_sel=n_sel),
        grid=(b, n_heads),
        in_specs=[spec, spec, spec],
        out_specs=spec,
        out_shape=jax.ShapeDtypeStruct(q.shape, F32),
        scratch_shapes=[pltpu.VMEM((s, d + LANES), BF16), pltpu.VMEM((s, d), BF16)],
        compiler_params=_params(("parallel", "parallel")),
        name="moba_prompt",
    )(q, k, v)


def _moba_decode_kernel(pt_ref, q_ref, kn_ref, vn_ref, *refs, n_pages, page, n_sel):
    k_refs = refs[:n_pages]
    v_refs = refs[n_pages:2 * n_pages]
    o_ref = refs[2 * n_pages]
    ppb = MOBA_BLOCK // page
    nb = n_pages // ppb
    q = q_ref[...]
    scale = HEAD_DIM_A ** -0.5

    lane = lax.broadcasted_iota(jnp.int32, (q.shape[0], LANES), 1)
    gate = jnp.zeros((q.shape[0], LANES), F32)
    for n in range(nb):
        ksum = jnp.zeros(q.shape, F32)
        for j in range(ppb):
            ksum = ksum + jnp.sum(k_refs[n * ppb + j][...], axis=0)
        g = jnp.sum(q * (ksum * (1.0 / MOBA_BLOCK)), axis=-1, keepdims=True)
        gate = jnp.where(lane == n, g, gate)
    sel = _block_select(gate, nb, n_sel, nb)

    n_h, d = q.shape
    qs = q * scale
    ones = jnp.ones((d, LANES), BF16)
    m = jnp.broadcast_to(jnp.sum(qs * kn_ref[...], axis=-1, keepdims=True), (n_h, LANES))
    l = jnp.ones_like(m)
    acc = vn_ref[...]
    for pg in range(n_pages):
        n = pg // ppb
        keep = jnp.broadcast_to(sel[:, n:n + 1], (n_h, LANES)) > 0.5
        prod = (k_refs[pg][...] * qs[None]).reshape(page * n_h, d).astype(BF16)
        s = _dot(prod, ones).reshape(page, n_h, LANES)
        s = jnp.where(keep[None], s, NEG_INF)
        m_new = jnp.maximum(m, jnp.max(s, axis=0))
        alpha = jnp.exp(m - m_new)
        p = jnp.exp(s - m_new[None])
        l = alpha * l + jnp.sum(p, axis=0)
        acc = alpha * acc + jnp.sum(p * v_refs[pg][...], axis=0)
        m = m_new
    o_ref[...] = acc / l


def _moba_decode(q, k_new, v_new, cache_k, cache_v, page_table, n_heads):
    nseq, n_pages = page_table.shape
    page = cache_k.shape[1]
    d = HEAD_DIM_A
    nb_all = -(-(n_pages * page + 1) // MOBA_BLOCK)
    n_sel = max(1, min(MOBA_TOPK, nb_all - 1))
    assert MOBA_BLOCK % page == 0 and (n_pages * page) % MOBA_BLOCK == 0
    hd = pl.BlockSpec((None, n_heads, d), lambda b, pt: (b, 0, 0))

    def page_spec(pg):
        return pl.BlockSpec((None, page, n_heads, d), lambda b, pt: (pt[b * n_pages + pg], 0, 0, 0))

    specs = [page_spec(pg) for pg in range(n_pages)]
    q3 = q.reshape(nseq, n_heads, d)
    return pl.pallas_call(
        functools.partial(_moba_decode_kernel, n_pages=n_pages, page=page, n_sel=n_sel),
        grid_spec=pltpu.PrefetchScalarGridSpec(
            num_scalar_prefetch=1,
            grid=(nseq,),
            in_specs=[hd, hd, hd] + specs + specs,
            out_specs=hd),
        out_shape=jax.ShapeDtypeStruct((nseq, n_heads, d), F32),
        compiler_params=_params(("parallel",)),
        name="moba_decode",
    )(page_table.reshape(-1), q3, k_new.reshape(nseq, n_heads, d), v_new.reshape(nseq, n_heads, d),
      *([cache_k] * n_pages), *([cache_v] * n_pages)).reshape(nseq, n_heads * d)


GLA_UNROLL = 8


def _gla_out(o, gn, g):
    return _rms(o) * gn * (g * jax.nn.sigmoid(g))


def _gla_prompt_kernel(q_ref, k_ref, v_ref, g_ref, alr_ref, wa2_ref, ba_ref, gn_ref, o_ref, sfin_ref, la_s,
                       *, n_chunks):
    c = GLA_CHUNK
    dk = q_ref.shape[1]
    z = _dot(alr_ref[...].astype(BF16), wa2_ref[...].astype(BF16)) + ba_ref[...]
    la_s[...] = _log_sigmoid(z) * (1.0 / GLA_TAU)
    row = lax.broadcasted_iota(jnp.int32, (c, c), 0)
    col = lax.broadcasted_iota(jnp.int32, (c, c), 1)
    causal = col <= row
    tril = causal.astype(BF16)
    gn = gn_ref[...]

    def body(i, st):
        r0 = pl.multiple_of(i * c, c)
        qc = q_ref[pl.ds(r0, c), :] * (dk ** -0.5)
        kc = k_ref[pl.ds(r0, c), :]
        vc = v_ref[pl.ds(r0, c), :].astype(BF16)
        ac = la_s[pl.ds(r0, c), :]
        a1 = ac.astype(BF16)
        r1 = ac - a1.astype(F32)
        a2 = r1.astype(BF16)
        a3 = (r1 - a2.astype(F32)).astype(BF16)
        b = _dot(tril, a1) + _dot(tril, a2) + _dot(tril, a3)
        q_t = (qc * jnp.exp(b)).astype(BF16)
        k_t = (kc * jnp.exp(-b)).astype(BF16)
        att = jnp.where(causal, _dot_nt(q_t, k_t), 0.0)
        b_last = b[c - 1:c, :]
        k_dec = (kc * jnp.exp(b_last - b)).astype(BF16)
        o_intra = _dot(att.astype(BF16), vc)
        st_in = _dot_tn(k_dec, vc)
        decay = _col_from_row(jnp.exp(b_last))
        o = o_intra + _dot(q_t, st.astype(BF16))
        o_ref[pl.ds(r0, c), :] = _gla_out(o, gn, g_ref[pl.ds(r0, c), :])
        return decay * st + st_in

    sfin_ref[...] = lax.fori_loop(0, n_chunks, body, jnp.zeros(sfin_ref.shape, F32), unroll=GLA_UNROLL)


def _gla_prompt(proj3, alr3, wa2, ba, gn, cols):
    b, s, _ = proj3.shape
    dk, dv = LANES, 2 * LANES
    h = N_HEADS_B
    assert s % GLA_CHUNK == 0
    sk = lambda off: pl.BlockSpec((None, s, dk), lambda bi, hi: (bi, 0, off + hi))
    sv = lambda off: pl.BlockSpec((None, s, dv), lambda bi, hi: (bi, 0, off + hi))
    return pl.pallas_call(
        functools.partial(_gla_prompt_kernel, n_chunks=s // GLA_CHUNK),
        grid=(b, h),
        in_specs=[sk(cols["q_b"] // dk), sk(cols["k_b"] // dk), sv(cols["v_b"] // dv), sv(cols["g_b"] // dv),
                  pl.BlockSpec((None, s, LANES), lambda bi, hi: (bi, 0, 0)),
                  pl.BlockSpec((LANES, dk), lambda bi, hi: (0, hi)),
                  pl.BlockSpec((1, dk), lambda bi, hi: (0, hi)),
                  pl.BlockSpec((1, dv), lambda bi, hi: (0, 0))],
        out_specs=[pl.BlockSpec((None, s, dv), lambda bi, hi: (bi, 0, hi)),
                   pl.BlockSpec((None, None, dk, dv), lambda bi, hi: (bi, hi, 0, 0))],
        out_shape=[jax.ShapeDtypeStruct((b, s, h * dv), F32), jax.ShapeDtypeStruct((b, h, dk, dv), F32)],
        scratch_shapes=[pltpu.VMEM((s, dk), F32)],
        compiler_params=_params(("parallel", "parallel")),
        name="gla_prompt",
    )(proj3, proj3, proj3, proj3, alr3, wa2, ba, gn)


def _gla_decode_kernel(q_ref, k_ref, v_ref, g_ref, alr_ref, wa2_ref, ba_ref, gn_ref, s0_ref, o_ref, s_ref):
    dk, dv = LANES, 2 * LANES
    wa2 = wa2_ref[...].astype(BF16)
    gn = gn_ref[...]

    def one_sequence(s, carry):
        alr = jnp.broadcast_to(alr_ref[s], (SUBLANES, LANES)).astype(BF16)
        z = _dot(alr, wa2)[0:1, :] + ba_ref[...]
        a = jnp.exp(_log_sigmoid(z) * (1.0 / GLA_TAU))
        q, k, v, g = q_ref[s], k_ref[s], v_ref[s], g_ref[s]
        outs = []
        for h in range(N_HEADS_B):
            ks = slice(h * dk, (h + 1) * dk)
            vs = slice(h * dv, (h + 1) * dv)
            a_col = _col_from_row(a[:, ks])
            k_col = _col_from_row(k[:, ks])
            q_col = _col_from_row(q[:, ks] * (dk ** -0.5))
            s_new = a_col * s0_ref[s, h] + k_col * v[:, vs]
            s_ref[s, h] = s_new
            o = jnp.sum(q_col * s_new, axis=0, keepdims=True)
            outs.append(_gla_out(o, gn, g[:, vs]))
        o_ref[s] = jnp.concatenate(outs, axis=1)
        return carry

    lax.fori_loop(0, q_ref.shape[0], one_sequence, 0)


def _gla_decode(proj_s, alr_s, wa2, ba, gn, state, cols):
    t, n = proj_s.shape
    dk, dv = LANES, 2 * LANES
    h = N_HEADS_B
    p3 = proj_s.reshape(t, 1, n)
    a3 = alr_s.reshape(t, 1, LANES)
    sb = _largest_tile(t, SUBLANES, 1)
    blk = lambda w, off: pl.BlockSpec((sb, 1, w), lambda i: (i, 0, off // w))
    full = lambda a: pl.BlockSpec(a.shape, lambda i: (0,) * a.ndim)
    st = pl.BlockSpec((sb, h, dk, dv), lambda i: (i, 0, 0, 0))
    o, s_new = pl.pallas_call(
        _gla_decode_kernel,
        grid=(t // sb,),
        in_specs=[blk(h * dk, cols["q_b"]), blk(h * dk, cols["k_b"]), blk(h * dv, cols["v_b"]),
                  blk(h * dv, cols["g_b"]), blk(LANES, 0), full(wa2), full(ba), full(gn), st],
        out_specs=[pl.BlockSpec((sb, 1, h * dv), lambda i: (i, 0, 0)), st],
        out_shape=[jax.ShapeDtypeStruct((t, 1, h * dv), F32), jax.ShapeDtypeStruct(state.shape, F32)],
        compiler_params=_params(("parallel",)),
        name="gla_decode",
    )(p3, p3, p3, p3, a3, wa2, ba, gn, state)
    return o.reshape(t, h * dv), s_new


def _merge_kernel(oa_ref, ob_ref, ga_ref, gb_ref, x_ref, wa_ref, wb_ref, wo_ref, n2_ref, rwh_ref, rwl_ref, rb_ref,
                  x1_ref, h2_ref, lg_ref):
    ya = _dot(oa_ref[...].astype(BF16), wa_ref[...])
    yb = _dot(ob_ref[...].astype(BF16), wb_ref[...])
    merged = jax.nn.sigmoid(ga_ref[...]) * ya + jax.nn.sigmoid(gb_ref[...]) * yb
    x1 = x_ref[...] + _dot(merged.astype(BF16), wo_ref[...])
    x1_ref[...] = x1
    h2 = _rms(x1) * n2_ref[...]
    hh, hl = _split2(h2)
    lg_ref[...] = _dot(hh, rwh_ref[...]) + _dot(hl, rwh_ref[...]) + _dot(hh, rwl_ref[...]) + rb_ref[...]
    h2_ref[...] = _pack_bf16_pairs(hh)


def _merge(o_a, o_b, proj, x, wa, wb, wo, n2, rwh, rwl, rb, cols, tm):
    t, d = x.shape
    row = lambda w, off: pl.BlockSpec((tm, w), lambda i: (i, off // w))
    full = lambda a: pl.BlockSpec(a.shape, lambda i: (0,) * a.ndim)
    return pl.pallas_call(
        _merge_kernel,
        grid=(t // tm,),
        in_specs=[row(o_a.shape[1], 0), row(o_b.shape[1], 0), row(d, cols["gate_a"]), row(d, cols["gate_b"]),
                  row(d, 0), full(wa), full(wb), full(wo), full(n2), full(rwh), full(rwl), full(rb)],
        out_specs=[row(d, 0), row(d // 2, 0), row(LANES, 0)],
        out_shape=[jax.ShapeDtypeStruct((t, d), F32), jax.ShapeDtypeStruct((t, d // 2), jnp.uint32),
                   jax.ShapeDtypeStruct((t, LANES), F32)],
        compiler_params=_params(("parallel",)),
        name="merge_out_proj",
    )(o_a, o_b, proj, proj, x, wa, wb, wo, n2, rwh, rwl, rb)


def _router_kernel(lg_ref, idx_ref, w_ref, rank_ref, cnt_ref, carry_s):
    @pl.when(pl.program_id(0) == 0)
    def _():
        carry_s[...] = jnp.zeros(carry_s.shape, F32)

    l = lg_ref[...]
    tm = l.shape[0]
    lane = lax.broadcasted_iota(jnp.int32, l.shape, 1).astype(F32)
    vals, idxs = [], []
    for _ in range(MOE_TOPK):
        mx = jnp.max(l, axis=-1, keepdims=True)
        ix = jnp.min(jnp.where(l == mx, lane, float(LANES)), axis=-1, keepdims=True)
        vals.append(mx)
        idxs.append(ix)
        l = jnp.where(lane == ix, -jnp.inf, l)
    es = [jnp.exp(v - vals[0]) for v in vals]
    tot = es[0]
    for e in es[1:]:
        tot = tot + e
    onehot = jnp.zeros(l.shape, F32)
    for ix in idxs:
        onehot = onehot + jnp.where(lane == ix, 1.0, 0.0)
    r = lax.broadcasted_iota(jnp.int32, (tm, tm), 0)
    c = lax.broadcasted_iota(jnp.int32, (tm, tm), 1)
    before = _dot((c < r).astype(BF16), onehot.astype(BF16)) + carry_s[...]
    idx_o = jnp.zeros(l.shape, F32)
    w_o = jnp.zeros(l.shape, F32)
    rank_o = jnp.zeros(l.shape, F32)
    for k in range(MOE_TOPK):
        rk = jnp.sum(jnp.where(lane == idxs[k], before, 0.0), axis=-1, keepdims=True)
        idx_o = jnp.where(lane == k, idxs[k], idx_o)
        w_o = jnp.where(lane == k, es[k] / tot, w_o)
        rank_o = jnp.where(lane == k, rk, rank_o)
    idx_ref[...] = idx_o.astype(jnp.int32)
    w_ref[...] = w_o
    rank_ref[...] = rank_o.astype(jnp.int32)
    carry_s[...] = carry_s[...] + jnp.sum(onehot, axis=0, keepdims=True)
    cnt_ref[...] = carry_s[...].astype(jnp.int32)


def _router(logits, tm):
    t = logits.shape[0]
    row = pl.BlockSpec((tm, LANES), lambda i: (i, 0))
    return pl.pallas_call(
        _router_kernel,
        grid=(t // tm,),
        in_specs=[row],
        out_specs=[row, row, row, pl.BlockSpec((1, LANES), lambda i: (0, 0))],
        out_shape=[jax.ShapeDtypeStruct((t, LANES), jnp.int32), jax.ShapeDtypeStruct((t, LANES), F32),
                   jax.ShapeDtypeStruct((t, LANES), jnp.int32), jax.ShapeDtypeStruct((1, LANES), jnp.int32)],
        scratch_shapes=[pltpu.VMEM((1, LANES), F32)],
        compiler_params=_params(("arbitrary",)),
        name="router",
    )(logits)


def _scatter_rows(dest_ref, base, x_ref, xs_hbm, sem):
    groups = x_ref.shape[0]
    per_group = SUBLANES * MOE_TOPK

    def issue(g, carry):
        for s in range(SUBLANES):
            for k in range(MOE_TOPK):
                dst = dest_ref[base + g * per_group + s * MOE_TOPK + k]
                pltpu.make_async_copy(x_ref.at[g, pl.ds(s, 1), :], xs_hbm.at[pl.ds(dst, 1), :],
                                      sem).start(priority=k % 2)
        return carry

    lax.fori_loop(0, groups, issue, 0)
    for _ in range(per_group):
        pltpu.make_async_copy(x_ref.at[:, 0, :], xs_hbm.at[pl.ds(0, groups), :], sem).wait()


def _dispatch_kernel(dest_ref, x_ref, xs_in_hbm, xs_hbm, sem):
    del xs_in_hbm
    tile_copies = x_ref.shape[0] * SUBLANES * MOE_TOPK
    _scatter_rows(dest_ref, pl.program_id(0) * tile_copies, x_ref, xs_hbm, sem)


def _dispatch(dest, x_packed, xs, tm):
    t, w = x_packed.shape
    x3 = x_packed.reshape(t // SUBLANES, SUBLANES, w)
    hbm = pl.BlockSpec(memory_space=pl.ANY)
    return pl.pallas_call(
        _dispatch_kernel,
        grid_spec=pltpu.PrefetchScalarGridSpec(
            num_scalar_prefetch=1,
            grid=(t // tm,),
            in_specs=[pl.BlockSpec((tm // SUBLANES, SUBLANES, w), lambda i, dst: (i, 0, 0)), hbm],
            out_specs=hbm,
            scratch_shapes=[pltpu.SemaphoreType.DMA(())]),
        out_shape=jax.ShapeDtypeStruct(xs.shape, xs.dtype),
        input_output_aliases={2: 0},
        compiler_params=_params(("arbitrary",)),
        name="moe_dispatch",
    )(dest, x3, xs)


def _stream_expert_weights(first_ref, nxt_ref, be_ref, w_hbms, stages, casts, sems, tn):
    j, i = pl.program_id(0), pl.program_id(1)
    nj = pl.num_programs(0)

    def copies(e, jj):
        c0 = pl.multiple_of(jj * tn, tn)
        return [pltpu.make_async_copy(w.at[e, :, pl.ds(c0, tn)], st, sems.at[n])
                for n, (w, st) in enumerate(zip(w_hbms, stages))]

    @pl.when(first_ref[i] == 1)
    def _():
        @pl.when((j == 0) & (i == 0))
        def _():
            for c in copies(be_ref[0], 0):
                c.start()

        for c in copies(be_ref[i], j):
            c.wait()
        for st, cs in zip(stages, casts):
            cs[...] = st[...].astype(BF16)
        nxt = nxt_ref[i]

        @pl.when(nxt >= 0)
        def _():
            for c in copies(nxt, j):
                c.start()

        @pl.when((nxt < 0) & (j + 1 < nj))
        def _():
            for c in copies(be_ref[0], j + 1):
                c.start()


def _expert_up_kernel(be_ref, first_ref, nxt_ref, nu_ref, x_ref, wg_hbm, wu_hbm, bg_ref, bu_ref, o_ref,
                      stg_g, stg_u, wg_s, wu_s, sems):
    _stream_expert_weights(first_ref, nxt_ref, be_ref, (wg_hbm, wu_hbm), (stg_g, stg_u), (wg_s, wu_s), sems,
                           o_ref.shape[1])
    used = pl.program_id(1) < nu_ref[0]

    @pl.when(used)
    def _():
        x = _unpack_bf16_pairs(x_ref[...])
        tf = o_ref.shape[1]
        half = tf // 2
        e = be_ref[pl.program_id(1)]
        c0 = pl.multiple_of(pl.program_id(0) * tf, tf)
        bg = bg_ref[e, :, pl.ds(c0, tf)]
        bu = bu_ref[e, :, pl.ds(c0, tf)]
        for cs in (slice(0, half), slice(half, 2 * half)):
            gate = jnp.minimum(_dot(x, wg_s[:, cs]) + bg[:, cs], SWIGLU_LIMIT)
            up = jnp.clip(_dot(x, wu_s[:, cs]) + bu[:, cs], -SWIGLU_LIMIT, SWIGLU_LIMIT)
            o_ref[:, cs] = ((up + 1.0) * (gate * jax.nn.sigmoid(SWIGLU_ALPHA * gate))).astype(BF16)

    @pl.when(jnp.logical_not(used))
    def _():
        o_ref[...] = jnp.zeros(o_ref.shape, o_ref.dtype)


def _expert_down_kernel(be_ref, first_ref, nxt_ref, nu_ref, h_ref, wd_hbm, bd_ref, o_ref, stg, wd_s, sems):
    _stream_expert_weights(first_ref, nxt_ref, be_ref, (wd_hbm,), (stg,), (wd_s,), sems, wd_s.shape[1])
    used = pl.program_id(1) < nu_ref[0]

    @pl.when(used)
    def _():
        bd = bd_ref[be_ref[pl.program_id(1)]]
        o_ref[...] = _pack_bf16_pairs((_dot(h_ref[...], wd_s[...]) + bd).astype(BF16))

    @pl.when(jnp.logical_not(used))
    def _():
        o_ref[...] = jnp.zeros(o_ref.shape, o_ref.dtype)


def _used_block(i, nu):
    return jnp.minimum(i, nu[0] - 1)


def _expert_up(meta, xs, wg, wu, bg, bu, bm, tf):
    p, half = xs.shape
    d, f = wg.shape[1], wg.shape[2]
    bspec = pl.BlockSpec(bg.shape, lambda j, i, be, fi, nx, nu: (0, 0, 0))
    hbm = pl.BlockSpec(memory_space=pl.ANY)
    return pl.pallas_call(
        _expert_up_kernel,
        grid_spec=pltpu.PrefetchScalarGridSpec(
            num_scalar_prefetch=4,
            grid=(f // tf, p // bm),
            in_specs=[pl.BlockSpec((bm, half), lambda j, i, be, fi, nx, nu: (_used_block(i, nu), 0)),
                      hbm, hbm, bspec, bspec],
            out_specs=pl.BlockSpec((bm, tf), lambda j, i, be, fi, nx, nu: (i, j)),
            scratch_shapes=[pltpu.VMEM((d, tf), F32), pltpu.VMEM((d, tf), F32),
                            pltpu.VMEM((d, tf), BF16), pltpu.VMEM((d, tf), BF16),
                            pltpu.SemaphoreType.DMA((2,))]),
        out_shape=jax.ShapeDtypeStruct((p, f), BF16),
        compiler_params=_params(("arbitrary", "arbitrary")),
        name="moe_up",
    )(*meta, xs, wg, wu, bg, bu)


def _expert_down(meta, hid, wd, bd, bm):
    p, f = hid.shape
    d = wd.shape[2]
    return pl.pallas_call(
        _expert_down_kernel,
        grid_spec=pltpu.PrefetchScalarGridSpec(
            num_scalar_prefetch=4,
            grid=(1, p // bm),
            in_specs=[pl.BlockSpec((bm, f), lambda j, i, be, fi, nx, nu: (_used_block(i, nu), 0)),
                      pl.BlockSpec(memory_space=pl.ANY),
                      pl.BlockSpec(bd.shape, lambda j, i, be, fi, nx, nu: (0, 0, 0))],
            out_specs=pl.BlockSpec((bm, d // 2), lambda j, i, be, fi, nx, nu: (i, 0)),
            scratch_shapes=[pltpu.VMEM((f, d), F32), pltpu.VMEM((f, d), BF16), pltpu.SemaphoreType.DMA((1,))]),
        out_shape=jax.ShapeDtypeStruct((p, d // 2), jnp.uint32),
        compiler_params=_params(("arbitrary", "arbitrary")),
        name="moe_down",
    )(*meta, hid, wd, bd)


def _combine_kernel(dest_ref, ys_hbm, x1_ref, w_ref, p_ref, pw_ref, pg_ref, gw_ref, y_ref, buf, sems):
    tm = x1_ref.shape[0]
    groups = tm // SUBLANES
    per_group = SUBLANES * MOE_TOPK
    i = pl.program_id(0)
    slot = i % 2

    def issue_group(tile, sl, g):
        base = tile * tm * MOE_TOPK
        for s in range(SUBLANES):
            for k in range(MOE_TOPK):
                dst = dest_ref[base + g * per_group + s * MOE_TOPK + k]
                pltpu.make_async_copy(ys_hbm.at[pl.ds(dst, 1), :], buf.at[sl, k, g, pl.ds(s, 1), :],
                                      sems.at[sl]).start(priority=k % 2)

    def wait_slot(sl):
        for k in range(MOE_TOPK):
            for s in range(SUBLANES):
                pltpu.make_async_copy(ys_hbm.at[pl.ds(0, groups), :], buf.at[sl, k, :, s, :], sems.at[sl]).wait()

    last = pl.num_programs(0) - 1

    @pl.when(i == 0)
    def _():
        lax.fori_loop(0, groups, lambda g, c: (issue_group(0, 0, g), c)[1], 0)

    nxt_tile = jnp.minimum(i + 1, last)
    d = y_ref.shape[1]
    n_col = d // (2 * LANES)
    tc = d // n_col
    n_phase = 2 + MOE_TOPK + n_col
    issued = [0]

    def issue_share(phase):
        upto = groups * (phase + 1) // n_phase
        for g in range(issued[0], upto):
            issue_group(nxt_tile, 1 - slot, g)
        issued[0] = upto

    issue_share(0)
    e = _rms(_dot(p_ref[...].astype(BF16), pw_ref[...])) * pg_ref[...]
    issue_share(1)
    wait_slot(slot)
    w = w_ref[...]
    x2 = x1_ref[...]
    for k in range(MOE_TOPK):
        u = buf[slot, k].reshape(tm, buf.shape[-1])
        rows = jnp.concatenate([lax.bitcast_convert_type(u << 16, F32),
                                lax.bitcast_convert_type(u & jnp.uint32(0xFFFF0000), F32)], axis=1)
        x2 = x2 + w[:, k:k + 1] * rows
        issue_share(2 + k)
    xn = _rms(x2).astype(BF16)
    for t in range(n_col):
        cs = slice(t * tc, (t + 1) * tc)
        gate = jax.nn.sigmoid(_dot(xn, gw_ref[:, cs]))
        y_ref[:, cs] = x2[:, cs] + gate * e[:, cs]
        issue_share(2 + MOE_TOPK + t)

    @pl.when(i == last)
    def _():
        wait_slot(1 - slot)


def _combine(dest, ys, x1, w, p, pw, pg, gw, tm):
    t, d = x1.shape
    full = lambda a: pl.BlockSpec(a.shape, lambda i, dst: (0,) * a.ndim)
    row = lambda width: pl.BlockSpec((tm, width), lambda i, dst: (i, 0))
    return pl.pallas_call(
        _combine_kernel,
        grid_spec=pltpu.PrefetchScalarGridSpec(
            num_scalar_prefetch=1,
            grid=(t // tm,),
            in_specs=[pl.BlockSpec(memory_space=pl.ANY), row(d), row(LANES), row(p.shape[1]),
                      full(pw), full(pg), full(gw)],
            out_specs=row(d),
            scratch_shapes=[pltpu.VMEM((2, MOE_TOPK, tm // SUBLANES, SUBLANES, d // 2), jnp.uint32),
                            pltpu.SemaphoreType.DMA((2,))]),
        out_shape=jax.ShapeDtypeStruct((t, d), F32),
        compiler_params=_params(("arbitrary",)),
        name="moe_combine_ple",
    )(dest, ys, x1, w, p, pw, pg, gw)


def _largest_tile(n, cap, quantum):
    best = None
    for t in range(quantum, cap + 1, quantum):
        if n % t == 0:
            best = t
    assert best is not None, (n, cap, quantum)
    return best


def kernel(x_prompt, x_sample, cache_k, cache_v, state_gla, page_table, p_prompt, p_sample, norm1_g, w_in, q_norm_g, k_norm_g, gla_w_a2, gla_b_a, gla_norm_g, w_branch_a, w_branch_b, w_out, norm2_g, router_w, router_b, moe_w_gate, moe_b_gate, moe_w_up, moe_b_up, moe_w_down, moe_b_down, ple_w, ple_norm_g, ple_gate_w):
    assert norm1_g.shape[0] == 1, "one layer"
    b, s, d = x_prompt.shape
    nseq = x_sample.shape[0]
    assert x_sample.shape[1] == 1
    n_heads = d // (2 * HEAD_DIM_A)
    wa = n_heads * HEAD_DIM_A
    kb_w = N_HEADS_B * LANES
    vb_w = 2 * kb_w
    n_exp = router_w.shape[2]
    tp, ts = b * s, nseq

    sizes = [("q_a", wa), ("k_a", wa), ("v_a", wa), ("q_b", kb_w), ("k_b", kb_w), ("v_b", vb_w), ("g_b", vb_w),
             ("a_lr", GLA_LOWRANK), ("gate_a", d), ("gate_b", d)]
    cols, src, off = {}, 0, 0
    w_in0 = w_in[0]
    for name, width in sizes:
        if name == "a_lr":
            w_side = jnp.pad(w_in0[:, src:src + width], ((0, 0), (0, LANES - width))).astype(BF16)
            split = src
        else:
            cols[name] = off
            off += width
        src += width
    n_main = off
    tn = _largest_tile(n_main, 1024, 2 * LANES)
    w_main = jnp.concatenate([w_in0[:, :split], w_in0[:, split + GLA_LOWRANK:]], axis=1).astype(BF16)

    g1 = norm1_g[0][None, :]
    qg, kg = q_norm_g[0][None, :], k_norm_g[0][None, :]
    wa2 = jnp.pad(gla_w_a2[0], ((0, LANES - GLA_LOWRANK), (0, 0)))
    ba = gla_b_a[0][None, :]
    gn = gla_norm_g[0][None, :]
    wba, wbb, wo = w_branch_a[0].astype(BF16), w_branch_b[0].astype(BF16), w_out[0].astype(BF16)
    n2 = norm2_g[0][None, :]
    rw = jnp.pad(router_w[0], ((0, 0), (0, LANES - n_exp)))
    rwh = rw.astype(BF16)
    rwl = (rw - rwh.astype(F32)).astype(BF16)
    rb = jnp.concatenate([router_b[0], jnp.full((LANES - n_exp,), -jnp.inf, F32)])[None, :]

    xp = x_prompt.reshape(tp, d)
    proj_p, alr_p = _norm_matmul(xp, g1, w_main, w_side, tn)
    tr = _largest_tile(s, 512, 8)
    tabs_p = _rope_tables(jnp.arange(s, dtype=jnp.int32))
    q_p, k_p, v_p = _qk_rope(proj_p, qg, kg, tabs_p, n_heads, tr, lambda i: i % (s // tr))
    o_a_p = _moba_prompt(q_p.reshape(b, s, wa), k_p.reshape(b, s, wa), v_p.reshape(b, s, wa), n_heads)
    o_b_p, gla_p = _gla_prompt(proj_p.reshape(b, s, n_main), alr_p.reshape(b, s, LANES), wa2, ba, gn, cols)
    x1_p, h2_p, lg_p = _merge(o_a_p.reshape(tp, wa), o_b_p.reshape(tp, vb_w), proj_p, xp, wba, wbb, wo, n2,
                              rwh, rwl, rb, cols, _largest_tile(tp, 128, 8))

    xs_ = x_sample.reshape(ts, d)
    past = page_table.shape[1] * cache_k.shape[2]
    proj_s, alr_s = _norm_matmul(xs_, g1, w_main, w_side, tn)
    tabs_s = tuple(jnp.broadcast_to(t_, (ts, HEAD_DIM_A)) for t_ in _rope_tables(jnp.full((1,), past, jnp.int32)))
    q_s, k_s, v_s = _qk_rope(proj_s, qg, kg, tabs_s, n_heads, ts, lambda i: i)
    o_a_s = _moba_decode(q_s, k_s, v_s, cache_k[0], cache_v[0], page_table, n_heads)
    o_b_s, gla_s = _gla_decode(proj_s, alr_s, wa2, ba, gn, state_gla[0], cols)
    x1_s, h2_s, lg_s = _merge(o_a_s, o_b_s, proj_s, xs_, wba, wbb, wo, n2, rwh, rwl, rb, cols, ts)

    t_all = tp + ts
    logits = jnp.concatenate([lg_p, lg_s], axis=0)
    idx_o, w_o, rank_o, cnt = _router(logits, _largest_tile(t_all, 512, LANES))
    bm = MOE_ROW_BLOCK
    n_assign = t_all * MOE_TOPK
    n_blocks = -(-n_assign // bm) + n_exp
    counts = cnt[0, :n_exp]
    pcounts = (counts + bm - 1) // bm * bm
    pend = jnp.cumsum(pcounts)
    pstart = pend - pcounts
    expert_ids = jnp.arange(n_exp, dtype=jnp.int32)
    start_of = jnp.sum(jnp.where(idx_o[:, :MOE_TOPK, None] == expert_ids, pstart, 0), axis=-1)
    dest = (start_of + rank_o[:, :MOE_TOPK]).reshape(-1)
    n_used = pend[n_exp - 1] // bm
    blk = jnp.arange(n_blocks, dtype=jnp.int32)
    blk = jnp.minimum(blk, n_used - 1)
    block_e = jnp.sum(pend[None, :] <= (blk * bm)[:, None], axis=1).astype(jnp.int32)
    first = jnp.concatenate([jnp.ones((1,), jnp.int32), (block_e[1:] != block_e[:-1]).astype(jnp.int32)])
    nxt_pos = jnp.sum(block_e[None, :] <= block_e[:, None], axis=1)
    nxt = jnp.where(nxt_pos < n_blocks, block_e[jnp.minimum(nxt_pos, n_blocks - 1)], -1).astype(jnp.int32)
    meta = (block_e, first, nxt, n_used.reshape(1).astype(jnp.int32))

    xs_rows = jnp.zeros((n_blocks * bm, d // 2), jnp.uint32)
    xs_rows = _dispatch(dest[:tp * MOE_TOPK], h2_p, xs_rows, _largest_tile(tp, 512, 8))
    xs_rows = _dispatch(dest[tp * MOE_TOPK:], h2_s, xs_rows, ts)
    f = moe_w_gate.shape[3]
    hid = _expert_up(meta, xs_rows, moe_w_gate[0], moe_w_up[0], moe_b_gate[0][:, None, :],
                     moe_b_up[0][:, None, :], bm, _largest_tile(f, 1024, 2 * LANES))
    ys = _expert_down(meta, hid, moe_w_down[0], moe_b_down[0][:, None, :], bm)

    pw, pg, gw = ple_w[0].astype(BF16), ple_norm_g[0][None, :], ple_gate_w[0].astype(BF16)
    tc = _largest_tile(ts, 128, 8)
    assert tp % tc == 0
    y_p = _combine(dest[:tp * MOE_TOPK], ys, x1_p, w_o[:tp], p_prompt[0].reshape(tp, -1), pw, pg, gw, tc)
    y_s = _combine(dest[tp * MOE_TOPK:], ys, x1_s, w_o[tp:], p_sample[0].reshape(ts, -1), pw, pg, gw, tc)

    hd = (n_heads, HEAD_DIM_A)
    return (y_p.reshape(b, s, d), y_s.reshape(nseq, 1, d),
            k_p.reshape(1, b, s, *hd), v_p.reshape(1, b, s, *hd), gla_p[None],
            k_s.reshape(1, nseq, 1, *hd), v_s.reshape(1, nseq, 1, *hd), gla_s[None])
```

```python
import functools

import jax
import jax.numpy as jnp
from jax import lax
from jax.experimental import pallas as pl
from jax.experimental.pallas import tpu as pltpu

HEAD_DIM_A = 128
MOBA_BLOCK = 256
MOBA_TOPK = 3
ROT_DIM = HEAD_DIM_A // 4
ROPE_THETA = 500000.0
N_HEADS_B = 4
GLA_LOWRANK = 16
GLA_TAU = 16.0
GLA_CHUNK = 64
MOE_TOPK = 4
SWIGLU_LIMIT = 7.0
SWIGLU_ALPHA = 1.702
EPS = 1e-6
NEG_INF = -1e30

LANES = 128
SUBLANES = 8
MOE_ROW_BLOCK = 256
VMEM_LIMIT = 56 * 1024 * 1024

F32 = jnp.float32
BF16 = jnp.bfloat16


def _params(sem):
    return pltpu.CompilerParams(dimension_semantics=sem, vmem_limit_bytes=VMEM_LIMIT)


def _dot(a, b):
    return jnp.dot(a, b, preferred_element_type=F32)


def _dot_nt(a, b):
    return lax.dot_general(a, b, (((1,), (1,)), ((), ())), preferred_element_type=F32)


def _dot_tn(a, b):
    return lax.dot_general(a, b, (((0,), (0,)), ((), ())), preferred_element_type=F32)


def _split2(x):
    hi = x.astype(BF16)
    lo = (x - hi.astype(F32)).astype(BF16)
    return hi, lo


def _rms(x):
    return x * lax.rsqrt(jnp.mean(x * x, axis=-1, keepdims=True) + EPS)


def _col_from_row(row):
    n = row.shape[1]
    r = lax.broadcasted_iota(jnp.int32, (n, n), 0)
    c = lax.broadcasted_iota(jnp.int32, (n, n), 1)
    return jnp.sum(jnp.where(r == c, jnp.broadcast_to(row, (n, n)), 0.0), axis=1, keepdims=True)


def _log_sigmoid(z):
    return jnp.minimum(z, 0.0) - jnp.log(1.0 + jnp.exp(-jnp.abs(z)))


def _pack_bf16_pairs(x16):
    half = x16.shape[1] // 2
    bits = lax.bitcast_convert_type(x16.astype(F32), jnp.uint32)
    return (bits[:, :half] >> 16) | bits[:, half:]


def _unpack_bf16_pairs(u):
    lo = lax.bitcast_convert_type(u << 16, F32).astype(BF16)
    hi = lax.bitcast_convert_type(u & jnp.uint32(0xFFFF0000), F32).astype(BF16)
    return jnp.concatenate([lo, hi], axis=1)


def _norm_matmul_kernel(x_ref, g_ref, w_ref, ws_ref, o_ref, os_ref, h_ref):
    @pl.when(pl.program_id(1) == 0)
    def _():
        h_ref[...] = (_rms(x_ref[...]) * g_ref[...]).astype(BF16)
        os_ref[...] = _dot(h_ref[...], ws_ref[...])

    o_ref[...] = _dot(h_ref[...], w_ref[...])


def _norm_matmul(x, g, w, w_side, tn):
    m, d = x.shape
    n = w.shape[1]
    ns = w_side.shape[1]
    tm = _largest_tile(m, 1024, SUBLANES)
    return pl.pallas_call(
        _norm_matmul_kernel,
        grid=(m // tm, n // tn),
        in_specs=[pl.BlockSpec((tm, d), lambda i, j: (i, 0)),
                  pl.BlockSpec((1, d), lambda i, j: (0, 0)),
                  pl.BlockSpec((d, tn), lambda i, j: (0, j)),
                  pl.BlockSpec((d, ns), lambda i, j: (0, 0))],
        out_specs=[pl.BlockSpec((tm, tn), lambda i, j: (i, j)), pl.BlockSpec((tm, ns), lambda i, j: (i, 0))],
        out_shape=[jax.ShapeDtypeStruct((m, n), F32), jax.ShapeDtypeStruct((m, ns), F32)],
        scratch_shapes=[pltpu.VMEM((tm, d), BF16)],
        compiler_params=_params(("parallel", "arbitrary")),
        name="in_proj",
    )(x, g, w, w_side)


def _qk_rope_kernel(q_ref, k_ref, v_ref, qg_ref, kg_ref, c_ref, sn_ref, sp_ref, qo_ref, ko_ref, vo_ref):
    c, sn, sp = c_ref[...], sn_ref[...], sp_ref[...]
    half = ROT_DIM // 2

    def norm_rope(x, g):
        y = _rms(x) * g
        up = pltpu.roll(y, LANES - half, 1)
        dn = pltpu.roll(y, half, 1)
        return y * c + up * sn + dn * sp

    qg, kg = qg_ref[...], kg_ref[...]
    for h in range(q_ref.shape[1] // HEAD_DIM_A):
        hs = slice(h * HEAD_DIM_A, (h + 1) * HEAD_DIM_A)
        qo_ref[:, hs] = norm_rope(q_ref[:, hs], qg)
        ko_ref[:, hs] = norm_rope(k_ref[:, hs], kg)
    vo_ref[...] = v_ref[...]


def _qk_rope(proj, qg, kg, tabs, n_heads, tr, tab_index):
    t = proj.shape[0]
    w = n_heads * HEAD_DIM_A
    col = lambda j: pl.BlockSpec((tr, w), lambda i: (i, j))
    vec = pl.BlockSpec((1, HEAD_DIM_A), lambda i: (0, 0))
    tab = pl.BlockSpec((tr, HEAD_DIM_A), lambda i: (tab_index(i), 0))
    return pl.pallas_call(
        _qk_rope_kernel,
        grid=(t // tr,),
        in_specs=[col(0), col(1), col(2), vec, vec, tab, tab, tab],
        out_specs=[col(0), col(0), col(0)],
        out_shape=[jax.ShapeDtypeStruct((t, w), F32)] * 3,
        compiler_params=_params(("parallel",)),
        name="qk_rope",
    )(proj, proj, proj, qg, kg, *tabs)


def _rope_tables(pos):
    half = ROT_DIM // 2
    inv = ROPE_THETA ** (-jnp.arange(half, dtype=F32) * 2.0 / ROT_DIM)
    ang = pos.astype(F32)[:, None] * inv[None, :]
    cos, sin = jnp.cos(ang), jnp.sin(ang)
    n = pos.shape[0]
    rest = HEAD_DIM_A - ROT_DIM
    c = jnp.concatenate([cos, cos, jnp.ones((n, rest), F32)], axis=1)
    sn = jnp.concatenate([-sin, jnp.zeros((n, half + rest), F32)], axis=1)
    sp = jnp.concatenate([jnp.zeros((n, half), F32), sin, jnp.zeros((n, rest), F32)], axis=1)
    return c, sn, sp


def _block_select(gate, n_cand, n_sel, max_cand):
    lane = lax.broadcasted_iota(jnp.int32, gate.shape, 1)
    gate = jnp.where(lane < n_cand, gate, -jnp.inf)
    rank = jnp.zeros(gate.shape, F32)
    for m in range(max_cand):
        gm = jnp.broadcast_to(gate[:, m:m + 1], gate.shape)
        beats = (gm > gate) | ((gm == gate) & (lane > m))
        rank = rank + jnp.where(beats, 1.0, 0.0)
    return jnp.where((rank < n_sel) & (lane < n_cand), 1.0, 0.0)


def _block_select_rows(gate_t, n_cand, n_sel):
    blk_id = lax.broadcasted_iota(jnp.int32, gate_t.shape, 0)
    gate_t = jnp.where(blk_id < n_cand, gate_t, -jnp.inf)
    rank = jnp.zeros(gate_t.shape, F32)
    for m in range(n_cand):
        gm = jnp.broadcast_to(gate_t[m:m + 1, :], gate_t.shape)
        beats = (gm > gate_t) | ((gm == gate_t) & (blk_id > m))
        rank = rank + jnp.where(beats, 1.0, 0.0)
    return jnp.where((rank < n_sel) & (blk_id < n_cand), 1.0, 0.0)


def _moba_prompt_kernel(q_ref, k_ref, v_ref, o_ref, kaug_s, v16_s, *, nb, n_sel):
    blk, d = MOBA_BLOCK, HEAD_DIM_A
    s_len = nb * blk
    kaug_s[:, :d] = k_ref[...].astype(BF16)
    key_blk = lax.broadcasted_iota(jnp.int32, (s_len, LANES), 0) // blk
    kaug_s[:, d:] = (key_blk == lax.broadcasted_iota(jnp.int32, (s_len, LANES), 1)).astype(BF16)
    v16_s[...] = v_ref[...].astype(BF16)
    nb_pad = -(-nb // 16) * 16
    means = [jnp.mean(k_ref[n * blk:(n + 1) * blk, :], axis=0, keepdims=True) for n in range(nb)]
    kmean = jnp.concatenate(means + [jnp.zeros((nb_pad - nb, d), F32)], axis=0)
    kh, kl = _split2(kmean)
    blk_id = lax.broadcasted_iota(jnp.int32, (nb_pad, blk), 0)
    to_lanes = (lax.broadcasted_iota(jnp.int32, (nb_pad, LANES), 0)
                == lax.broadcasted_iota(jnp.int32, (nb_pad, LANES), 1)).astype(BF16)
    row = lax.broadcasted_iota(jnp.int32, (blk, blk), 0)
    col = lax.broadcasted_iota(jnp.int32, (blk, blk), 1)
    scale = d ** -0.5
    for qb in range(nb):
        q = q_ref[qb * blk:(qb + 1) * blk, :]
        keep_t = jnp.where(blk_id == qb, 1.0, 0.0)
        if qb > 0:
            qh, ql = _split2(q)
            gate_t = _dot_nt(kh, qh) + _dot_nt(kl, qh) + _dot_nt(kh, ql)
            keep_t = keep_t + _block_select_rows(gate_t, qb, n_sel)
        keep = _dot_tn(keep_t.astype(BF16), to_lanes)
        bias = ((1.0 - keep) * NEG_INF).astype(BF16)
        nk = (qb + 1) * blk
        s = _dot_nt(jnp.concatenate([(q * scale).astype(BF16), bias], axis=1), kaug_s[:nk, :])
        s_own = jnp.where(col <= row, s[:, qb * blk:], NEG_INF)
        s = s_own if qb == 0 else jnp.concatenate([s[:, :qb * blk], s_own], axis=1)
        m = jnp.max(s, axis=-1, keepdims=True)
        p = jnp.exp(s - m)
        l = jnp.sum(p, axis=-1, keepdims=True)
        o_ref[qb * blk:(qb + 1) * blk, :] = _dot(p.astype(BF16), v16_s[:nk, :]) / l


def _moba_prompt(q, k, v, n_heads):
    b, s, _ = q.shape
    blk = MOBA_BLOCK
    assert s % blk == 0 and s // blk <= LANES
    nb = s // blk
    n_sel = max(1, min(MOBA_TOPK, nb - 1))
    d = HEAD_DIM_A
    spec = pl.BlockSpec((None, s, d), lambda bi, h: (bi, 0, h))
    return pl.pallas_call(
        functools.partial(_moba_prompt_kernel, nb=nb, n_sel=n_sel),
        grid=(b, n_heads),
        in_specs=[spec, spec, spec],
        out_specs=spec,
        out_shape=jax.ShapeDtypeStruct(q.shape, F32),
        scratch_shapes=[pltpu.VMEM((s, d + LANES), BF16), pltpu.VMEM((s, d), BF16)],
        compiler_params=_params(("parallel", "parallel")),
        name="moba_prompt",
    )(q, k, v)


def _moba_decode_kernel(pt_ref, q_ref, kn_ref, vn_ref, *refs, n_pages, page, n_sel):
    k_refs = refs[:n_pages]
    v_refs = refs[n_pages:2 * n_pages]
    o_ref = refs[2 * n_pages]
    ppb = MOBA_BLOCK // page
    nb = n_pages // ppb
    q = q_ref[...]
    scale = HEAD_DIM_A ** -0.5

    lane = lax.broadcasted_iota(jnp.int32, (q.shape[0], LANES), 1)
    gate = jnp.zeros((q.shape[0], LANES), F32)
    for n in range(nb):
        ksum = jnp.zeros(q.shape, F32)
        for j in range(ppb):
            ksum = ksum + jnp.sum(k_refs[n * ppb + j][...], axis=0)
        g = jnp.sum(q * (ksum * (1.0 / MOBA_BLOCK)), axis=-1, keepdims=True)
        gate = jnp.where(lane == n, g, gate)
    sel = _block_select(gate, nb, n_sel, nb)

    n_h, d = q.shape
    qs = q * scale
    ones = jnp.ones((d, LANES), BF16)
    m = jnp.broadcast_to(jnp.sum(qs * kn_ref[...], axis=-1, keepdims=True), (n_h, LANES))
    l = jnp.ones_like(m)
    acc = vn_ref[...]
    for pg in range(n_pages):
        n = pg // ppb
        keep = jnp.broadcast_to(sel[:, n:n + 1], (n_h, LANES)) > 0.5
        prod = (k_refs[pg][...] * qs[None]).reshape(page * n_h, d).astype(BF16)
        s = _dot(prod, ones).reshape(page, n_h, LANES)
        s = jnp.where(keep[None], s, NEG_INF)
        m_new = jnp.maximum(m, jnp.max(s, axis=0))
        alpha = jnp.exp(m - m_new)
        p = jnp.exp(s - m_new[None])
        l = alpha * l + jnp.sum(p, axis=0)
        acc = alpha * acc + jnp.sum(p * v_refs[pg][...], axis=0)
        m = m_new
    o_ref[...] = acc / l


def _moba_decode(q, k_new, v_new, cache_k, cache_v, page_table, n_heads):
    nseq, n_pages = page_table.shape
    page = cache_k.shape[1]
    d = HEAD_DIM_A
    nb_all = -(-(n_pages * page + 1) // MOBA_BLOCK)
    n_sel = max(1, min(MOBA_TOPK, nb_all - 1))
    assert MOBA_BLOCK % page == 0 and (n_pages * page) % MOBA_BLOCK == 0
    hd = pl.BlockSpec((None, n_heads, d), lambda b, pt: (b, 0, 0))

    def page_spec(pg):
        return pl.BlockSpec((None, page, n_heads, d), lambda b, pt: (pt[b * n_pages + pg], 0, 0, 0))

    specs = [page_spec(pg) for pg in range(n_pages)]
    q3 = q.reshape(nseq, n_heads, d)
    return pl.pallas_call(
        functools.partial(_moba_decode_kernel, n_pages=n_pages, page=page, n_sel=n_sel),
        grid_spec=pltpu.PrefetchScalarGridSpec(
            num_scalar_prefetch=1,
            grid=(nseq,),
            in_specs=[hd, hd, hd] + specs + specs,
            out_specs=hd),
        out_shape=jax.ShapeDtypeStruct((nseq, n_heads, d), F32),
        compiler_params=_params(("parallel",)),
        name="moba_decode",
    )(page_table.reshape(-1), q3, k_new.reshape(nseq, n_heads, d), v_new.reshape(nseq, n_heads, d),
      *([cache_k] * n_pages), *([cache_v] * n_pages)).reshape(nseq, n_heads * d)


GLA_UNROLL = 16


def _gla_out(o, gn, g):
    return _rms(o) * gn * (g * jax.nn.sigmoid(g))


def _gla_prompt_kernel(q_ref, k_ref, v_ref, g_ref, alr_ref, wa2_ref, ba_ref, gn_ref, o_ref, sfin_ref, la_s,
                       *, n_chunks):
    c = GLA_CHUNK
    dk = q_ref.shape[1]
    z = _dot(alr_ref[...].astype(BF16), wa2_ref[...].astype(BF16)) + ba_ref[...]
    la_s[...] = _log_sigmoid(z) * (1.0 / GLA_TAU)
    row = lax.broadcasted_iota(jnp.int32, (c, c), 0)
    col = lax.broadcasted_iota(jnp.int32, (c, c), 1)
    causal = col <= row
    tril = causal.astype(BF16)
    gn = gn_ref[...]

    def body(i, st):
        r0 = pl.multiple_of(i * c, c)
        qc = q_ref[pl.ds(r0, c), :] * (dk ** -0.5)
        kc = k_ref[pl.ds(r0, c), :]
        vc = v_ref[pl.ds(r0, c), :].astype(BF16)
        ac = la_s[pl.ds(r0, c), :]
        a1 = ac.astype(BF16)
        r1 = ac - a1.astype(F32)
        a2 = r1.astype(BF16)
        a3 = (r1 - a2.astype(F32)).astype(BF16)
        b = _dot(tril, a1) + _dot(tril, a2) + _dot(tril, a3)
        q_t = (qc * jnp.exp(b)).astype(BF16)
        k_t = (kc * jnp.exp(-b)).astype(BF16)
        att = jnp.where(causal, _dot_nt(q_t, k_t), 0.0)
        b_last = b[c - 1:c, :]
        k_dec = (kc * jnp.exp(b_last - b)).astype(BF16)
        o_intra = _dot(att.astype(BF16), vc)
        st_in = _dot_tn(k_dec, vc)
        decay = _col_from_row(jnp.exp(b_last))
        o = o_intra + _dot(q_t, st.astype(BF16))
        o_ref[pl.ds(r0, c), :] = _gla_out(o, gn, g_ref[pl.ds(r0, c), :])
        return decay * st + st_in

    sfin_ref[...] = lax.fori_loop(0, n_chunks, body, jnp.zeros(sfin_ref.shape, F32), unroll=GLA_UNROLL)


def _gla_prompt(proj3, alr3, wa2, ba, gn, cols):
    b, s, _ = proj3.shape
    dk, dv = LANES, 2 * LANES
    h = N_HEADS_B
    assert s % GLA_CHUNK == 0
    sk = lambda off: pl.BlockSpec((None, s, dk), lambda bi, hi: (bi, 0, off + hi))
    sv = lambda off: pl.BlockSpec((None, s, dv), lambda bi, hi: (bi, 0, off + hi))
    return pl.pallas_call(
        functools.partial(_gla_prompt_kernel, n_chunks=s // GLA_CHUNK),
        grid=(b, h),
        in_specs=[sk(cols["q_b"] // dk), sk(cols["k_b"] // dk), sv(cols["v_b"] // dv), sv(cols["g_b"] // dv),
                  pl.BlockSpec((None, s, LANES), lambda bi, hi: (bi, 0, 0)),
                  pl.BlockSpec((LANES, dk), lambda bi, hi: (0, hi)),
                  pl.BlockSpec((1, dk), lambda bi, hi: (0, hi)),
                  pl.BlockSpec((1, dv), lambda bi, hi: (0, 0))],
        out_specs=[pl.BlockSpec((None, s, dv), lambda bi, hi: (bi, 0, hi)),
                   pl.BlockSpec((None, None, dk, dv), lambda bi, hi: (bi, hi, 0, 0))],
        out_shape=[jax.ShapeDtypeStruct((b, s, h * dv), F32), jax.ShapeDtypeStruct((b, h, dk, dv), F32)],
        scratch_shapes=[pltpu.VMEM((s, dk), F32)],
        compiler_params=_params(("parallel", "parallel")),
        name="gla_prompt",
    )(proj3, proj3, proj3, proj3, alr3, wa2, ba, gn)


def _gla_decode_kernel(q_ref, k_ref, v_ref, g_ref, alr_ref, wa2_ref, ba_ref, gn_ref, s0_ref, o_ref, s_ref):
    dk, dv = LANES, 2 * LANES
    wa2 = wa2_ref[...].astype(BF16)
    gn = gn_ref[...]

    def one_sequence(s, carry):
        alr = jnp.broadcast_to(alr_ref[s], (SUBLANES, LANES)).astype(BF16)
        z = _dot(alr, wa2)[0:1, :] + ba_ref[...]
        a = jnp.exp(_log_sigmoid(z) * (1.0 / GLA_TAU))
        q, k, v, g = q_ref[s], k_ref[s], v_ref[s], g_ref[s]
        outs = []
        for h in range(N_HEADS_B):
            ks = slice(h * dk, (h + 1) * dk)
            vs = slice(h * dv, (h + 1) * dv)
            a_col = _col_from_row(a[:, ks])
            k_col = _col_from_row(k[:, ks])
            q_col = _col_from_row(q[:, ks] * (dk ** -0.5))
            s_new = a_col * s0_ref[s, h] + k_col * v[:, vs]
            s_ref[s, h] = s_new
            o = jnp.sum(q_col * s_new, axis=0, keepdims=True)
            outs.append(_gla_out(o, gn, g[:, vs]))
        o_ref[s] = jnp.concatenate(outs, axis=1)
        return carry

    lax.fori_loop(0, q_ref.shape[0], one_sequence, 0)


def _gla_decode(proj_s, alr_s, wa2, ba, gn, state, cols):
    t, n = proj_s.shape
    dk, dv = LANES, 2 * LANES
    h = N_HEADS_B
    p3 = proj_s.reshape(t, 1, n)
    a3 = alr_s.reshape(t, 1, LANES)
    sb = _largest_tile(t, SUBLANES, 1)
    blk = lambda w, off: pl.BlockSpec((sb, 1, w), lambda i: (i, 0, off // w))
    full = lambda a: pl.BlockSpec(a.shape, lambda i: (0,) * a.ndim)
    st = pl.BlockSpec((sb, h, dk, dv), lambda i: (i, 0, 0, 0))
    o, s_new = pl.pallas_call(
        _gla_decode_kernel,
        grid=(t // sb,),
        in_specs=[blk(h * dk, cols["q_b"]), blk(h * dk, cols["k_b"]), blk(h * dv, cols["v_b"]),
                  blk(h * dv, cols["g_b"]), blk(LANES, 0), full(wa2), full(ba), full(gn), st],
        out_specs=[pl.BlockSpec((sb, 1, h * dv), lambda i: (i, 0, 0)), st],
        out_shape=[jax.ShapeDtypeStruct((t, 1, h * dv), F32), jax.ShapeDtypeStruct(state.shape, F32)],
        compiler_params=_params(("parallel",)),
        name="gla_decode",
    )(p3, p3, p3, p3, a3, wa2, ba, gn, state)
    return o.reshape(t, h * dv), s_new


def _merge_kernel(oa_ref, ob_ref, ga_ref, gb_ref, x_ref, wa_ref, wb_ref, wo_ref, n2_ref, rwh_ref, rwl_ref, rb_ref,
                  x1_ref, h2_ref, lg_ref):
    ya = _dot(oa_ref[...].astype(BF16), wa_ref[...])
    yb = _dot(ob_ref[...].astype(BF16), wb_ref[...])
    merged = jax.nn.sigmoid(ga_ref[...]) * ya + jax.nn.sigmoid(gb_ref[...]) * yb
    x1 = x_ref[...] + _dot(merged.astype(BF16), wo_ref[...])
    x1_ref[...] = x1
    h2 = _rms(x1) * n2_ref[...]
    hh, hl = _split2(h2)
    lg_ref[...] = _dot(hh, rwh_ref[...]) + _dot(hl, rwh_ref[...]) + _dot(hh, rwl_ref[...]) + rb_ref[...]
    h2_ref[...] = _pack_bf16_pairs(hh)


def _merge(o_a, o_b, proj, x, wa, wb, wo, n2, rwh, rwl, rb, cols, tm):
    t, d = x.shape
    row = lambda w, off: pl.BlockSpec((tm, w), lambda i: (i, off // w))
    full = lambda a: pl.BlockSpec(a.shape, lambda i: (0,) * a.ndim)
    return pl.pallas_call(
        _merge_kernel,
        grid=(t // tm,),
        in_specs=[row(o_a.shape[1], 0), row(o_b.shape[1], 0), row(d, cols["gate_a"]), row(d, cols["gate_b"]),
                  row(d, 0), full(wa), full(wb), full(wo), full(n2), full(rwh), full(rwl), full(rb)],
        out_specs=[row(d, 0), row(d // 2, 0), row(LANES, 0)],
        out_shape=[jax.ShapeDtypeStruct((t, d), F32), jax.ShapeDtypeStruct((t, d // 2), jnp.uint32),
                   jax.ShapeDtypeStruct((t, LANES), F32)],
        compiler_params=_params(("parallel",)),
        name="merge_out_proj",
    )(o_a, o_b, proj, proj, x, wa, wb, wo, n2, rwh, rwl, rb)


def _router_kernel(lg_ref, idx_ref, w_ref, rank_ref, cnt_ref, carry_s):
    @pl.when(pl.program_id(0) == 0)
    def _():
        carry_s[...] = jnp.zeros(carry_s.shape, F32)

    l = lg_ref[...]
    tm = l.shape[0]
    lane = lax.broadcasted_iota(jnp.int32, l.shape, 1).astype(F32)
    vals, idxs = [], []
    for _ in range(MOE_TOPK):
        mx = jnp.max(l, axis=-1, keepdims=True)
        ix = jnp.min(jnp.where(l == mx, lane, float(LANES)), axis=-1, keepdims=True)
        vals.append(mx)
        idxs.append(ix)
        l = jnp.where(lane == ix, -jnp.inf, l)
    es = [jnp.exp(v - vals[0]) for v in vals]
    tot = es[0]
    for e in es[1:]:
        tot = tot + e
    onehot = jnp.zeros(l.shape, F32)
    for ix in idxs:
        onehot = onehot + jnp.where(lane == ix, 1.0, 0.0)
    r = lax.broadcasted_iota(jnp.int32, (tm, tm), 0)
    c = lax.broadcasted_iota(jnp.int32, (tm, tm), 1)
    before = _dot((c < r).astype(BF16), onehot.astype(BF16)) + carry_s[...]
    idx_o = jnp.zeros(l.shape, F32)
    w_o = jnp.zeros(l.shape, F32)
    rank_o = jnp.zeros(l.shape, F32)
    for k in range(MOE_TOPK):
        rk = jnp.sum(jnp.where(lane == idxs[k], before, 0.0), axis=-1, keepdims=True)
        idx_o = jnp.where(lane == k, idxs[k], idx_o)
        w_o = jnp.where(lane == k, es[k] / tot, w_o)
        rank_o = jnp.where(lane == k, rk, rank_o)
    idx_ref[...] = idx_o.astype(jnp.int32)
    w_ref[...] = w_o
    rank_ref[...] = rank_o.astype(jnp.int32)
    carry_s[...] = carry_s[...] + jnp.sum(onehot, axis=0, keepdims=True)
    cnt_ref[...] = carry_s[...].astype(jnp.int32)


def _router(logits, tm):
    t = logits.shape[0]
    row = pl.BlockSpec((tm, LANES), lambda i: (i, 0))
    return pl.pallas_call(
        _router_kernel,
        grid=(t // tm,),
        in_specs=[row],
        out_specs=[row, row, row, pl.BlockSpec((1, LANES), lambda i: (0, 0))],
        out_shape=[jax.ShapeDtypeStruct((t, LANES), jnp.int32), jax.ShapeDtypeStruct((t, LANES), F32),
                   jax.ShapeDtypeStruct((t, LANES), jnp.int32), jax.ShapeDtypeStruct((1, LANES), jnp.int32)],
        scratch_shapes=[pltpu.VMEM((1, LANES), F32)],
        compiler_params=_params(("arbitrary",)),
        name="router",
    )(logits)


def _scatter_rows(dest_ref, base, x_ref, xs_hbm, sem):
    groups = x_ref.shape[0]
    per_group = SUBLANES * MOE_TOPK

    def issue(g, carry):
        for s in range(SUBLANES):
            for k in range(MOE_TOPK):
                dst = dest_ref[base + g * per_group + s * MOE_TOPK + k]
                pltpu.make_async_copy(x_ref.at[g, pl.ds(s, 1), :], xs_hbm.at[pl.ds(dst, 1), :],
                                      sem).start(priority=k % 2)
        return carry

    lax.fori_loop(0, groups, issue, 0)
    for _ in range(per_group):
        pltpu.make_async_copy(x_ref.at[:, 0, :], xs_hbm.at[pl.ds(0, groups), :], sem).wait()


def _dispatch_kernel(dest_ref, x_ref, xs_in_hbm, xs_hbm, sem):
    del xs_in_hbm
    tile_copies = x_ref.shape[0] * SUBLANES * MOE_TOPK
    _scatter_rows(dest_ref, pl.program_id(0) * tile_copies, x_ref, xs_hbm, sem)


def _dispatch(dest, x_packed, xs, tm):
    t, w = x_packed.shape
    x3 = x_packed.reshape(t // SUBLANES, SUBLANES, w)
    hbm = pl.BlockSpec(memory_space=pl.ANY)
    return pl.pallas_call(
        _dispatch_kernel,
        grid_spec=pltpu.PrefetchScalarGridSpec(
            num_scalar_prefetch=1,
            grid=(t // tm,),
            in_specs=[pl.BlockSpec((tm // SUBLANES, SUBLANES, w), lambda i, dst: (i, 0, 0)), hbm],
            out_specs=hbm,
            scratch_shapes=[pltpu.SemaphoreType.DMA(())]),
        out_shape=jax.ShapeDtypeStruct(xs.shape, xs.dtype),
        input_output_aliases={2: 0},
        compiler_params=_params(("arbitrary",)),
        name="moe_dispatch",
    )(dest, x3, xs)


def _stream_expert_weights(first_ref, nxt_ref, be_ref, w_hbms, stages, casts, sems, tn):
    j, i = pl.program_id(0), pl.program_id(1)
    nj = pl.num_programs(0)

    def copies(e, jj):
        c0 = pl.multiple_of(jj * tn, tn)
        return [pltpu.make_async_copy(w.at[e, :, pl.ds(c0, tn)], st, sems.at[n])
                for n, (w, st) in enumerate(zip(w_hbms, stages))]

    @pl.when(first_ref[i] == 1)
    def _():
        @pl.when((j == 0) & (i == 0))
        def _():
            for c in copies(be_ref[0], 0):
                c.start()

        for c in copies(be_ref[i], j):
            c.wait()
        for st, cs in zip(stages, casts):
            cs[...] = st[...].astype(BF16)
        nxt = nxt_ref[i]

        @pl.when(nxt >= 0)
        def _():
            for c in copies(nxt, j):
                c.start()

        @pl.when((nxt < 0) & (j + 1 < nj))
        def _():
            for c in copies(be_ref[0], j + 1):
                c.start()


def _expert_up_kernel(be_ref, first_ref, nxt_ref, nu_ref, x_ref, wg_hbm, wu_hbm, bg_ref, bu_ref, o_ref,
                      stg_g, stg_u, wg_s, wu_s, sems):
    _stream_expert_weights(first_ref, nxt_ref, be_ref, (wg_hbm, wu_hbm), (stg_g, stg_u), (wg_s, wu_s), sems,
                           o_ref.shape[1])
    used = pl.program_id(1) < nu_ref[0]

    @pl.when(used)
    def _():
        x = _unpack_bf16_pairs(x_ref[...])
        tf = o_ref.shape[1]
        half = tf // 2
        e = be_ref[pl.program_id(1)]
        c0 = pl.multiple_of(pl.program_id(0) * tf, tf)
        bg = bg_ref[e, :, pl.ds(c0, tf)]
        bu = bu_ref[e, :, pl.ds(c0, tf)]
        for cs in (slice(0, half), slice(half, 2 * half)):
            gate = jnp.minimum(_dot(x, wg_s[:, cs]) + bg[:, cs], SWIGLU_LIMIT)
            up = jnp.clip(_dot(x, wu_s[:, cs]) + bu[:, cs], -SWIGLU_LIMIT, SWIGLU_LIMIT)
            o_ref[:, cs] = ((up + 1.0) * (gate * jax.nn.sigmoid(SWIGLU_ALPHA * gate))).astype(BF16)

    @pl.when(jnp.logical_not(used))
    def _():
        o_ref[...] = jnp.zeros(o_ref.shape, o_ref.dtype)


def _expert_down_kernel(be_ref, first_ref, nxt_ref, nu_ref, h_ref, wd_hbm, bd_ref, o_ref, stg, wd_s, sems):
    _stream_expert_weights(first_ref, nxt_ref, be_ref, (wd_hbm,), (stg,), (wd_s,), sems, wd_s.shape[1])
    used = pl.program_id(1) < nu_ref[0]

    @pl.when(used)
    def _():
        bd = bd_ref[be_ref[pl.program_id(1)]]
        o_ref[...] = _pack_bf16_pairs((_dot(h_ref[...], wd_s[...]) + bd).astype(BF16))

    @pl.when(jnp.logical_not(used))
    def _():
        o_ref[...] = jnp.zeros(o_ref.shape, o_ref.dtype)


def _used_block(i, nu):
    return jnp.minimum(i, nu[0] - 1)


def _expert_up(meta, xs, wg, wu, bg, bu, bm, tf):
    p, half = xs.shape
    d, f = wg.shape[1], wg.shape[2]
    bspec = pl.BlockSpec(bg.shape, lambda j, i, be, fi, nx, nu: (0, 0, 0))
    hbm = pl.BlockSpec(memory_space=pl.ANY)
    return pl.pallas_call(
        _expert_up_kernel,
        grid_spec=pltpu.PrefetchScalarGridSpec(
            num_scalar_prefetch=4,
            grid=(f // tf, p // bm),
            in_specs=[pl.BlockSpec((bm, half), lambda j, i, be, fi, nx, nu: (_used_block(i, nu), 0)),
                      hbm, hbm, bspec, bspec],
            out_specs=pl.BlockSpec((bm, tf), lambda j, i, be, fi, nx, nu: (i, j)),
            scratch_shapes=[pltpu.VMEM((d, tf), F32), pltpu.VMEM((d, tf), F32),
                            pltpu.VMEM((d, tf), BF16), pltpu.VMEM((d, tf), BF16),
                            pltpu.SemaphoreType.DMA((2,))]),
        out_shape=jax.ShapeDtypeStruct((p, f), BF16),
        compiler_params=_params(("arbitrary", "arbitrary")),
        name="moe_up",
    )(*meta, xs, wg, wu, bg, bu)


def _expert_down(meta, hid, wd, bd, bm):
    p, f = hid.shape
    d = wd.shape[2]
    return pl.pallas_call(
        _expert_down_kernel,
        grid_spec=pltpu.PrefetchScalarGridSpec(
            num_scalar_prefetch=4,
            grid=(1, p // bm),
            in_specs=[pl.BlockSpec((bm, f), lambda j, i, be, fi, nx, nu: (_used_block(i, nu), 0)),
                      pl.BlockSpec(memory_space=pl.ANY),
                      pl.BlockSpec(bd.shape, lambda j, i, be, fi, nx, nu: (0, 0, 0))],
            out_specs=pl.BlockSpec((bm, d // 2), lambda j, i, be, fi, nx, nu: (i, 0)),
            scratch_shapes=[pltpu.VMEM((f, d), F32), pltpu.VMEM((f, d), BF16), pltpu.SemaphoreType.DMA((1,))]),
        out_shape=jax.ShapeDtypeStruct((p, d // 2), jnp.uint32),
        compiler_params=_params(("arbitrary", "arbitrary")),
        name="moe_down",
    )(*meta, hid, wd, bd)


def _combine_kernel(dest_ref, ys_hbm, x1_ref, w_ref, p_ref, pw_ref, pg_ref, gw_ref, y_ref, buf, sems):
    tm = x1_ref.shape[0]
    groups = tm // SUBLANES
    per_group = SUBLANES * MOE_TOPK
    i = pl.program_id(0)
    slot = i % 2

    def issue_group(tile, sl, g):
        base = tile * tm * MOE_TOPK
        for s in range(SUBLANES):
            for k in range(MOE_TOPK):
                dst = dest_ref[base + g * per_group + s * MOE_TOPK + k]
                pltpu.make_async_copy(ys_hbm.at[pl.ds(dst, 1), :], buf.at[sl, k, g, pl.ds(s, 1), :],
                                      sems.at[sl]).start(priority=k % 2)

    def wait_slot(sl):
        for k in range(MOE_TOPK):
            for s in range(SUBLANES):
                pltpu.make_async_copy(ys_hbm.at[pl.ds(0, groups), :], buf.at[sl, k, :, s, :], sems.at[sl]).wait()

    last = pl.num_programs(0) - 1

    @pl.when(i == 0)
    def _():
        lax.fori_loop(0, groups, lambda g, c: (issue_group(0, 0, g), c)[1], 0)

    nxt_tile = jnp.minimum(i + 1, last)
    d = y_ref.shape[1]
    n_col = d // (2 * LANES)
    tc = d // n_col
    n_phase = 2 + MOE_TOPK + n_col
    issued = [0]

    def issue_share(phase):
        upto = groups * (phase + 1) // n_phase
        for g in range(issued[0], upto):
            issue_group(nxt_tile, 1 - slot, g)
        issued[0] = upto

    issue_share(0)
    e = _rms(_dot(p_ref[...].astype(BF16), pw_ref[...])) * pg_ref[...]
    issue_share(1)
    wait_slot(slot)
    w = w_ref[...]
    x2 = x1_ref[...]
    for k in range(MOE_TOPK):
        u = buf[slot, k].reshape(tm, buf.shape[-1])
        rows = jnp.concatenate([lax.bitcast_convert_type(u << 16, F32),
                                lax.bitcast_convert_type(u & jnp.uint32(0xFFFF0000), F32)], axis=1)
        x2 = x2 + w[:, k:k + 1] * rows
        issue_share(2 + k)
    xn = _rms(x2).astype(BF16)
    for t in range(n_col):
        cs = slice(t * tc, (t + 1) * tc)
        gate = jax.nn.sigmoid(_dot(xn, gw_ref[:, cs]))
        y_ref[:, cs] = x2[:, cs] + gate * e[:, cs]
        issue_share(2 + MOE_TOPK + t)

    @pl.when(i == last)
    def _():
        wait_slot(1 - slot)


def _combine(dest, ys, x1, w, p, pw, pg, gw, tm):
    t, d = x1.shape
    full = lambda a: pl.BlockSpec(a.shape, lambda i, dst: (0,) * a.ndim)
    row = lambda width: pl.BlockSpec((tm, width), lambda i, dst: (i, 0))
    return pl.pallas_call(
        _combine_kernel,
        grid_spec=pltpu.PrefetchScalarGridSpec(
            num_scalar_prefetch=1,
            grid=(t // tm,),
            in_specs=[pl.BlockSpec(memory_space=pl.ANY), row(d), row(LANES), row(p.shape[1]),
                      full(pw), full(pg), full(gw)],
            out_specs=row(d),
            scratch_shapes=[pltpu.VMEM((2, MOE_TOPK, tm // SUBLANES, SUBLANES, d // 2), jnp.uint32),
                            pltpu.SemaphoreType.DMA((2,))]),
        out_shape=jax.ShapeDtypeStruct((t, d), F32),
        compiler_params=_params(("arbitrary",)),
        name="moe_combine_ple",
    )(dest, ys, x1, w, p, pw, pg, gw)


def _largest_tile(n, cap, quantum):
    best = None
    for t in range(quantum, cap + 1, quantum):
        if n % t == 0:
            best = t
    assert best is not None, (n, cap, quantum)
    return best


def kernel(x_prompt, x_sample, cache_k, cache_v, state_gla, page_table, p_prompt, p_sample, norm1_g, w_in, q_norm_g, k_norm_g, gla_w_a2, gla_b_a, gla_norm_g, w_branch_a, w_branch_b, w_out, norm2_g, router_w, router_b, moe_w_gate, moe_b_gate, moe_w_up, moe_b_up, moe_w_down, moe_b_down, ple_w, ple_norm_g, ple_gate_w):
    assert norm1_g.shape[0] == 1, "one layer"
    b, s, d = x_prompt.shape
    nseq = x_sample.shape[0]
    assert x_sample.shape[1] == 1
    n_heads = d // (2 * HEAD_DIM_A)
    wa = n_heads * HEAD_DIM_A
    kb_w = N_HEADS_B * LANES
    vb_w = 2 * kb_w
    n_exp = router_w.shape[2]
    tp, ts = b * s, nseq

    sizes = [("q_a", wa), ("k_a", wa), ("v_a", wa), ("q_b", kb_w), ("k_b", kb_w), ("v_b", vb_w), ("g_b", vb_w),
             ("a_lr", GLA_LOWRANK), ("gate_a", d), ("gate_b", d)]
    cols, src, off = {}, 0, 0
    w_in0 = w_in[0]
    for name, width in sizes:
        if name == "a_lr":
            w_side = jnp.pad(w_in0[:, src:src + width], ((0, 0), (0, LANES - width))).astype(BF16)
            split = src
        else:
            cols[name] = off
            off += width
        src += width
    n_main = off
    tn = _largest_tile(n_main, 1024, 2 * LANES)
    w_main = jnp.concatenate([w_in0[:, :split], w_in0[:, split + GLA_LOWRANK:]], axis=1).astype(BF16)

    g1 = norm1_g[0][None, :]
    qg, kg = q_norm_g[0][None, :], k_norm_g[0][None, :]
    wa2 = jnp.pad(gla_w_a2[0], ((0, LANES - GLA_LOWRANK), (0, 0)))
    ba = gla_b_a[0][None, :]
    gn = gla_norm_g[0][None, :]
    wba, wbb, wo = w_branch_a[0].astype(BF16), w_branch_b[0].astype(BF16), w_out[0].astype(BF16)
    n2 = norm2_g[0][None, :]
    rw = jnp.pad(router_w[0], ((0, 0), (0, LANES - n_exp)))
    rwh = rw.astype(BF16)
    rwl = (rw - rwh.astype(F32)).astype(BF16)
    rb = jnp.concatenate([router_b[0], jnp.full((LANES - n_exp,), -jnp.inf, F32)])[None, :]

    xp = x_prompt.reshape(tp, d)
    proj_p, alr_p = _norm_matmul(xp, g1, w_main, w_side, tn)
    tr = _largest_tile(s, 512, 8)
    tabs_p = _rope_tables(jnp.arange(s, dtype=jnp.int32))
    q_p, k_p, v_p = _qk_rope(proj_p, qg, kg, tabs_p, n_heads, tr, lambda i: i % (s // tr))
    o_a_p = _moba_prompt(q_p.reshape(b, s, wa), k_p.reshape(b, s, wa), v_p.reshape(b, s, wa), n_heads)
    o_b_p, gla_p = _gla_prompt(proj_p.reshape(b, s, n_main), alr_p.reshape(b, s, LANES), wa2, ba, gn, cols)
    x1_p, h2_p, lg_p = _merge(o_a_p.reshape(tp, wa), o_b_p.reshape(tp, vb_w), proj_p, xp, wba, wbb, wo, n2,
                              rwh, rwl, rb, cols, _largest_tile(tp, 128, 8))

    xs_ = x_sample.reshape(ts, d)
    past = page_table.shape[1] * cache_k.shape[2]
    proj_s, alr_s = _norm_matmul(xs_, g1, w_main, w_side, tn)
    tabs_s = tuple(jnp.broadcast_to(t_, (ts, HEAD_DIM_A)) for t_ in _rope_tables(jnp.full((1,), past, jnp.int32)))
    q_s, k_s, v_s = _qk_rope(proj_s, qg, kg, tabs_s, n_heads, ts, lambda i: i)
    o_a_s = _moba_decode(q_s, k_s, v_s, cache_k[0], cache_v[0], page_table, n_heads)
    o_b_s, gla_s = _gla_decode(proj_s, alr_s, wa2, ba, gn, state_gla[0], cols)
    x1_s, h2_s, lg_s = _merge(o_a_s, o_b_s, proj_s, xs_, wba, wbb, wo, n2, rwh, rwl, rb, cols, ts)

    t_all = tp + ts
    logits = jnp.concatenate([lg_p, lg_s], axis=0)
    idx_o, w_o, rank_o, cnt = _router(logits, _largest_tile(t_all, 512, LANES))
    bm = MOE_ROW_BLOCK
    n_assign = t_all * MOE_TOPK
    n_blocks = -(-n_assign // bm) + n_exp
    counts = cnt[0, :n_exp]
    pcounts = (counts + bm - 1) // bm * bm
    pend = jnp.cumsum(pcounts)
    pstart = pend - pcounts
    expert_ids = jnp.arange(n_exp, dtype=jnp.int32)
    start_of = jnp.sum(jnp.where(idx_o[:, :MOE_TOPK, None] == expert_ids, pstart, 0), axis=-1)
    dest = (start_of + rank_o[:, :MOE_TOPK]).reshape(-1)
    n_used = pend[n_exp - 1] // bm
    blk = jnp.arange(n_blocks, dtype=jnp.int32)
    blk = jnp.minimum(blk, n_used - 1)
    block_e = jnp.sum(pend[None, :] <= (blk * bm)[:, None], axis=1).astype(jnp.int32)
    first = jnp.concatenate([jnp.ones((1,), jnp.int32), (block_e[1:] != block_e[:-1]).astype(jnp.int32)])
    nxt_pos = jnp.sum(block_e[None, :] <= block_e[:, None], axis=1)
    nxt = jnp.where(nxt_pos < n_blocks, block_e[jnp.minimum(nxt_pos, n_blocks - 1)], -1).astype(jnp.int32)
    meta = (block_e, first, nxt, n_used.reshape(1).astype(jnp.int32))

    xs_rows = jnp.zeros((n_blocks * bm, d // 2), jnp.uint32)
    xs_rows = _dispatch(dest[:tp * MOE_TOPK], h2_p, xs_rows, _largest_tile(tp, 512, 8))
    xs_rows = _dispatch(dest[tp * MOE_TOPK:], h2_s, xs_rows, ts)
    f = moe_w_gate.shape[3]
    hid = _expert_up(meta, xs_rows, moe_w_gate[0], moe_w_up[0], moe_b_gate[0][:, None, :],
                     moe_b_up[0][:, None, :], bm, _largest_tile(f, 1024, 2 * LANES))
    ys = _expert_down(meta, hid, moe_w_down[0], moe_b_down[0][:, None, :], bm)

    pw, pg, gw = ple_w[0].astype(BF16), ple_norm_g[0][None, :], ple_gate_w[0].astype(BF16)
    y_p = _combine(dest[:tp * MOE_TOPK], ys, x1_p, w_o[:tp], p_prompt[0].reshape(tp, -1), pw, pg, gw,
                   _largest_tile(tp, 256, 8 * SUBLANES))
    y_s = _combine(dest[tp * MOE_TOPK:], ys, x1_s, w_o[tp:], p_sample[0].reshape(ts, -1), pw, pg, gw,
                   _largest_tile(ts, 128, 8 * SUBLANES))

    hd = (n_heads, HEAD_DIM_A)
    return (y_p.reshape(b, s, d), y_s.reshape(nseq, 1, d),
            k_p.reshape(1, b, s, *hd), v_p.reshape(1, b, s, *hd), gla_p[None],
            k_s.reshape(1, nseq, 1, *hd), v_s.reshape(1, nseq, 1, *hd), gla_s[None])
```

```python
import functools

import jax
import jax.numpy as jnp
from jax import lax
from jax.experimental import pallas as pl
from jax.experimental.pallas import tpu as pltpu

HEAD_DIM_A = 128
MOBA_BLOCK = 256
MOBA_TOPK = 3
ROT_DIM = HEAD_DIM_A // 4
ROPE_THETA = 500000.0
N_HEADS_B = 4
GLA_LOWRANK = 16
GLA_TAU = 16.0
GLA_CHUNK = 64
MOE_TOPK = 4
SWIGLU_LIMIT = 7.0
SWIGLU_ALPHA = 1.702
EPS = 1e-6
NEG_INF = -1e30

LANES = 128
SUBLANES = 8
MOE_ROW_BLOCK = 256
VMEM_LIMIT = 56 * 1024 * 1024

F32 = jnp.float32
BF16 = jnp.bfloat16


def _params(sem):
    return pltpu.CompilerParams(dimension_semantics=sem, vmem_limit_bytes=VMEM_LIMIT)


def _dot(a, b):
    return jnp.dot(a, b, preferred_element_type=F32)


def _dot_nt(a, b):
    return lax.dot_general(a, b, (((1,), (1,)), ((), ())), preferred_element_type=F32)


def _dot_tn(a, b):
    return lax.dot_general(a, b, (((0,), (0,)), ((), ())), preferred_element_type=F32)


def _split2(x):
    hi = x.astype(BF16)
    lo = (x - hi.astype(F32)).astype(BF16)
    return hi, lo


def _rms(x):
    return x * lax.rsqrt(jnp.mean(x * x, axis=-1, keepdims=True) + EPS)


def _col_from_row(row):
    n = row.shape[1]
    r = lax.broadcasted_iota(jnp.int32, (n, n), 0)
    c = lax.broadcasted_iota(jnp.int32, (n, n), 1)
    return jnp.sum(jnp.where(r == c, jnp.broadcast_to(row, (n, n)), 0.0), axis=1, keepdims=True)


def _log_sigmoid(z):
    return jnp.minimum(z, 0.0) - jnp.log(1.0 + jnp.exp(-jnp.abs(z)))


def _pack_bf16_pairs(x16):
    half = x16.shape[1] // 2
    bits = lax.bitcast_convert_type(x16.astype(F32), jnp.uint32)
    return (bits[:, :half] >> 16) | bits[:, half:]


def _unpack_bf16_pairs(u):
    lo = lax.bitcast_convert_type(u << 16, F32).astype(BF16)
    hi = lax.bitcast_convert_type(u & jnp.uint32(0xFFFF0000), F32).astype(BF16)
    return jnp.concatenate([lo, hi], axis=1)


def _norm_matmul_kernel(x_ref, g_ref, w_ref, ws_ref, o_ref, os_ref, h_ref):
    @pl.when(pl.program_id(1) == 0)
    def _():
        h_ref[...] = (_rms(x_ref[...]) * g_ref[...]).astype(BF16)
        os_ref[...] = _dot(h_ref[...], ws_ref[...])

    o_ref[...] = _dot(h_ref[...], w_ref[...])


def _norm_matmul(x, g, w, w_side, tn):
    m, d = x.shape
    n = w.shape[1]
    ns = w_side.shape[1]
    tm = _largest_tile(m, 1024, SUBLANES)
    return pl.pallas_call(
        _norm_matmul_kernel,
        grid=(m // tm, n // tn),
        in_specs=[pl.BlockSpec((tm, d), lambda i, j: (i, 0)),
                  pl.BlockSpec((1, d), lambda i, j: (0, 0)),
                  pl.BlockSpec((d, tn), lambda i, j: (0, j)),
                  pl.BlockSpec((d, ns), lambda i, j: (0, 0))],
        out_specs=[pl.BlockSpec((tm, tn), lambda i, j: (i, j)), pl.BlockSpec((tm, ns), lambda i, j: (i, 0))],
        out_shape=[jax.ShapeDtypeStruct((m, n), F32), jax.ShapeDtypeStruct((m, ns), F32)],
        scratch_shapes=[pltpu.VMEM((tm, d), BF16)],
        compiler_params=_params(("parallel", "arbitrary")),
        name="in_proj",
    )(x, g, w, w_side)


def _qk_rope_kernel(q_ref, k_ref, v_ref, qg_ref, kg_ref, c_ref, sn_ref, sp_ref, qo_ref, ko_ref, vo_ref):
    c, sn, sp = c_ref[...], sn_ref[...], sp_ref[...]
    half = ROT_DIM // 2

    def norm_rope(x, g):
        y = _rms(x) * g
        up = pltpu.roll(y, LANES - half, 1)
        dn = pltpu.roll(y, half, 1)
        return y * c + up * sn + dn * sp

    qg, kg = qg_ref[...], kg_ref[...]
    for h in range(q_ref.shape[1] // HEAD_DIM_A):
        hs = slice(h * HEAD_DIM_A, (h + 1) * HEAD_DIM_A)
        qo_ref[:, hs] = norm_rope(q_ref[:, hs], qg)
        ko_ref[:, hs] = norm_rope(k_ref[:, hs], kg)
    vo_ref[...] = v_ref[...]


def _qk_rope(proj, qg, kg, tabs, n_heads, tr, tab_index):
    t = proj.shape[0]
    w = n_heads * HEAD_DIM_A
    col = lambda j: pl.BlockSpec((tr, w), lambda i: (i, j))
    vec = pl.BlockSpec((1, HEAD_DIM_A), lambda i: (0, 0))
    tab = pl.BlockSpec((tr, HEAD_DIM_A), lambda i: (tab_index(i), 0))
    return pl.pallas_call(
        _qk_rope_kernel,
        grid=(t // tr,),
        in_specs=[col(0), col(1), col(2), vec, vec, tab, tab, tab],
        out_specs=[col(0), col(0), col(0)],
        out_shape=[jax.ShapeDtypeStruct((t, w), F32)] * 3,
        compiler_params=_params(("parallel",)),
        name="qk_rope",
    )(proj, proj, proj, qg, kg, *tabs)


def _rope_tables(pos):
    half = ROT_DIM // 2
    inv = ROPE_THETA ** (-jnp.arange(half, dtype=F32) * 2.0 / ROT_DIM)
    ang = pos.astype(F32)[:, None] * inv[None, :]
    cos, sin = jnp.cos(ang), jnp.sin(ang)
    n = pos.shape[0]
    rest = HEAD_DIM_A - ROT_DIM
    c = jnp.concatenate([cos, cos, jnp.ones((n, rest), F32)], axis=1)
    sn = jnp.concatenate([-sin, jnp.zeros((n, half + rest), F32)], axis=1)
    sp = jnp.concatenate([jnp.zeros((n, half), F32), sin, jnp.zeros((n, rest), F32)], axis=1)
    return c, sn, sp


def _block_select(gate, n_cand, n_sel, max_cand):
    lane = lax.broadcasted_iota(jnp.int32, gate.shape, 1)
    gate = jnp.where(lane < n_cand, gate, -jnp.inf)
    rank = jnp.zeros(gate.shape, F32)
    for m in range(max_cand):
        gm = jnp.broadcast_to(gate[:, m:m + 1], gate.shape)
        beats = (gm > gate) | ((gm == gate) & (lane > m))
        rank = rank + jnp.where(beats, 1.0, 0.0)
    return jnp.where((rank < n_sel) & (lane < n_cand), 1.0, 0.0)


def _block_select_rows(gate_t, n_cand, n_sel):
    blk_id = lax.broadcasted_iota(jnp.int32, gate_t.shape, 0)
    gate_t = jnp.where(blk_id < n_cand, gate_t, -jnp.inf)
    rank = jnp.zeros(gate_t.shape, F32)
    for m in range(n_cand):
        gm = jnp.broadcast_to(gate_t[m:m + 1, :], gate_t.shape)
        beats = (gm > gate_t) | ((gm == gate_t) & (blk_id > m))
        rank = rank + jnp.where(beats, 1.0, 0.0)
    return jnp.where((rank < n_sel) & (blk_id < n_cand), 1.0, 0.0)


def _moba_prompt_kernel(q_ref, k_ref, v_ref, o_ref, kaug_s, v16_s, *, nb, n_sel):
    blk, d = MOBA_BLOCK, HEAD_DIM_A
    s_len = nb * blk
    kaug_s[:, :d] = k_ref[...].astype(BF16)
    key_blk = lax.broadcasted_iota(jnp.int32, (s_len, LANES), 0) // blk
    kaug_s[:, d:] = (key_blk == lax.broadcasted_iota(jnp.int32, (s_len, LANES), 1)).astype(BF16)
    v16_s[...] = v_ref[...].astype(BF16)
    nb_pad = -(-nb // 16) * 16
    means = [jnp.mean(k_ref[n * blk:(n + 1) * blk, :], axis=0, keepdims=True) for n in range(nb)]
    kmean = jnp.concatenate(means + [jnp.zeros((nb_pad - nb, d), F32)], axis=0)
    kh, kl = _split2(kmean)
    blk_id = lax.broadcasted_iota(jnp.int32, (nb_pad, blk), 0)
    to_lanes = (lax.broadcasted_iota(jnp.int32, (nb_pad, LANES), 0)
                == lax.broadcasted_iota(jnp.int32, (nb_pad, LANES), 1)).astype(BF16)
    row = lax.broadcasted_iota(jnp.int32, (blk, blk), 0)
    col = lax.broadcasted_iota(jnp.int32, (blk, blk), 1)
    scale = d ** -0.5
    for qb in range(nb):
        q = q_ref[qb * blk:(qb + 1) * blk, :]
        keep_t = jnp.where(blk_id == qb, 1.0, 0.0)
        if qb > 0:
            qh, ql = _split2(q)
            gate_t = _dot_nt(kh, qh) + _dot_nt(kl, qh) + _dot_nt(kh, ql)
            keep_t = keep_t + _block_select_rows(gate_t, qb, n_sel)
        keep = _dot_tn(keep_t.astype(BF16), to_lanes)
        bias = ((1.0 - keep) * NEG_INF).astype(BF16)
        nk = (qb + 1) * blk
        s = _dot_nt(jnp.concatenate([(q * scale).astype(BF16), bias], axis=1), kaug_s[:nk, :])
        s_own = jnp.where(col <= row, s[:, qb * blk:], NEG_INF)
        s = s_own if qb == 0 else jnp.concatenate([s[:, :qb * blk], s_own], axis=1)
        m = jnp.max(s, axis=-1, keepdims=True)
        p = jnp.exp(s - m)
        l = jnp.sum(p, axis=-1, keepdims=True)
        o_ref[qb * blk:(qb + 1) * blk, :] = _dot(p.astype(BF16), v16_s[:nk, :]) / l


def _moba_prompt(q, k, v, n_heads):
    b, s, _ = q.shape
    blk = MOBA_BLOCK
    assert s % blk == 0 and s // blk <= LANES
    nb = s // blk
    n_sel = max(1, min(MOBA_TOPK, nb - 1))
    d = HEAD_DIM_A
    spec = pl.BlockSpec((None, s, d), lambda bi, h: (bi, 0, h))
    return pl.pallas_call(
        functools.partial(_moba_prompt_kernel, nb=nb, n_sel=n_sel),
        grid=(b, n_heads),
        in_specs=[spec, spec, spec],
        out_specs=spec,
        out_shape=jax.ShapeDtypeStruct(q.shape, F32),
        scratch_shapes=[pltpu.VMEM((s, d + LANES), BF16), pltpu.VMEM((s, d), BF16)],
        compiler_params=_params(("parallel", "parallel")),
        name="moba_prompt",
    )(q, k, v)


def _moba_decode_kernel(pt_ref, q_ref, kn_ref, vn_ref, *refs, n_pages, page, n_sel):
    k_refs = refs[:n_pages]
    v_refs = refs[n_pages:2 * n_pages]
    o_ref = refs[2 * n_pages]
    ppb = MOBA_BLOCK // page
    nb = n_pages // ppb
    q = q_ref[...]
    scale = HEAD_DIM_A ** -0.5

    lane = lax.broadcasted_iota(jnp.int32, (q.shape[0], LANES), 1)
    gate = jnp.zeros((q.shape[0], LANES), F32)
    for n in range(nb):
        ksum = jnp.zeros(q.shape, F32)
        for j in range(ppb):
            ksum = ksum + jnp.sum(k_refs[n * ppb + j][...], axis=0)
        g = jnp.sum(q * (ksum * (1.0 / MOBA_BLOCK)), axis=-1, keepdims=True)
        gate = jnp.where(lane == n, g, gate)
    sel = _block_select(gate, nb, n_sel, nb)

    n_h, d = q.shape
    qs = q * scale
    ones = jnp.ones((d, LANES), BF16)
    m = jnp.broadcast_to(jnp.sum(qs * kn_ref[...], axis=-1, keepdims=True), (n_h, LANES))
    l = jnp.ones_like(m)
    acc = vn_ref[...]
    for pg in range(n_pages):
        n = pg // ppb
        keep = jnp.broadcast_to(sel[:, n:n + 1], (n_h, LANES)) > 0.5
        prod = (k_refs[pg][...] * qs[None]).reshape(page * n_h, d).astype(BF16)
        s = _dot(prod, ones).reshape(page, n_h, LANES)
        s = jnp.where(keep[None], s, NEG_INF)
        m_new = jnp.maximum(m, jnp.max(s, axis=0))
        alpha = jnp.exp(m - m_new)
        p = jnp.exp(s - m_new[None])
        l = alpha * l + jnp.sum(p, axis=0)
        acc = alpha * acc + jnp.sum(p * v_refs[pg][...], axis=0)
        m = m_new
    o_ref[...] = acc / l


def _moba_decode(q, k_new, v_new, cache_k, cache_v, page_table, n_heads):
    nseq, n_pages = page_table.shape
    page = cache_k.shape[1]
    d = HEAD_DIM_A
    nb_all = -(-(n_pages * page + 1) // MOBA_BLOCK)
    n_sel = max(1, min(MOBA_TOPK, nb_all - 1))
    assert MOBA_BLOCK % page == 0 and (n_pages * page) % MOBA_BLOCK == 0
    hd = pl.BlockSpec((None, n_heads, d), lambda b, pt: (b, 0, 0))

    def page_spec(pg):
        return pl.BlockSpec((None, page, n_heads, d), lambda b, pt: (pt[b * n_pages + pg], 0, 0, 0))

    specs = [page_spec(pg) for pg in range(n_pages)]
    q3 = q.reshape(nseq, n_heads, d)
    return pl.pallas_call(
        functools.partial(_moba_decode_kernel, n_pages=n_pages, page=page, n_sel=n_sel),
        grid_spec=pltpu.PrefetchScalarGridSpec(
            num_scalar_prefetch=1,
            grid=(nseq,),
            in_specs=[hd, hd, hd] + specs + specs,
            out_specs=hd),
        out_shape=jax.ShapeDtypeStruct((nseq, n_heads, d), F32),
        compiler_params=_params(("parallel",)),
        name="moba_decode",
    )(page_table.reshape(-1), q3, k_new.reshape(nseq, n_heads, d), v_new.reshape(nseq, n_heads, d),
      *([cache_k] * n_pages), *([cache_v] * n_pages)).reshape(nseq, n_heads * d)


GLA_UNROLL = 16


def _gla_out(o, gn, g):
    return _rms(o) * gn * (g * jax.nn.sigmoid(g))


def _gla_prompt_kernel(q_ref, k_ref, v_ref, g_ref, alr_ref, wa2_ref, ba_ref, gn_ref, o_ref, sfin_ref, la_s,
                       *, n_chunks):
    c = GLA_CHUNK
    dk = q_ref.shape[1]
    z = _dot(alr_ref[...].astype(BF16), wa2_ref[...].astype(BF16)) + ba_ref[...]
    la_s[...] = _log_sigmoid(z) * (1.0 / GLA_TAU)
    row = lax.broadcasted_iota(jnp.int32, (c, c), 0)
    col = lax.broadcasted_iota(jnp.int32, (c, c), 1)
    causal = col <= row
    tril = causal.astype(BF16)
    gn = gn_ref[...]

    def body(i, st):
        r0 = pl.multiple_of(i * c, c)
        qc = q_ref[pl.ds(r0, c), :] * (dk ** -0.5)
        kc = k_ref[pl.ds(r0, c), :]
        vc = v_ref[pl.ds(r0, c), :].astype(BF16)
        ac = la_s[pl.ds(r0, c), :]
        a1 = ac.astype(BF16)
        r1 = ac - a1.astype(F32)
        a2 = r1.astype(BF16)
        a3 = (r1 - a2.astype(F32)).astype(BF16)
        b = _dot(tril, a1) + _dot(tril, a2) + _dot(tril, a3)
        q_t = (qc * jnp.exp(b)).astype(BF16)
        k_t = (kc * jnp.exp(-b)).astype(BF16)
        att = jnp.where(causal, _dot_nt(q_t, k_t), 0.0)
        b_last = b[c - 1:c, :]
        k_dec = (kc * jnp.exp(b_last - b)).astype(BF16)
        o_intra = _dot(att.astype(BF16), vc)
        st_in = _dot_tn(k_dec, vc)
        decay = _col_from_row(jnp.exp(b_last))
        o = o_intra + _dot(q_t, st.astype(BF16))
        o_ref[pl.ds(r0, c), :] = _gla_out(o, gn, g_ref[pl.ds(r0, c), :])
        return decay * st + st_in

    sfin_ref[...] = lax.fori_loop(0, n_chunks, body, jnp.zeros(sfin_ref.shape, F32), unroll=GLA_UNROLL)


def _gla_prompt(proj3, alr3, wa2, ba, gn, cols):
    b, s, _ = proj3.shape
    dk, dv = LANES, 2 * LANES
    h = N_HEADS_B
    assert s % GLA_CHUNK == 0
    sk = lambda off: pl.BlockSpec((None, s, dk), lambda bi, hi: (bi, 0, off + hi))
    sv = lambda off: pl.BlockSpec((None, s, dv), lambda bi, hi: (bi, 0, off + hi))
    return pl.pallas_call(
        functools.partial(_gla_prompt_kernel, n_chunks=s // GLA_CHUNK),
        grid=(b, h),
        in_specs=[sk(cols["q_b"] // dk), sk(cols["k_b"] // dk), sv(cols["v_b"] // dv), sv(cols["g_b"] // dv),
                  pl.BlockSpec((None, s, LANES), lambda bi, hi: (bi, 0, 0)),
                  pl.BlockSpec((LANES, dk), lambda bi, hi: (0, hi)),
                  pl.BlockSpec((1, dk), lambda bi, hi: (0, hi)),
                  pl.BlockSpec((1, dv), lambda bi, hi: (0, 0))],
        out_specs=[pl.BlockSpec((None, s, dv), lambda bi, hi: (bi, 0, hi)),
                   pl.BlockSpec((None, None, dk, dv), lambda bi, hi: (bi, hi, 0, 0))],
        out_shape=[jax.ShapeDtypeStruct((b, s, h * dv), F32), jax.ShapeDtypeStruct((b, h, dk, dv), F32)],
        scratch_shapes=[pltpu.VMEM((s, dk), F32)],
        compiler_params=_params(("parallel", "parallel")),
        name="gla_prompt",
    )(proj3, proj3, proj3, proj3, alr3, wa2, ba, gn)


def _gla_decode_kernel(q_ref, k_ref, v_ref, g_ref, alr_ref, wa2_ref, ba_ref, gn_ref, s0_ref, o_ref, s_ref):
    dk, dv = LANES, 2 * LANES
    wa2 = wa2_ref[...].astype(BF16)
    gn = gn_ref[...]

    def one_sequence(s, carry):
        alr = jnp.broadcast_to(alr_ref[s], (SUBLANES, LANES)).astype(BF16)
        z = _dot(alr, wa2)[0:1, :] + ba_ref[...]
        a = jnp.exp(_log_sigmoid(z) * (1.0 / GLA_TAU))
        q, k, v, g = q_ref[s], k_ref[s], v_ref[s], g_ref[s]
        outs = []
        for h in range(N_HEADS_B):
            ks = slice(h * dk, (h + 1) * dk)
            vs = slice(h * dv, (h + 1) * dv)
            a_col = _col_from_row(a[:, ks])
            k_col = _col_from_row(k[:, ks])
            q_col = _col_from_row(q[:, ks] * (dk ** -0.5))
            s_new = a_col * s0_ref[s, h] + k_col * v[:, vs]
            s_ref[s, h] = s_new
            o = jnp.sum(q_col * s_new, axis=0, keepdims=True)
            outs.append(_gla_out(o, gn, g[:, vs]))
        o_ref[s] = jnp.concatenate(outs, axis=1)
        return carry

    lax.fori_loop(0, q_ref.shape[0], one_sequence, 0)


def _gla_decode(proj_s, alr_s, wa2, ba, gn, state, cols):
    t, n = proj_s.shape
    dk, dv = LANES, 2 * LANES
    h = N_HEADS_B
    p3 = proj_s.reshape(t, 1, n)
    a3 = alr_s.reshape(t, 1, LANES)
    sb = _largest_tile(t, SUBLANES, 1)
    blk = lambda w, off: pl.BlockSpec((sb, 1, w), lambda i: (i, 0, off // w))
    full = lambda a: pl.BlockSpec(a.shape, lambda i: (0,) * a.ndim)
    st = pl.BlockSpec((sb, h, dk, dv), lambda i: (i, 0, 0, 0))
    o, s_new = pl.pallas_call(
        _gla_decode_kernel,
        grid=(t // sb,),
        in_specs=[blk(h * dk, cols["q_b"]), blk(h * dk, cols["k_b"]), blk(h * dv, cols["v_b"]),
                  blk(h * dv, cols["g_b"]), blk(LANES, 0), full(wa2), full(ba), full(gn), st],
        out_specs=[pl.BlockSpec((sb, 1, h * dv), lambda i: (i, 0, 0)), st],
        out_shape=[jax.ShapeDtypeStruct((t, 1, h * dv), F32), jax.ShapeDtypeStruct(state.shape, F32)],
        compiler_params=_params(("parallel",)),
        name="gla_decode",
    )(p3, p3, p3, p3, a3, wa2, ba, gn, state)
    return o.reshape(t, h * dv), s_new


def _merge_kernel(oa_ref, ob_ref, ga_ref, gb_ref, x_ref, wa_ref, wb_ref, wo_ref, n2_ref, rwh_ref, rwl_ref, rb_ref,
                  x1_ref, h2_ref, lg_ref):
    ya = _dot(oa_ref[...].astype(BF16), wa_ref[...])
    yb = _dot(ob_ref[...].astype(BF16), wb_ref[...])
    merged = jax.nn.sigmoid(ga_ref[...]) * ya + jax.nn.sigmoid(gb_ref[...]) * yb
    x1 = x_ref[...] + _dot(merged.astype(BF16), wo_ref[...])
    x1_ref[...] = x1
    h2 = _rms(x1) * n2_ref[...]
    hh, hl = _split2(h2)
    lg_ref[...] = _dot(hh, rwh_ref[...]) + _dot(hl, rwh_ref[...]) + _dot(hh, rwl_ref[...]) + rb_ref[...]
    h2_ref[...] = _pack_bf16_pairs(hh)


def _merge(o_a, o_b, proj, x, wa, wb, wo, n2, rwh, rwl, rb, cols, tm):
    t, d = x.shape
    row = lambda w, off: pl.BlockSpec((tm, w), lambda i: (i, off // w))
    full = lambda a: pl.BlockSpec(a.shape, lambda i: (0,) * a.ndim, pipeline_mode=pl.Buffered(1))
    return pl.pallas_call(
        _merge_kernel,
        grid=(t // tm,),
        in_specs=[row(o_a.shape[1], 0), row(o_b.shape[1], 0), row(d, cols["gate_a"]), row(d, cols["gate_b"]),
                  row(d, 0), full(wa), full(wb), full(wo), full(n2), full(rwh), full(rwl), full(rb)],
        out_specs=[row(d, 0), row(d // 2, 0), row(LANES, 0)],
        out_shape=[jax.ShapeDtypeStruct((t, d), F32), jax.ShapeDtypeStruct((t, d // 2), jnp.uint32),
                   jax.ShapeDtypeStruct((t, LANES), F32)],
        compiler_params=_params(("parallel",)),
        name="merge_out_proj",
    )(o_a, o_b, proj, proj, x, wa, wb, wo, n2, rwh, rwl, rb)


def _router_kernel(lg_ref, idx_ref, w_ref, rank_ref, cnt_ref, carry_s):
    @pl.when(pl.program_id(0) == 0)
    def _():
        carry_s[...] = jnp.zeros(carry_s.shape, F32)

    l = lg_ref[...]
    tm = l.shape[0]
    lane = lax.broadcasted_iota(jnp.int32, l.shape, 1).astype(F32)
    vals, idxs = [], []
    for _ in range(MOE_TOPK):
        mx = jnp.max(l, axis=-1, keepdims=True)
        ix = jnp.min(jnp.where(l == mx, lane, float(LANES)), axis=-1, keepdims=True)
        vals.append(mx)
        idxs.append(ix)
        l = jnp.where(lane == ix, -jnp.inf, l)
    es = [jnp.exp(v - vals[0]) for v in vals]
    tot = es[0]
    for e in es[1:]:
        tot = tot + e
    onehot = jnp.zeros(l.shape, F32)
    for ix in idxs:
        onehot = onehot + jnp.where(lane == ix, 1.0, 0.0)
    r = lax.broadcasted_iota(jnp.int32, (tm, tm), 0)
    c = lax.broadcasted_iota(jnp.int32, (tm, tm), 1)
    before = _dot((c < r).astype(BF16), onehot.astype(BF16)) + carry_s[...]
    idx_o = jnp.zeros(l.shape, F32)
    w_o = jnp.zeros(l.shape, F32)
    rank_o = jnp.zeros(l.shape, F32)
    for k in range(MOE_TOPK):
        rk = jnp.sum(jnp.where(lane == idxs[k], before, 0.0), axis=-1, keepdims=True)
        idx_o = jnp.where(lane == k, idxs[k], idx_o)
        w_o = jnp.where(lane == k, es[k] / tot, w_o)
        rank_o = jnp.where(lane == k, rk, rank_o)
    idx_ref[...] = idx_o.astype(jnp.int32)
    w_ref[...] = w_o
    rank_ref[...] = rank_o.astype(jnp.int32)
    carry_s[...] = carry_s[...] + jnp.sum(onehot, axis=0, keepdims=True)
    cnt_ref[...] = carry_s[...].astype(jnp.int32)


def _router(logits, tm):
    t = logits.shape[0]
    row = pl.BlockSpec((tm, LANES), lambda i: (i, 0))
    return pl.pallas_call(
        _router_kernel,
        grid=(t // tm,),
        in_specs=[row],
        out_specs=[row, row, row, pl.BlockSpec((1, LANES), lambda i: (0, 0))],
        out_shape=[jax.ShapeDtypeStruct((t, LANES), jnp.int32), jax.ShapeDtypeStruct((t, LANES), F32),
                   jax.ShapeDtypeStruct((t, LANES), jnp.int32), jax.ShapeDtypeStruct((1, LANES), jnp.int32)],
        scratch_shapes=[pltpu.VMEM((1, LANES), F32)],
        compiler_params=_params(("arbitrary",)),
        name="router",
    )(logits)


def _scatter_rows(dest_ref, base, x_ref, xs_hbm, sem):
    groups = x_ref.shape[0]
    per_group = SUBLANES * MOE_TOPK

    def issue(g, carry):
        for s in range(SUBLANES):
            for k in range(MOE_TOPK):
                dst = dest_ref[base + g * per_group + s * MOE_TOPK + k]
                pltpu.make_async_copy(x_ref.at[g, pl.ds(s, 1), :], xs_hbm.at[pl.ds(dst, 1), :],
                                      sem).start(priority=k % 2)
        return carry

    lax.fori_loop(0, groups, issue, 0)
    for _ in range(per_group):
        pltpu.make_async_copy(x_ref.at[:, 0, :], xs_hbm.at[pl.ds(0, groups), :], sem).wait()


def _dispatch_kernel(dest_ref, x_ref, xs_in_hbm, xs_hbm, sem):
    del xs_in_hbm
    tile_copies = x_ref.shape[0] * SUBLANES * MOE_TOPK
    _scatter_rows(dest_ref, pl.program_id(0) * tile_copies, x_ref, xs_hbm, sem)


def _dispatch(dest, x_packed, xs, tm):
    t, w = x_packed.shape
    x3 = x_packed.reshape(t // SUBLANES, SUBLANES, w)
    hbm = pl.BlockSpec(memory_space=pl.ANY)
    return pl.pallas_call(
        _dispatch_kernel,
        grid_spec=pltpu.PrefetchScalarGridSpec(
            num_scalar_prefetch=1,
            grid=(t // tm,),
            in_specs=[pl.BlockSpec((tm // SUBLANES, SUBLANES, w), lambda i, dst: (i, 0, 0)), hbm],
            out_specs=hbm,
            scratch_shapes=[pltpu.SemaphoreType.DMA(())]),
        out_shape=jax.ShapeDtypeStruct(xs.shape, xs.dtype),
        input_output_aliases={2: 0},
        compiler_params=_params(("arbitrary",)),
        name="moe_dispatch",
    )(dest, x3, xs)


def _stream_expert_weights(first_ref, nxt_ref, be_ref, w_hbms, stages, casts, sems, tn):
    j, i = pl.program_id(0), pl.program_id(1)
    nj = pl.num_programs(0)

    def copies(e, jj):
        c0 = pl.multiple_of(jj * tn, tn)
        return [pltpu.make_async_copy(w.at[e, :, pl.ds(c0, tn)], st, sems.at[n])
                for n, (w, st) in enumerate(zip(w_hbms, stages))]

    @pl.when(first_ref[i] == 1)
    def _():
        @pl.when((j == 0) & (i == 0))
        def _():
            for c in copies(be_ref[0], 0):
                c.start()

        for c in copies(be_ref[i], j):
            c.wait()
        for st, cs in zip(stages, casts):
            cs[...] = st[...].astype(BF16)
        nxt = nxt_ref[i]

        @pl.when(nxt >= 0)
        def _():
            for c in copies(nxt, j):
                c.start()

        @pl.when((nxt < 0) & (j + 1 < nj))
        def _():
            for c in copies(be_ref[0], j + 1):
                c.start()


def _expert_up_kernel(be_ref, first_ref, nxt_ref, nu_ref, x_ref, wg_hbm, wu_hbm, bg_ref, bu_ref, o_ref,
                      stg_g, stg_u, wg_s, wu_s, sems):
    _stream_expert_weights(first_ref, nxt_ref, be_ref, (wg_hbm, wu_hbm), (stg_g, stg_u), (wg_s, wu_s), sems,
                           o_ref.shape[1])
    used = pl.program_id(1) < nu_ref[0]

    @pl.when(used)
    def _():
        x = _unpack_bf16_pairs(x_ref[...])
        tf = o_ref.shape[1]
        half = tf // 2
        e = be_ref[pl.program_id(1)]
        c0 = pl.multiple_of(pl.program_id(0) * tf, tf)
        bg = bg_ref[e, :, pl.ds(c0, tf)]
        bu = bu_ref[e, :, pl.ds(c0, tf)]
        for cs in (slice(0, half), slice(half, 2 * half)):
            gate = jnp.minimum(_dot(x, wg_s[:, cs]) + bg[:, cs], SWIGLU_LIMIT)
            up = jnp.clip(_dot(x, wu_s[:, cs]) + bu[:, cs], -SWIGLU_LIMIT, SWIGLU_LIMIT)
            o_ref[:, cs] = ((up + 1.0) * (gate * jax.nn.sigmoid(SWIGLU_ALPHA * gate))).astype(BF16)

    @pl.when(jnp.logical_not(used))
    def _():
        o_ref[...] = jnp.zeros(o_ref.shape, o_ref.dtype)


def _expert_down_kernel(be_ref, first_ref, nxt_ref, nu_ref, h_ref, wd_hbm, bd_ref, o_ref, stg, wd_s, sems):
    _stream_expert_weights(first_ref, nxt_ref, be_ref, (wd_hbm,), (stg,), (wd_s,), sems, wd_s.shape[1])
    used = pl.program_id(1) < nu_ref[0]

    @pl.when(used)
    def _():
        bd = bd_ref[be_ref[pl.program_id(1)]]
        o_ref[...] = _pack_bf16_pairs((_dot(h_ref[...], wd_s[...]) + bd).astype(BF16))

    @pl.when(jnp.logical_not(used))
    def _():
        o_ref[...] = jnp.zeros(o_ref.shape, o_ref.dtype)


def _used_block(i, nu):
    return jnp.minimum(i, nu[0] - 1)


def _expert_up(meta, xs, wg, wu, bg, bu, bm, tf):
    p, half = xs.shape
    d, f = wg.shape[1], wg.shape[2]
    bspec = pl.BlockSpec(bg.shape, lambda j, i, be, fi, nx, nu: (0, 0, 0))
    hbm = pl.BlockSpec(memory_space=pl.ANY)
    return pl.pallas_call(
        _expert_up_kernel,
        grid_spec=pltpu.PrefetchScalarGridSpec(
            num_scalar_prefetch=4,
            grid=(f // tf, p // bm),
            in_specs=[pl.BlockSpec((bm, half), lambda j, i, be, fi, nx, nu: (_used_block(i, nu), 0)),
                      hbm, hbm, bspec, bspec],
            out_specs=pl.BlockSpec((bm, tf), lambda j, i, be, fi, nx, nu: (i, j)),
            scratch_shapes=[pltpu.VMEM((d, tf), F32), pltpu.VMEM((d, tf), F32),
                            pltpu.VMEM((d, tf), BF16), pltpu.VMEM((d, tf), BF16),
                            pltpu.SemaphoreType.DMA((2,))]),
        out_shape=jax.ShapeDtypeStruct((p, f), BF16),
        compiler_params=_params(("arbitrary", "arbitrary")),
        name="moe_up",
    )(*meta, xs, wg, wu, bg, bu)


def _expert_down(meta, hid, wd, bd, bm):
    p, f = hid.shape
    d = wd.shape[2]
    return pl.pallas_call(
        _expert_down_kernel,
        grid_spec=pltpu.PrefetchScalarGridSpec(
            num_scalar_prefetch=4,
            grid=(1, p // bm),
            in_specs=[pl.BlockSpec((bm, f), lambda j, i, be, fi, nx, nu: (_used_block(i, nu), 0)),
                      pl.BlockSpec(memory_space=pl.ANY),
                      pl.BlockSpec(bd.shape, lambda j, i, be, fi, nx, nu: (0, 0, 0))],
            out_specs=pl.BlockSpec((bm, d // 2), lambda j, i, be, fi, nx, nu: (i, 0)),
            scratch_shapes=[pltpu.VMEM((f, d), F32), pltpu.VMEM((f, d), BF16), pltpu.SemaphoreType.DMA((1,))]),
        out_shape=jax.ShapeDtypeStruct((p, d // 2), jnp.uint32),
        compiler_params=_params(("arbitrary", "arbitrary")),
        name="moe_down",
    )(*meta, hid, wd, bd)


def _combine_kernel(dest_ref, ys_hbm, x1_ref, w_ref, p_ref, pw_ref, pg_ref, gw_ref, y_ref, buf, sems):
    tm = x1_ref.shape[0]
    groups = tm // SUBLANES
    per_group = SUBLANES * MOE_TOPK
    i = pl.program_id(0)
    slot = i % 2

    def issue_group(tile, sl, g):
        base = tile * tm * MOE_TOPK
        for s in range(SUBLANES):
            for k in range(MOE_TOPK):
                dst = dest_ref[base + g * per_group + s * MOE_TOPK + k]
                pltpu.make_async_copy(ys_hbm.at[pl.ds(dst, 1), :], buf.at[sl, k, g, pl.ds(s, 1), :],
                                      sems.at[sl]).start(priority=k % 2)

    def wait_slot(sl):
        for k in range(MOE_TOPK):
            for s in range(SUBLANES):
                pltpu.make_async_copy(ys_hbm.at[pl.ds(0, groups), :], buf.at[sl, k, :, s, :], sems.at[sl]).wait()

    last = pl.num_programs(0) - 1

    @pl.when(i == 0)
    def _():
        lax.fori_loop(0, groups, lambda g, c: (issue_group(0, 0, g), c)[1], 0)

    nxt_tile = jnp.minimum(i + 1, last)
    d = y_ref.shape[1]
    n_col = d // (2 * LANES)
    tc = d // n_col
    n_phase = 2 + MOE_TOPK + n_col
    issued = [0]

    def issue_share(phase):
        upto = groups * (phase + 1) // n_phase
        for g in range(issued[0], upto):
            issue_group(nxt_tile, 1 - slot, g)
        issued[0] = upto

    issue_share(0)
    e = _rms(_dot(p_ref[...].astype(BF16), pw_ref[...])) * pg_ref[...]
    issue_share(1)
    wait_slot(slot)
    w = w_ref[...]
    x2 = x1_ref[...]
    for k in range(MOE_TOPK):
        u = buf[slot, k].reshape(tm, buf.shape[-1])
        rows = jnp.concatenate([lax.bitcast_convert_type(u << 16, F32),
                                lax.bitcast_convert_type(u & jnp.uint32(0xFFFF0000), F32)], axis=1)
        x2 = x2 + w[:, k:k + 1] * rows
        issue_share(2 + k)
    xn = _rms(x2).astype(BF16)
    for t in range(n_col):
        cs = slice(t * tc, (t + 1) * tc)
        gate = jax.nn.sigmoid(_dot(xn, gw_ref[:, cs]))
        y_ref[:, cs] = x2[:, cs] + gate * e[:, cs]
        issue_share(2 + MOE_TOPK + t)

    @pl.when(i == last)
    def _():
        wait_slot(1 - slot)


def _combine(dest, ys, x1, w, p, pw, pg, gw, tm):
    t, d = x1.shape
    full = lambda a: pl.BlockSpec(a.shape, lambda i, dst: (0,) * a.ndim)
    row = lambda width: pl.BlockSpec((tm, width), lambda i, dst: (i, 0))
    return pl.pallas_call(
        _combine_kernel,
        grid_spec=pltpu.PrefetchScalarGridSpec(
            num_scalar_prefetch=1,
            grid=(t // tm,),
            in_specs=[pl.BlockSpec(memory_space=pl.ANY), row(d), row(LANES), row(p.shape[1]),
                      full(pw), full(pg), full(gw)],
            out_specs=row(d),
            scratch_shapes=[pltpu.VMEM((2, MOE_TOPK, tm // SUBLANES, SUBLANES, d // 2), jnp.uint32),
                            pltpu.SemaphoreType.DMA((2,))]),
        out_shape=jax.ShapeDtypeStruct((t, d), F32),
        compiler_params=_params(("arbitrary",)),
        name="moe_combine_ple",
    )(dest, ys, x1, w, p, pw, pg, gw)


def _largest_tile(n, cap, quantum):
    best = None
    for t in range(quantum, cap + 1, quantum):
        if n % t == 0:
            best = t
    assert best is not None, (n, cap, quantum)
    return best


def kernel(x_prompt, x_sample, cache_k, cache_v, state_gla, page_table, p_prompt, p_sample, norm1_g, w_in, q_norm_g, k_norm_g, gla_w_a2, gla_b_a, gla_norm_g, w_branch_a, w_branch_b, w_out, norm2_g, router_w, router_b, moe_w_gate, moe_b_gate, moe_w_up, moe_b_up, moe_w_down, moe_b_down, ple_w, ple_norm_g, ple_gate_w):
    assert norm1_g.shape[0] == 1, "one layer"
    b, s, d = x_prompt.shape
    nseq = x_sample.shape[0]
    assert x_sample.shape[1] == 1
    n_heads = d // (2 * HEAD_DIM_A)
    wa = n_heads * HEAD_DIM_A
    kb_w = N_HEADS_B * LANES
    vb_w = 2 * kb_w
    n_exp = router_w.shape[2]
    tp, ts = b * s, nseq

    sizes = [("q_a", wa), ("k_a", wa), ("v_a", wa), ("q_b", kb_w), ("k_b", kb_w), ("v_b", vb_w), ("g_b", vb_w),
             ("a_lr", GLA_LOWRANK), ("gate_a", d), ("gate_b", d)]
    cols, src, off = {}, 0, 0
    w_in0 = w_in[0]
    for name, width in sizes:
        if name == "a_lr":
            w_side = jnp.pad(w_in0[:, src:src + width], ((0, 0), (0, LANES - width))).astype(BF16)
            split = src
        else:
            cols[name] = off
            off += width
        src += width
    n_main = off
    tn = _largest_tile(n_main, 1024, 2 * LANES)
    w_main = jnp.concatenate([w_in0[:, :split], w_in0[:, split + GLA_LOWRANK:]], axis=1).astype(BF16)

    g1 = norm1_g[0][None, :]
    qg, kg = q_norm_g[0][None, :], k_norm_g[0][None, :]
    wa2 = jnp.pad(gla_w_a2[0], ((0, LANES - GLA_LOWRANK), (0, 0)))
    ba = gla_b_a[0][None, :]
    gn = gla_norm_g[0][None, :]
    wba, wbb, wo = w_branch_a[0].astype(BF16), w_branch_b[0].astype(BF16), w_out[0].astype(BF16)
    n2 = norm2_g[0][None, :]
    rw = jnp.pad(router_w[0], ((0, 0), (0, LANES - n_exp)))
    rwh = rw.astype(BF16)
    rwl = (rw - rwh.astype(F32)).astype(BF16)
    rb = jnp.concatenate([router_b[0], jnp.full((LANES - n_exp,), -jnp.inf, F32)])[None, :]

    xp = x_prompt.reshape(tp, d)
    proj_p, alr_p = _norm_matmul(xp, g1, w_main, w_side, tn)
    tr = _largest_tile(s, 512, 8)
    tabs_p = _rope_tables(jnp.arange(s, dtype=jnp.int32))
    q_p, k_p, v_p = _qk_rope(proj_p, qg, kg, tabs_p, n_heads, tr, lambda i: i % (s // tr))
    o_a_p = _moba_prompt(q_p.reshape(b, s, wa), k_p.reshape(b, s, wa), v_p.reshape(b, s, wa), n_heads)
    o_b_p, gla_p = _gla_prompt(proj_p.reshape(b, s, n_main), alr_p.reshape(b, s, LANES), wa2, ba, gn, cols)
    x1_p, h2_p, lg_p = _merge(o_a_p.reshape(tp, wa), o_b_p.reshape(tp, vb_w), proj_p, xp, wba, wbb, wo, n2,
                              rwh, rwl, rb, cols, _largest_tile(tp, 256, 8))

    xs_ = x_sample.reshape(ts, d)
    past = page_table.shape[1] * cache_k.shape[2]
    proj_s, alr_s = _norm_matmul(xs_, g1, w_main, w_side, tn)
    tabs_s = tuple(jnp.broadcast_to(t_, (ts, HEAD_DIM_A)) for t_ in _rope_tables(jnp.full((1,), past, jnp.int32)))
    q_s, k_s, v_s = _qk_rope(proj_s, qg, kg, tabs_s, n_heads, ts, lambda i: i)
    o_a_s = _moba_decode(q_s, k_s, v_s, cache_k[0], cache_v[0], page_table, n_heads)
    o_b_s, gla_s = _gla_decode(proj_s, alr_s, wa2, ba, gn, state_gla[0], cols)
    x1_s, h2_s, lg_s = _merge(o_a_s, o_b_s, proj_s, xs_, wba, wbb, wo, n2, rwh, rwl, rb, cols, ts)

    t_all = tp + ts
    logits = jnp.concatenate([lg_p, lg_s], axis=0)
    idx_o, w_o, rank_o, cnt = _router(logits, _largest_tile(t_all, 512, LANES))
    bm = MOE_ROW_BLOCK
    n_assign = t_all * MOE_TOPK
    n_blocks = -(-n_assign // bm) + n_exp
    counts = cnt[0, :n_exp]
    pcounts = (counts + bm - 1) // bm * bm
    pend = jnp.cumsum(pcounts)
    pstart = pend - pcounts
    expert_ids = jnp.arange(n_exp, dtype=jnp.int32)
    start_of = jnp.sum(jnp.where(idx_o[:, :MOE_TOPK, None] == expert_ids, pstart, 0), axis=-1)
    dest = (start_of + rank_o[:, :MOE_TOPK]).reshape(-1)
    n_used = pend[n_exp - 1] // bm
    blk = jnp.arange(n_blocks, dtype=jnp.int32)
    blk = jnp.minimum(blk, n_used - 1)
    block_e = jnp.sum(pend[None, :] <= (blk * bm)[:, None], axis=1).astype(jnp.int32)
    first = jnp.concatenate([jnp.ones((1,), jnp.int32), (block_e[1:] != block_e[:-1]).astype(jnp.int32)])
    nxt_pos = jnp.sum(block_e[None, :] <= block_e[:, None], axis=1)
    nxt = jnp.where(nxt_pos < n_blocks, block_e[jnp.minimum(nxt_pos, n_blocks - 1)], -1).astype(jnp.int32)
    meta = (block_e, first, nxt, n_used.reshape(1).astype(jnp.int32))

    xs_rows = jnp.zeros((n_blocks * bm, d // 2), jnp.uint32)
    xs_rows = _dispatch(dest[:tp * MOE_TOPK], h2_p, xs_rows, _largest_tile(tp, 512, 8))
    xs_rows = _dispatch(dest[tp * MOE_TOPK:], h2_s, xs_rows, ts)
    f = moe_w_gate.shape[3]
    hid = _expert_up(meta, xs_rows, moe_w_gate[0], moe_w_up[0], moe_b_gate[0][:, None, :],
                     moe_b_up[0][:, None, :], bm, _largest_tile(f, 1024, 2 * LANES))
    ys = _expert_down(meta, hid, moe_w_down[0], moe_b_down[0][:, None, :], bm)

    pw, pg, gw = ple_w[0].astype(BF16), ple_norm_g[0][None, :], ple_gate_w[0].astype(BF16)
    y_p = _combine(dest[:tp * MOE_TOPK], ys, x1_p, w_o[:tp], p_prompt[0].reshape(tp, -1), pw, pg, gw,
                   _largest_tile(tp, 256, 8 * SUBLANES))
    y_s = _combine(dest[tp * MOE_TOPK:], ys, x1_s, w_o[tp:], p_sample[0].reshape(ts, -1), pw, pg, gw,
                   _largest_tile(ts, 128, 8 * SUBLANES))

    hd = (n_heads, HEAD_DIM_A)
    return (y_p.reshape(b, s, d), y_s.reshape(nseq, 1, d),
            k_p.reshape(1, b, s, *hd), v_p.reshape(1, b, s, *hd), gla_p[None],
            k_s.reshape(1, nseq, 1, *hd), v_s.reshape(1, nseq, 1, *hd), gla_s[None])
```
